```python
import math
import jax
import jax.numpy as jnp
from jax import lax
import numpy as np

D_MODEL = 1024
BATCH = 2
SEQ = 8192
DEPTH = 2

CHUNK = 64
CONV_K = 4
GDN_HEADS = 4
GDN_DK = 128
GDN_DV = 128
MLSTM_HEADS = 4
MLSTM_DQK = 64
MLSTM_DV = 128
MLSTM_SOFTCAP = 15.0
RWKV_HEADS = 8
RWKV_DH = 64
RWKV_WIDTH = RWKV_HEADS * RWKV_DH
LORA_W = 64
LORA_A = 64
LORA_V = 32
LORA_G = 128
N_BRANCH = 3
BRANCH_WIDTH = 512
D_FF = 2816
N_EXPERTS = 8
TOP_K = 2
D_FF_EXPERT = 3584
MOE_BLOCK = 256
DEEPNORM_ALPHA = (2 * DEPTH) ** 0.25
DEEPNORM_BETA = (8 * DEPTH) ** -0.25
LN_EPS = 1e-5
RWKV_GN_EPS = 64e-5

GDN_SIZES = (GDN_HEADS * GDN_DK, GDN_HEADS * GDN_DK, GDN_HEADS * GDN_DV, GDN_HEADS * GDN_DV, GDN_HEADS, GDN_HEADS)
MLSTM_SIZES = (MLSTM_HEADS * MLSTM_DQK, MLSTM_HEADS * MLSTM_DQK, MLSTM_HEADS * MLSTM_DV, MLSTM_HEADS * MLSTM_DV, MLSTM_HEADS, MLSTM_HEADS)
RWKV_SIZES = (RWKV_WIDTH, RWKV_WIDTH, RWKV_WIDTH, LORA_W, LORA_A, LORA_G)
GATE_SIZES = (D_MODEL,) * N_BRANCH
IN_GROUPS = (sum(GDN_SIZES), sum(MLSTM_SIZES), sum(RWKV_SIZES), sum(GATE_SIZES))
IN_WIDTH = sum(IN_GROUPS)

kernel_name = 'hybrid_gdn_mlstm_rwkv7_moe_deepnorm'


def _split(z, sizes):
    return jnp.split(z, np.cumsum(sizes)[:-1].tolist(), axis=-1)


def _layer_norm(x, g, b):
    xf = x.astype(jnp.float32)
    mu = jnp.mean(xf, -1, keepdims=True)
    var = jnp.mean(jnp.square(xf - mu), -1, keepdims=True)
    return ((xf - mu) * lax.rsqrt(var + LN_EPS)).astype(x.dtype) * g + b


def _head_split(z, n_heads):
    return z.reshape(z.shape[:-1] + (n_heads, z.shape[-1] // n_heads))


def _head_rms(z, n_heads, eps=1e-6):
    zh = _head_split(z, n_heads)
    return zh * lax.rsqrt(jnp.mean(zh * zh, -1, keepdims=True) + eps)


def _l2norm(z, eps=1e-6):
    return z * lax.rsqrt(jnp.sum(z * z, -1, keepdims=True) + eps)


def _to_chunks(z, n_heads):
    b, t, hd = z.shape
    return z.reshape(b, t // CHUNK, CHUNK, n_heads, hd // n_heads).transpose(0, 3, 1, 2, 4)


def _scalar_chunks(s):
    b, t, h = s.shape
    return s.reshape(b, t // CHUNK, CHUNK, h).transpose(0, 3, 1, 2)


def _from_chunks(z):
    b, h, nc, c, d = z.shape
    return z.transpose(0, 2, 3, 1, 4).reshape(b, nc * c, h * d)


def _causal_masks():
    incl = jnp.tril(jnp.ones((CHUNK, CHUNK), bool))
    return incl, jnp.tril(incl, -1)


def _unit_lower_inverse(a):
    eye = jnp.broadcast_to(jnp.eye(CHUNK, dtype=a.dtype), a.shape)
    return lax.linalg.triangular_solve(eye + a, eye, left_side=True, lower=True, unit_diagonal=True)


def _chunk_scan(step, init, xs):
    _, ys = lax.scan(step, init, tuple(jnp.moveaxis(t, 2, 0) for t in xs))
    return jnp.moveaxis(ys, 0, 2)


def _causal_conv(z, w):
    return lax.conv_general_dilated(z, w[:, None, :].astype(z.dtype), window_strides=(1,), padding=((CONV_K - 1, 0),),
                                    dimension_numbers=('NWC', 'WIO', 'NWC'), feature_group_count=z.shape[-1])


def _token_shift(z, mu):
    prev = jnp.pad(z, ((0, 0), (1, 0), (0, 0)))[:, :-1]
    return z + mu * (prev - z)


def _swiglu(h, w_gate, w_up, w_down):
    return (jax.nn.silu(h @ w_gate) * (h @ w_up)) @ w_down


def _gated_deltanet(q, k, v, z, a, b, conv_w, a_log, dt_bias, norm_g):
    f32 = jnp.float32
    hh = GDN_HEADS
    qkv = jax.nn.silu(_causal_conv(jnp.concatenate([q, k, v], -1), conv_w)).astype(f32)
    q, k, v = _split(qkv, GDN_SIZES[:3])
    q = _l2norm(_to_chunks(q, hh)) * GDN_DK ** -0.5
    k = _l2norm(_to_chunks(k, hh))
    v = _to_chunks(v, hh)
    g = _scalar_chunks(-jnp.exp(a_log.astype(f32)) * jax.nn.softplus(a.astype(f32) + dt_bias.astype(f32)))
    beta = _scalar_chunks(jax.nn.sigmoid(b.astype(f32)))
    g = jnp.cumsum(g, -1)
    incl, strict = _causal_masks()
    decay = jnp.exp(jnp.where(incl, g[..., :, None] - g[..., None, :], -jnp.inf))
    kb = k * beta[..., None]
    a_kk = jnp.where(strict, jnp.einsum('bhnck,bhnsk->bhncs', kb, k) * decay, 0.0)
    t_inv = _unit_lower_inverse(a_kk)
    u = jnp.einsum('bhncs,bhnsv->bhncv', t_inv, v * beta[..., None])
    w = jnp.einsum('bhncs,bhnsk->bhnck', t_inv, kb * jnp.exp(g)[..., None])
    qk = jnp.einsum('bhnck,bhnsk->bhncs', q, k) * decay
    q_dec = q * jnp.exp(g)[..., None]
    k_dec = k * jnp.exp(g[..., -1:] - g)[..., None]
    g_last = jnp.exp(g[..., -1])

    def step(s, inp):
        qd, wc, uc, qkc, kd, gl = inp
        v_new = uc - jnp.einsum('bhck,bhkv->bhcv', wc, s)
        o = jnp.einsum('bhck,bhkv->bhcv', qd, s) + jnp.einsum('bhcs,bhsv->bhcv', qkc, v_new)
        s = s * gl[..., None, None] + jnp.einsum('bhck,bhcv->bhkv', kd, v_new)
        return s, o

    s0 = jnp.zeros(q.shape[:2] + (GDN_DK, GDN_DV), f32)
    o = _chunk_scan(step, s0, (q_dec, w, u, qk, k_dec, g_last))
    o = _head_rms(_from_chunks(o), hh) * norm_g.astype(f32)
    return (o.reshape(o.shape[:2] + (-1,)) * jax.nn.silu(z.astype(f32))).astype(z.dtype)


def _mlstm(q, k, v, o_pre, i_pre, f_pre, b_i, b_f, norm_g):
    f32 = jnp.float32
    hh = MLSTM_HEADS
    q = _to_chunks(q.astype(f32), hh)
    k = _to_chunks(k.astype(f32), hh) * MLSTM_DQK ** -0.5
    v = _to_chunks(v.astype(f32), hh)
    cap = lambda t: MLSTM_SOFTCAP * jnp.tanh(t / MLSTM_SOFTCAP)
    log_i = _scalar_chunks(cap(i_pre.astype(f32) + b_i.astype(f32)))
    log_f = _scalar_chunks(jax.nn.log_sigmoid(cap(f_pre.astype(f32) + b_f.astype(f32))))
    bcum = jnp.cumsum(log_f, -1)
    incl, _ = _causal_masks()
    log_d = jnp.where(incl, bcum[..., :, None] - bcum[..., None, :] + log_i[..., None, :], -jnp.inf)
    m_intra = jnp.max(log_d, -1)
    p = jnp.exp(log_d - m_intra[..., None]) * jnp.einsum('bhnck,bhnsk->bhncs', q, k)
    num_intra = jnp.einsum('bhncs,bhnsv->bhncv', p, v)
    den_intra = jnp.sum(p, -1)
    log_e = bcum[..., -1:] - bcum + log_i
    m_end = jnp.max(log_e, -1)
    k_w = k * jnp.exp(log_e - m_end[..., None])[..., None]
    b_last = bcum[..., -1]

    def step(carry, inp):
        cs, ns, m = carry
        qc, bc, mi, numi, deni, kw, vc, me, bl = inp
        m_t = jnp.maximum(bc + m[..., None], mi)
        s_inter = jnp.exp(bc + m[..., None] - m_t)
        s_intra = jnp.exp(mi - m_t)
        num = s_inter[..., None] * jnp.einsum('bhck,bhkv->bhcv', qc, cs) + s_intra[..., None] * numi
        den = s_inter * jnp.einsum('bhck,bhk->bhc', qc, ns) + s_intra * deni
        hc = num / jnp.maximum(jnp.abs(den), jnp.exp(-m_t))[..., None]
        m_new = jnp.maximum(bl + m, me)
        s_old = jnp.exp(bl + m - m_new)
        s_new = jnp.exp(me - m_new)
        cs = s_old[..., None, None] * cs + s_new[..., None, None] * jnp.einsum('bhck,bhcv->bhkv', kw, vc)
        ns = s_old[..., None] * ns + s_new[..., None] * jnp.sum(kw, -2)
        return (cs, ns, m_new), hc

    bh = q.shape[:2]
    init = (jnp.zeros(bh + (MLSTM_DQK, MLSTM_DV), f32), jnp.zeros(bh + (MLSTM_DQK,), f32), jnp.zeros(bh, f32))
    hs = _chunk_scan(step, init, (q, bcum, m_intra, num_intra, den_intra, k_w, v, m_end, b_last))
    hs = _head_rms(_from_chunks(hs), hh) * norm_g.astype(f32).reshape(hh, MLSTM_DV)
    return (hs.reshape(hs.shape[:2] + (-1,)) * jax.nn.sigmoid(o_pre.astype(f32))).astype(o_pre.dtype)


def _rwkv7(r, k, v, x_w, x_a, x_g, w0, w2, a0, a2, g2, k_k, k_a, r_k, lnx_g, lnx_b, v_first, v_mix):
    f32 = jnp.float32
    hh = RWKV_HEADS
    r, k, v = r.astype(f32), k.astype(f32), v.astype(f32)
    log_w = -jnp.exp(-jax.nn.softplus(-(w0.astype(f32) + jnp.tanh(x_w.astype(f32)) @ w2.astype(f32))) - 0.5)
    a = jax.nn.sigmoid(a0.astype(f32) + x_a.astype(f32) @ a2.astype(f32))
    if v_mix is None:
        v_first = v
    else:
        x_v, v0, v2 = v_mix
        v = v + (v_first - v) * jax.nn.sigmoid(v0.astype(f32) + x_v.astype(f32) @ v2.astype(f32))
    g = jax.nn.sigmoid(x_g.astype(f32)) @ g2.astype(f32)
    kk = _l2norm(_head_split(k * k_k.astype(f32), hh)).reshape(k.shape)
    k = k * (1.0 + (a - 1.0) * k_a.astype(f32))
    rc, kc, vc, kkc, ac, lwc = (_to_chunks(t, hh) for t in (r, k, v, kk, a, log_w))
    alpha, beta = -kkc, kkc * ac
    lw = jnp.cumsum(lwc, 3)
    lw_ex = lw - lwc
    lw_last = lw[..., -1:, :]
    inv = jnp.exp(-lw)
    al_bar = alpha * jnp.exp(lw_ex)
    be_hat, k_hat = beta * inv, kc * inv
    r_bar = rc * jnp.exp(lw)
    incl, strict = _causal_masks()
    dots = lambda p, q: jnp.einsum('bhnck,bhnsk->bhncs', p, q)
    mix = lambda m, z: jnp.einsum('bhncs,bhnsd->bhncd', m, z)
    l_ab = jnp.where(strict, dots(al_bar, be_hat), 0.0)
    l_ak = jnp.where(strict, dots(al_bar, k_hat), 0.0)
    q_rb = jnp.where(incl, dots(r_bar, be_hat), 0.0)
    q_rk = jnp.where(incl, dots(r_bar, k_hat), 0.0)
    t_inv = _unit_lower_inverse(-l_ab)
    w_u = mix(t_inv, al_bar)
    u0 = mix(t_inv, mix(l_ak, vc))
    y_k = mix(q_rk, vc)
    to_end = jnp.exp(lw_last - lw)
    be_end, k_end = beta * to_end, kc * to_end
    g_chunk = jnp.exp(lw_last[..., 0, :])

    def step(s, inp):
        wu, u0c, rb, qrb, ykc, bec, kec, vcc, gc = inp
        u = jnp.einsum('bhck,bhkv->bhcv', wu, s) + u0c
        y = jnp.einsum('bhck,bhkv->bhcv', rb, s) + jnp.einsum('bhcs,bhsv->bhcv', qrb, u) + ykc
        s = gc[..., :, None] * s + jnp.einsum('bhck,bhcv->bhkv', bec, u) + jnp.einsum('bhck,bhcv->bhkv', kec, vcc)
        return s, y

    s0 = jnp.zeros(rc.shape[:2] + (RWKV_DH, RWKV_DH), f32)
    y = _from_chunks(_chunk_scan(step, s0, (w_u, u0, r_bar, q_rb, y_k, be_end, k_end, vc, g_chunk)))
    yh = _head_split(y, hh)
    mu = jnp.mean(yh, -1, keepdims=True)
    var = jnp.mean(jnp.square(yh - mu), -1, keepdims=True)
    y = ((yh - mu) * lax.rsqrt(var + RWKV_GN_EPS)).reshape(y.shape) * lnx_g.astype(f32) + lnx_b.astype(f32)
    bonus = jnp.sum(_head_split(r * k * r_k.astype(f32).reshape(-1), hh), -1, keepdims=True) * _head_split(v, hh)
    y = (y + bonus.reshape(y.shape)) * g
    return y.astype(x_w.dtype), v_first


def _mixer_sublayer(h, w_in, gdn_conv, gdn_a_log, gdn_dt_bias, gdn_norm_g, mlstm_b_i, mlstm_b_f, mlstm_norm_g,
                    rwkv_mu, rwkv_w0, rwkv_w2, rwkv_a0, rwkv_a2, rwkv_g2, rwkv_k_k, rwkv_k_a, rwkv_r_k,
                    rwkv_lnx_g, rwkv_lnx_b, w_branch, w_o, v_first, v_res):
    w = w_in if v_res is None else jnp.concatenate([w_in, v_res[0]], axis=1)
    z_all = h @ w
    z, z_v = z_all[..., :IN_WIDTH], z_all[..., IN_WIDTH:]
    z_gdn, z_mlstm, z_rwkv, z_gate = _split(z, IN_GROUPS)
    gq, gk, gv, gz, ga, gb = _split(z_gdn, GDN_SIZES)
    y_a = _gated_deltanet(gq, gk, gv, gz, ga, gb, gdn_conv, gdn_a_log, gdn_dt_bias, gdn_norm_g)
    mq, mk, mv, mo, mi, mf = _split(z_mlstm, MLSTM_SIZES)
    y_b = _mlstm(mq, mk, mv, mo, mi, mf, mlstm_b_i, mlstm_b_f, mlstm_norm_g)
    rr, rk, rv, rw, ra, rg = _split(_token_shift(z_rwkv, rwkv_mu), RWKV_SIZES)
    v_mix = None if v_res is None else (_token_shift(z_v, v_res[1]), v_res[2], v_res[3])
    y_c, v_first = _rwkv7(rr, rk, rv, rw, ra, rg, rwkv_w0, rwkv_w2, rwkv_a0, rwkv_a2, rwkv_g2, rwkv_k_k, rwkv_k_a,
                          rwkv_r_k, rwkv_lnx_g, rwkv_lnx_b, v_first, v_mix)
    gate_a, gate_b, gate_c = _split(jax.nn.sigmoid(z_gate), GATE_SIZES)
    merged = gate_a * (y_a @ w_branch[0]) + gate_b * (y_b @ w_branch[1]) + gate_c * (y_c @ w_branch[2])
    return merged @ w_o, v_first


def _moe_swiglu(h, router_w, router_b, w_gate, w_up, w_down):
    b, t, d = h.shape
    n = b * t
    xf = h.reshape(n, d)
    logits = xf.astype(jnp.float32) @ router_w.astype(jnp.float32) + router_b.astype(jnp.float32)
    top_logit, top_e = lax.top_k(logits, TOP_K)
    gate = jax.nn.softmax(top_logit, axis=-1)
    n_assign = n * TOP_K
    e_flat = top_e.reshape(-1)
    tok_flat = jnp.arange(n_assign, dtype=jnp.int32) // TOP_K
    order = jnp.argsort(e_flat)
    e_sorted = e_flat[order]
    counts = jnp.zeros((N_EXPERTS,), jnp.int32).at[e_flat].add(1)
    padded = (counts + MOE_BLOCK - 1) // MOE_BLOCK * MOE_BLOCK
    pad_end = jnp.cumsum(padded)
    pad_start = pad_end - padded
    start = jnp.cumsum(counts) - counts
    dest = pad_start[e_sorted] + jnp.arange(n_assign, dtype=jnp.int32) - start[e_sorted]
    n_blocks = -(-n_assign // MOE_BLOCK) + N_EXPERTS
    rows = n_blocks * MOE_BLOCK
    row_tok = jnp.full((rows,), n, jnp.int32).at[dest].set(tok_flat[order])
    row_gate = jnp.zeros((rows,), h.dtype).at[dest].set(gate.reshape(-1)[order].astype(h.dtype))
    block_e = jnp.minimum(jnp.searchsorted(pad_end, jnp.arange(n_blocks, dtype=jnp.int32) * MOE_BLOCK, side='right'), N_EXPERTS - 1)
    x_pad = jnp.concatenate([xf, jnp.zeros((1, d), xf.dtype)], 0)
    xb = x_pad[row_tok].reshape(n_blocks, MOE_BLOCK, d)

    def expert_block(args):
        xblk, e = args
        return _swiglu(xblk, w_gate[e], w_up[e], w_down[e])

    yb = lax.map(expert_block, (xb, block_e)).reshape(rows, d)
    y = jnp.zeros((n + 1, d), h.dtype).at[row_tok].add(yb * row_gate[:, None])
    return y[:n].reshape(b, t, d)


def setup_inputs(seed: int = 0) -> dict:
    key = jax.random.key(seed)
    keys = iter(jax.random.split(key, 64))
    f32 = jnp.float32

    def normal(shape, scale):
        return jax.random.normal(next(keys), shape, f32) * scale

    def uniform(shape, lo, hi):
        return jax.random.uniform(next(keys), shape, f32, lo, hi)

    d = D_MODEL
    n_even, n_odd, n_res = (DEPTH + 1) // 2, DEPTH // 2, DEPTH - 1
    dt = jnp.exp(uniform((DEPTH, GDN_HEADS), math.log(1e-3), math.log(1e-1)))
    return {
        'x': normal((BATCH, SEQ, d), 1.0),
        'c': normal((BATCH, d), 1.0),
        'ln_in_g': 1.0 + normal((d,), 0.02),
        'ln_in_b': normal((d,), 0.02),
        'ada_w': normal((DEPTH, d, 6 * d), 0.3 * d ** -0.5),
        'ada_b': normal((DEPTH, 6 * d), 0.02),
        'w_in': normal((DEPTH, d, IN_WIDTH), d ** -0.5),
        'gdn_conv': normal((DEPTH, CONV_K, sum(GDN_SIZES[:3])), CONV_K ** -0.5),
        'gdn_a_log': jnp.log(uniform((DEPTH, GDN_HEADS), 1.0, 16.0)),
        'gdn_dt_bias': dt + jnp.log(-jnp.expm1(-dt)),
        'gdn_norm_g': 1.0 + normal((DEPTH, GDN_DV), 0.02),
        'mlstm_b_i': normal((DEPTH, MLSTM_HEADS), 0.1),
        'mlstm_b_f': jnp.linspace(3.0, 6.0, MLSTM_HEADS, dtype=f32) + normal((DEPTH, MLSTM_HEADS), 0.1),
        'mlstm_norm_g': 1.0 + normal((DEPTH, MLSTM_HEADS * MLSTM_DV), 0.02),
        'rwkv_mu': uniform((DEPTH, sum(RWKV_SIZES)), 0.0, 1.0),
        'rwkv_w0': uniform((DEPTH, RWKV_WIDTH), -6.0, -1.0),
        'rwkv_w2': normal((DEPTH, LORA_W, RWKV_WIDTH), 0.1 * LORA_W ** -0.5),
        'rwkv_a0': normal((DEPTH, RWKV_WIDTH), 0.1),
        'rwkv_a2': normal((DEPTH, LORA_A, RWKV_WIDTH), 0.1 * LORA_A ** -0.5),
        'rwkv_g2': normal((DEPTH, LORA_G, RWKV_WIDTH), LORA_G ** -0.5),
        'rwkv_k_k': 0.85 + normal((DEPTH, RWKV_WIDTH), 0.05),
        'rwkv_k_a': 1.0 + normal((DEPTH, RWKV_WIDTH), 0.05),
        'rwkv_r_k': normal((DEPTH, RWKV_HEADS, RWKV_DH), 0.1),
        'rwkv_lnx_g': 1.0 + normal((DEPTH, RWKV_WIDTH), 0.02),
        'rwkv_lnx_b': normal((DEPTH, RWKV_WIDTH), 0.02),
        'rwkv_v1': normal((n_res, d, LORA_V), d ** -0.5),
        'rwkv_mu_v1': uniform((n_res, LORA_V), 0.0, 1.0),
        'rwkv_v0': 1.0 + normal((n_res, RWKV_WIDTH), 0.1),
        'rwkv_v2': normal((n_res, LORA_V, RWKV_WIDTH), 0.1 * LORA_V ** -0.5),
        'w_branch': normal((DEPTH, N_BRANCH, BRANCH_WIDTH, d), DEEPNORM_BETA * BRANCH_WIDTH ** -0.5),
        'w_o': normal((DEPTH, d, d), DEEPNORM_BETA * d ** -0.5),
        'ln1_g': 1.0 + normal((DEPTH, d), 0.02),
        'ln1_b': normal((DEPTH, d), 0.02),
        'ln2_g': 1.0 + normal((DEPTH, d), 0.02),
        'ln2_b': normal((DEPTH, d), 0.02),
        'ffn_w_gate': normal((n_even, d, D_FF), d ** -0.5),
        'ffn_w_up': normal((n_even, d, D_FF), d ** -0.5),
        'ffn_w_down': normal((n_even, D_FF, d), DEEPNORM_BETA * D_FF ** -0.5),
        'moe_router': normal((n_odd, d, N_EXPERTS), d ** -0.5),
        'moe_router_b': normal((n_odd, N_EXPERTS), 0.01),
        'moe_w_gate': normal((n_odd, N_EXPERTS, d, D_FF_EXPERT), d ** -0.5),
        'moe_w_up': normal((n_odd, N_EXPERTS, d, D_FF_EXPERT), d ** -0.5),
        'moe_w_down': normal((n_odd, N_EXPERTS, D_FF_EXPERT, d), DEEPNORM_BETA * D_FF_EXPERT ** -0.5),
    }


def reference(x, c, ln_in_g, ln_in_b, ada_w, ada_b, w_in, gdn_conv, gdn_a_log, gdn_dt_bias, gdn_norm_g,
              mlstm_b_i, mlstm_b_f, mlstm_norm_g, rwkv_mu, rwkv_w0, rwkv_w2, rwkv_a0, rwkv_a2, rwkv_g2,
              rwkv_k_k, rwkv_k_a, rwkv_r_k, rwkv_lnx_g, rwkv_lnx_b, rwkv_v1, rwkv_mu_v1, rwkv_v0, rwkv_v2,
              w_branch, w_o, ln1_g, ln1_b, ln2_g, ln2_b, ffn_w_gate, ffn_w_up, ffn_w_down,
              moe_router, moe_router_b, moe_w_gate, moe_w_up, moe_w_down):
    x = _layer_norm(x, ln_in_g, ln_in_b)
    cond = jax.nn.silu(c)
    v_first = None
    for layer in range(DEPTH):
        mod = cond @ ada_w[layer] + ada_b[layer]
        sh1, sc1, gt1, sh2, sc2, gt2 = jnp.split(mod[:, None, :], 6, axis=-1)
        v_res = None if layer == 0 else (rwkv_v1[layer - 1], rwkv_mu_v1[layer - 1], rwkv_v0[layer - 1], rwkv_v2[layer - 1])
        y, v_first = _mixer_sublayer(x * (1.0 + sc1) + sh1, w_in[layer], gdn_conv[layer], gdn_a_log[layer],
                                     gdn_dt_bias[layer], gdn_norm_g[layer], mlstm_b_i[layer], mlstm_b_f[layer],
                                     mlstm_norm_g[layer], rwkv_mu[layer], rwkv_w0[layer], rwkv_w2[layer],
                                     rwkv_a0[layer], rwkv_a2[layer], rwkv_g2[layer], rwkv_k_k[layer],
                                     rwkv_k_a[layer], rwkv_r_k[layer], rwkv_lnx_g[layer], rwkv_lnx_b[layer],
                                     w_branch[layer], w_o[layer], v_first, v_res)
        x = _layer_norm(DEEPNORM_ALPHA * x + (1.0 + gt1) * y, ln1_g[layer], ln1_b[layer])
        h = x * (1.0 + sc2) + sh2
        if layer % 2 == 0:
            i = layer // 2
            y = _swiglu(h, ffn_w_gate[i], ffn_w_up[i], ffn_w_down[i])
        else:
            i = layer // 2
            y = _moe_swiglu(h, moe_router[i], moe_router_b[i], moe_w_gate[i], moe_w_up[i], moe_w_down[i])
        x = _layer_norm(DEEPNORM_ALPHA * x + (1.0 + gt2) * y, ln2_g[layer], ln2_b[layer])
    return x
```

```python
import functools
import math

import jax
import jax.numpy as jnp
from jax import lax
from jax.experimental import pallas as pl
from jax.experimental.pallas import tpu as pltpu

f32 = jnp.float32
bf16 = jnp.bfloat16
i32 = jnp.int32

D_MODEL = 1024
DEPTH = 2
CHUNK = 64
GDN_HEADS = 4
GDN_DK = 128
MLSTM_HEADS = 4
MLSTM_DQK = 64
MLSTM_DV = 128
MLSTM_SOFTCAP = 15.0
RWKV_HEADS = 8
RWKV_DH = 64
RWKV_WIDTH = RWKV_HEADS * RWKV_DH
LORA_W = 64
LORA_A = 64
LORA_V = 32
LORA_G = 128
N_EXPERTS = 8
TOP_K = 2
DEEPNORM_ALPHA = (2 * DEPTH) ** 0.25
LN_EPS = 1e-5
RWKV_GN_EPS = 64e-5

LANES = 128
HALF = LANES // 2

Z_GDN = 0
Z_MLSTM = 2048
Z_RWKV = 3584
Z_GATE = 5120
Z_MISC = 8192
Z_WIDTH = 8704
MISC_SMALLS = 384
SM_GDN_A, SM_GDN_B, SM_ML_I, SM_ML_F = 0, 4, 8, 12

MOE_ROWS = 1024
VMEM_LIMIT = 56 * 1024 * 1024


def _cparams(*sem):
    return pltpu.CompilerParams(dimension_semantics=sem, vmem_limit_bytes=VMEM_LIMIT)


def _dot(a, b):
    return jnp.dot(a.astype(bf16), b.astype(bf16), preferred_element_type=f32)


def _dot_nt(a, b):
    return lax.dot_general(a.astype(bf16), b.astype(bf16), (((1,), (1,)), ((), ())), preferred_element_type=f32)


def _dot_tn(a, b):
    return lax.dot_general(a.astype(bf16), b.astype(bf16), (((0,), (0,)), ((), ())), preferred_element_type=f32)


def _sigmoid(x):
    return 1.0 / (1.0 + jnp.exp(-x))


def _silu(x):
    return x * _sigmoid(x)


def _softplus(x):
    return jnp.maximum(x, 0.0) + jnp.log1p(jnp.exp(-jnp.abs(x)))


def _iota2(shape, axis):
    return lax.broadcasted_iota(i32, shape, axis)


def _cumsum_rows(x):
    n = x.shape[0]
    rows = _iota2(x.shape, 0)
    sh = 1
    while sh < n:
        x = x + jnp.where(rows >= sh, pltpu.roll(x, sh, 0), 0.0)
        sh *= 2
    return x


def _unit_lower_inverse(a):
    c = a.shape[0]
    eye = (_iota2((c, c), 0) == _iota2((c, c), 1)).astype(f32)
    p = -a
    t = eye + p
    for _ in range(int(math.log2(c)) - 1):
        p = _dot(p, p)
        t = t + _dot(t, p)
    return t


def _ln_rows(r, g, b):
    mu = jnp.mean(r, -1, keepdims=True)
    d = r - mu
    var = jnp.mean(d * d, -1, keepdims=True)
    return d * lax.rsqrt(var + LN_EPS) * g + b


def _half_masks(shape):
    lane = _iota2(shape, len(shape) - 1)
    return lane < HALF, lane >= HALF


def _head64_sums(x):
    lo, _ = _half_masks((x.shape[0], LANES))
    outs = []
    for gi in range(x.shape[1] // LANES):
        xs = x[:, gi * LANES : (gi + 1) * LANES]
        s_lo = jnp.sum(jnp.where(lo, xs, 0.0), -1, keepdims=True)
        s_hi = jnp.sum(jnp.where(lo, 0.0, xs), -1, keepdims=True)
        outs.append(jnp.where(lo, s_lo, s_hi))
    return jnp.concatenate(outs, axis=1)


def _ada_body(c_ref, w_ref, b_ref, o_ref):
    o_ref[...] = _dot(_silu(c_ref[...]), w_ref[...]) + b_ref[...]


def _ada_mod(c_pad, w, b):
    n = w.shape[1]
    tn = n // 4
    return pl.pallas_call(
        _ada_body,
        grid=(n // tn,),
        in_specs=[
            pl.BlockSpec(c_pad.shape, lambda j: (0, 0)),
            pl.BlockSpec((w.shape[0], tn), lambda j: (0, j)),
            pl.BlockSpec((1, tn), lambda j: (0, j)),
        ],
        out_specs=pl.BlockSpec((c_pad.shape[0], tn), lambda j: (0, j)),
        out_shape=jax.ShapeDtypeStruct((c_pad.shape[0], n), f32),
        compiler_params=_cparams("arbitrary"),
        name="ada_mod",
    )(c_pad, w, b)


def _entry_body(x_ref, g_ref, b_ref, sc_ref, sh_ref, xn_ref, h_ref):
    xn = _ln_rows(x_ref[...], g_ref[...], b_ref[...])
    xn_ref[...] = xn
    h_ref[...] = (xn * (1.0 + sc_ref[...]) + sh_ref[...]).astype(h_ref.dtype)


def _mod_spec(rows_per_batch, tm, slot):
    return pl.BlockSpec((None, 1, D_MODEL), lambda i: ((i * tm // rows_per_batch) * 6 + slot, 0, 0))


def _row_spec(tm, width=D_MODEL):
    return pl.BlockSpec((tm, width), lambda i: (i, 0))


def _vec_spec(width=D_MODEL):
    return pl.BlockSpec((1, width), lambda i: (0, 0))


def _entry_ln(x, g, b, mod, t):
    n = x.shape[0]
    tm = 512
    return pl.pallas_call(
        _entry_body,
        grid=(n // tm,),
        in_specs=[_row_spec(tm), _vec_spec(), _vec_spec(), _mod_spec(t, tm, 1), _mod_spec(t, tm, 0)],
        out_specs=[_row_spec(tm), _row_spec(tm)],
        out_shape=[jax.ShapeDtypeStruct((n, D_MODEL), f32), jax.ShapeDtypeStruct((n, D_MODEL), bf16)],
        compiler_params=_cparams("arbitrary"),
        name="entry_ln",
    )(x, g, b, mod, mod)


def _mm_body(x_ref, w_ref, o_ref):
    o_ref[...] = jnp.dot(x_ref[...], w_ref[...], preferred_element_type=f32).astype(o_ref.dtype)


def _matmul(x, w, tm, tn, out_dtype, name):
    m, k = x.shape
    n = w.shape[1]
    return pl.pallas_call(
        _mm_body,
        grid=(m // tm, n // tn),
        in_specs=[pl.BlockSpec((tm, k), lambda i, j: (i, 0)), pl.BlockSpec((k, tn), lambda i, j: (0, j))],
        out_specs=pl.BlockSpec((tm, tn), lambda i, j: (i, j)),
        out_shape=jax.ShapeDtypeStruct((m, n), out_dtype),
        compiler_params=_cparams("arbitrary", "arbitrary"),
        name=name,
    )(x, w)


def _gdn_body(qkv_ref, zg_ref, sm_ref, conv_ref, par_ref, ng_ref, o_ref, s_ref, buf_ref):
    c = qkv_ref.shape[0]
    w3 = qkv_ref.shape[1]

    @pl.when(pl.program_id(1) == 0)
    def _():
        s_ref[...] = jnp.zeros(s_ref.shape, f32)
        buf_ref[0:8, :] = jnp.zeros((8, w3), f32)

    raw = qkv_ref[...]
    buf_ref[8 : 8 + c, :] = raw
    conv = conv_ref[3:4, :] * raw
    for j in range(3):
        conv = conv + conv_ref[j : j + 1, :] * buf_ref[5 + j : 5 + j + c, :]
    buf_ref[0:8, :] = buf_ref[c : c + 8, :]
    x = _silu(conv)

    sm = sm_ref[...]
    a_log = par_ref[0:1, :]
    dt_bias = par_ref[1:2, :]
    gcum = _cumsum_rows(-jnp.exp(a_log) * _softplus(sm + dt_bias))
    gcum_t = gcum.T
    beta_all = _sigmoid(sm)

    incl = _iota2((c, c), 0) >= _iota2((c, c), 1)
    strict = _iota2((c, c), 0) > _iota2((c, c), 1)
    ng = ng_ref[...]
    zg = zg_ref[...]
    for h in range(GDN_HEADS):
        q = x[:, h * 128 : (h + 1) * 128]
        k = x[:, 512 + h * 128 : 512 + (h + 1) * 128]
        v = x[:, 1024 + h * 128 : 1024 + (h + 1) * 128]
        q = q * lax.rsqrt(jnp.sum(q * q, -1, keepdims=True) + 1e-6) * GDN_DK**-0.5
        k = k * lax.rsqrt(jnp.sum(k * k, -1, keepdims=True) + 1e-6)
        g = gcum[:, SM_GDN_A + h : SM_GDN_A + h + 1]
        g_row = gcum_t[SM_GDN_A + h : SM_GDN_A + h + 1, :]
        g_last = gcum[c - 1 : c, SM_GDN_A + h : SM_GDN_A + h + 1]
        beta = beta_all[:, SM_GDN_B + h : SM_GDN_B + h + 1]
        decay = jnp.where(incl, jnp.exp(g - g_row), 0.0)
        eg = jnp.exp(g)
        kb = k * beta
        a_kk = jnp.where(strict, _dot_nt(kb, k) * decay, 0.0)
        t_inv = _unit_lower_inverse(a_kk)
        uw = _dot(t_inv, jnp.concatenate([v * beta, kb * eg], axis=1))
        u, w = uw[:, :128], uw[:, 128:]
        qk = _dot_nt(q, k) * decay
        s = s_ref[h]
        ws_qs = _dot(jnp.concatenate([w, q * eg], axis=0), s)
        v_new = u - ws_qs[:c]
        o = ws_qs[c:] + _dot(qk, v_new)
        s_ref[h] = s * jnp.exp(g_last) + _dot_tn(k * jnp.exp(g_last - g), v_new)
        o = o * lax.rsqrt(jnp.mean(o * o, -1, keepdims=True) + 1e-6) * ng
        o_ref[:, h * 128 : (h + 1) * 128] = o * _silu(zg[:, h * 128 : (h + 1) * 128])


def _gdn(z, conv_w, par, norm_g, b, t):
    c = CHUNK
    nc = t // c
    row = lambda bi, n: bi * nc + n
    return pl.pallas_call(
        _gdn_body,
        grid=(b, nc),
        in_specs=[
            pl.BlockSpec((c, 1536), lambda bi, n: (row(bi, n), Z_GDN // 1536)),
            pl.BlockSpec((c, 512), lambda bi, n: (row(bi, n), (Z_GDN + 1536) // 512)),
            pl.BlockSpec((c, 128), lambda bi, n: (row(bi, n), (Z_MISC + MISC_SMALLS) // 128)),
            pl.BlockSpec((4, 1536), lambda bi, n: (0, 0)),
            pl.BlockSpec((8, 128), lambda bi, n: (0, 0)),
            pl.BlockSpec((1, 128), lambda bi, n: (0, 0)),
        ],
        out_specs=pl.BlockSpec((c, 512), lambda bi, n: (row(bi, n), 0)),
        out_shape=jax.ShapeDtypeStruct((b * t, 512), f32),
        scratch_shapes=[pltpu.VMEM((GDN_HEADS, 128, 128), f32), pltpu.VMEM((c + 8, 1536), f32)],
        compiler_params=_cparams("arbitrary", "arbitrary"),
        name="gdn",
    )(z, z, z, conv_w, par, norm_g)


def _mlstm_body(qk_ref, v_ref, og_ref, sm_ref, par_ref, ng_ref, o_ref, cs_ref, n_ref, m_ref):
    c = qk_ref.shape[0]

    @pl.when(pl.program_id(1) == 0)
    def _():
        cs_ref[...] = jnp.zeros(cs_ref.shape, f32)
        n_ref[...] = jnp.zeros(n_ref.shape, f32)
        m_ref[...] = jnp.zeros(m_ref.shape, f32)

    def cap(x):
        return MLSTM_SOFTCAP * jnp.tanh(x / MLSTM_SOFTCAP)

    sm = sm_ref[...] + par_ref[0:1, :]
    capped = cap(sm)
    log_i = capped
    bcum = _cumsum_rows(-_softplus(-capped))
    log_i_t = log_i.T
    bcum_t = bcum.T

    incl = _iota2((c, c), 0) >= _iota2((c, c), 1)
    lo, hi = _half_masks((c, LANES))
    rows_lo = _iota2((LANES, 1), 0) < HALF
    lane_lo = _iota2((1, LANES), 1) < HALF
    qk = qk_ref[...]
    v_all = v_ref[...]
    og = og_ref[...]
    for p in range(MLSTM_HEADS // 2):
        q01 = qk[:, p * 128 : (p + 1) * 128]
        k01 = qk[:, 256 + p * 128 : 256 + (p + 1) * 128] * MLSTM_DQK**-0.5
        cs = cs_ref[p]
        nvec = n_ref[p]
        upd = []
        for e in range(2):
            h = 2 * p + e
            mask = lo if e == 0 else hi
            q = jnp.where(mask, q01, 0.0)
            v = v_all[:, h * 128 : (h + 1) * 128]
            bc = bcum[:, SM_ML_F + h : SM_ML_F + h + 1]
            b_row = bcum_t[SM_ML_F + h : SM_ML_F + h + 1, :]
            li = log_i[:, SM_ML_I + h : SM_ML_I + h + 1]
            li_row = log_i_t[SM_ML_I + h : SM_ML_I + h + 1, :]
            b_last = bcum[c - 1 : c, SM_ML_F + h : SM_ML_F + h + 1]
            m = m_ref[h : h + 1, 0:1]
            log_d = jnp.where(incl, bc - b_row + li_row, -jnp.inf)
            m_intra = jnp.max(log_d, -1, keepdims=True)
            pm = jnp.exp(log_d - m_intra) * _dot_nt(q, k01)
            num_intra = _dot(pm, v)
            den_intra = jnp.sum(pm, -1, keepdims=True)
            log_e = b_last - bc + li
            m_end = jnp.max(log_e, 0, keepdims=True)
            kw = jnp.where(mask, k01, 0.0) * jnp.exp(log_e - m_end)
            m_t = jnp.maximum(bc + m, m_intra)
            s_inter = jnp.exp(bc + m - m_t)
            s_intra = jnp.exp(m_intra - m_t)
            num = s_inter * _dot(q, cs) + s_intra * num_intra
            den = s_inter * jnp.sum(q * nvec, -1, keepdims=True) + s_intra * den_intra
            hc = num / jnp.maximum(jnp.abs(den), jnp.exp(-m_t))
            m_new = jnp.maximum(b_last + m, m_end)
            s_old = jnp.exp(b_last + m - m_new)
            s_new = jnp.exp(m_end - m_new)
            upd.append((s_old, s_new * _dot_tn(kw, v), s_new * jnp.sum(kw, 0, keepdims=True)))
            m_ref[h : h + 1, :] = jnp.broadcast_to(m_new, (1, LANES))
            hs = hc * lax.rsqrt(jnp.mean(hc * hc, -1, keepdims=True) + 1e-6) * ng_ref[:, h * 128 : (h + 1) * 128]
            o_ref[:, h * 128 : (h + 1) * 128] = hs * _sigmoid(og[:, h * 128 : (h + 1) * 128])
        cs_ref[p] = jnp.where(rows_lo, upd[0][0], upd[1][0]) * cs + upd[0][1] + upd[1][1]
        n_ref[p] = jnp.where(lane_lo, upd[0][0], upd[1][0]) * nvec + upd[0][2] + upd[1][2]


def _mlstm(z, par, norm_g, b, t):
    c = CHUNK
    nc = t // c
    row = lambda bi, n: bi * nc + n
    return pl.pallas_call(
        _mlstm_body,
        grid=(b, nc),
        in_specs=[
            pl.BlockSpec((c, 512), lambda bi, n: (row(bi, n), Z_MLSTM // 512)),
            pl.BlockSpec((c, 512), lambda bi, n: (row(bi, n), (Z_MLSTM + 512) // 512)),
            pl.BlockSpec((c, 512), lambda bi, n: (row(bi, n), (Z_MLSTM + 1024) // 512)),
            pl.BlockSpec((c, 128), lambda bi, n: (row(bi, n), (Z_MISC + MISC_SMALLS) // 128)),
            pl.BlockSpec((8, 128), lambda bi, n: (0, 0)),
            pl.BlockSpec((1, 512), lambda bi, n: (0, 0)),
        ],
        out_specs=pl.BlockSpec((c, 512), lambda bi, n: (row(bi, n), 0)),
        out_shape=jax.ShapeDtypeStruct((b * t, 512), f32),
        scratch_shapes=[
            pltpu.VMEM((MLSTM_HEADS // 2, 128, 128), f32),
            pltpu.VMEM((MLSTM_HEADS // 2, 1, 128), f32),
            pltpu.VMEM((8, 128), f32),
        ],
        compiler_params=_cparams("arbitrary", "arbitrary"),
        name="mlstm",
    )(z, z, z, z, par, norm_g)


def _rwkv_body(*refs, has_vres):
    if has_vres:
        (r_ref, k_ref, v_ref, misc_ref, vf_ref, mu_ref, mum_ref, vecs_ref, w2_ref, a2_ref, g2_ref, v2_ref,
         y_ref, st_ref, tail_ref, tailm_ref) = refs
    else:
        (r_ref, k_ref, v_ref, misc_ref, mu_ref, mum_ref, vecs_ref, w2_ref, a2_ref, g2_ref,
         y_ref, vf_out_ref, st_ref, tail_ref, tailm_ref) = refs
    c = r_ref.shape[0]

    @pl.when(pl.program_id(1) == 0)
    def _():
        st_ref[...] = jnp.zeros(st_ref.shape, f32)
        tail_ref[0:8, :] = jnp.zeros((8, tail_ref.shape[1]), f32)
        tailm_ref[0:8, :] = jnp.zeros((8, tailm_ref.shape[1]), f32)

    raw = jnp.concatenate([r_ref[...], k_ref[...], v_ref[...]], axis=1)
    tail_ref[8 : 8 + c, :] = raw
    prev = tail_ref[7 : 7 + c, :]
    tail_ref[0:8, :] = tail_ref[c : c + 8, :]
    rkv = raw + mu_ref[...] * (prev - raw)
    rawm = misc_ref[...]
    tailm_ref[8 : 8 + c, :] = rawm
    prevm = tailm_ref[7 : 7 + c, :]
    tailm_ref[0:8, :] = tailm_ref[c : c + 8, :]
    misc = rawm + mum_ref[...] * (prevm - rawm)

    r = rkv[:, 0:512]
    k = rkv[:, 512:1024]
    v = rkv[:, 1024:1536]
    w0, a0, k_k, k_a, r_k, lnx_g, lnx_b, v0 = (vecs_ref[i : i + 1, :] for i in range(8))
    wa = misc[:, 0:128]
    log_w = -jnp.exp(-_softplus(-(w0 + _dot(jnp.tanh(wa), w2_ref[...]))) - 0.5)
    a = _sigmoid(a0 + _dot(wa, a2_ref[...]))
    if has_vres:
        v = v + (vf_ref[...] - v) * _sigmoid(v0 + _dot(misc[:, 256:384], v2_ref[...]))
    else:
        vf_out_ref[...] = v
    g = _dot(_sigmoid(misc[:, 128:256]), g2_ref[...])
    kk = k * k_k
    kk = kk * lax.rsqrt(_head64_sums(kk * kk) + 1e-6)
    k = k * (1.0 + (a - 1.0) * k_a)
    alpha = -kk
    beta = kk * a
    lw = _cumsum_rows(log_w)
    lw_last = lw[c - 1 : c, :]
    inv = jnp.exp(-lw)
    al_bar = alpha * jnp.exp(lw - log_w)
    be_hat = beta * inv
    k_hat = k * inv
    r_bar = r * jnp.exp(lw)
    to_end = jnp.exp(lw_last - lw)
    be_end = beta * to_end
    k_end = k * to_end
    g_chunk = jnp.exp(lw_last)

    incl = _iota2((c, c), 0) >= _iota2((c, c), 1)
    strict = _iota2((c, c), 0) > _iota2((c, c), 1)
    lo, hi = _half_masks((c, LANES))
    same_head = (_iota2((LANES, LANES), 0) < HALF) == (_iota2((LANES, LANES), 1) < HALF)

    def sel(x0, x1):
        return jnp.where(lo, x0, x1)

    ys = []
    for p in range(RWKV_HEADS // 2):
        sl = slice(p * 128, (p + 1) * 128)
        al_p, be_p, kh_p, rb_p, v_p = al_bar[:, sl], be_hat[:, sl], k_hat[:, sl], r_bar[:, sl], v[:, sl]
        t_inv, l_ak, q_rb, q_rk = [], [], [], []
        for e in range(2):
            mask = lo if e == 0 else hi
            al_e = jnp.where(mask, al_p, 0.0)
            rb_e = jnp.where(mask, rb_p, 0.0)
            l_ab = jnp.where(strict, _dot_nt(al_e, be_p), 0.0)
            l_ak.append(jnp.where(strict, _dot_nt(al_e, kh_p), 0.0))
            q_rb.append(jnp.where(incl, _dot_nt(rb_e, be_p), 0.0))
            q_rk.append(jnp.where(incl, _dot_nt(rb_e, kh_p), 0.0))
            t_inv.append(_unit_lower_inverse(-l_ab))
        w_u = sel(_dot(t_inv[0], al_p), _dot(t_inv[1], al_p))
        lakv = sel(_dot(l_ak[0], v_p), _dot(l_ak[1], v_p))
        u0 = sel(_dot(t_inv[0], lakv), _dot(t_inv[1], lakv))
        y_k = sel(_dot(q_rk[0], v_p), _dot(q_rk[1], v_p))
        st = st_ref[p]
        u = _dot_nt(w_u, st) + u0
        y = _dot_nt(rb_p, st) + sel(_dot(q_rb[0], u), _dot(q_rb[1], u)) + y_k
        new = _dot_tn(u, be_end[:, sl]) + _dot_tn(v_p, k_end[:, sl])
        st_ref[p] = g_chunk[:, sl] * st + jnp.where(same_head, new, 0.0)
        ys.append(y)
    y = jnp.concatenate(ys, axis=1)
    mu = _head64_sums(y) * (1.0 / RWKV_DH)
    d = y - mu
    var = _head64_sums(d * d) * (1.0 / RWKV_DH)
    y = d * lax.rsqrt(var + RWKV_GN_EPS) * lnx_g + lnx_b
    bonus = _head64_sums(r * k * r_k) * v
    y_ref[...] = (y + bonus) * g


def _rwkv(z, v_first, mu, mu_misc, vecs, w2p, a2p, g2, v2p, b, t):
    c = CHUNK
    nc = t // c
    row = lambda bi, n: bi * nc + n
    has_vres = v_first is not None
    zspec = lambda off: pl.BlockSpec((c, 512), lambda bi, n: (row(bi, n), off // 512))
    full = lambda shape: pl.BlockSpec(shape, lambda bi, n: (0,) * len(shape))
    blk = pl.BlockSpec((c, 512), lambda bi, n: (row(bi, n), 0))
    in_specs = [zspec(Z_RWKV), zspec(Z_RWKV + 512), zspec(Z_RWKV + 1024), zspec(Z_MISC)]
    args = [z, z, z, z]
    if has_vres:
        in_specs.append(blk)
        args.append(v_first)
    in_specs += [full((1, 1536)), full((1, 512)), full((8, 512)), full((128, 512)), full((128, 512)), full((128, 512))]
    args += [mu, mu_misc, vecs, w2p, a2p, g2]
    if has_vres:
        in_specs.append(full((128, 512)))
        args.append(v2p)
    out_sd = jax.ShapeDtypeStruct((b * t, 512), f32)
    res = pl.pallas_call(
        functools.partial(_rwkv_body, has_vres=has_vres),
        grid=(b, nc),
        in_specs=in_specs,
        out_specs=blk if has_vres else [blk, blk],
        out_shape=out_sd if has_vres else [out_sd, out_sd],
        scratch_shapes=[
            pltpu.VMEM((RWKV_HEADS // 2, 128, 128), f32),
            pltpu.VMEM((c + 8, 1536), f32),
            pltpu.VMEM((c + 8, 512), f32),
        ],
        compiler_params=_cparams("arbitrary", "arbitrary"),
        name="rwkv_res" if has_vres else "rwkv",
    )(*args)
    return (res, v_first) if has_vres else (res[0], res[1])


def _merge_body(ya_ref, yb_ref, yc_ref, ga_ref, gb_ref, gc_ref, x_ref, wb_ref, wo_ref, gt_ref, g_ref, b_ref,
                sc_ref, sh_ref, xn_ref, h_ref):
    merged = _sigmoid(ga_ref[...]) * _dot(ya_ref[...], wb_ref[0])
    merged = merged + _sigmoid(gb_ref[...]) * _dot(yb_ref[...], wb_ref[1])
    merged = merged + _sigmoid(gc_ref[...]) * _dot(yc_ref[...], wb_ref[2])
    y = _dot(merged, wo_ref[...])
    xn = _ln_rows(DEEPNORM_ALPHA * x_ref[...] + (1.0 + gt_ref[...]) * y, g_ref[...], b_ref[...])
    xn_ref[...] = xn
    h_ref[...] = (xn * (1.0 + sc_ref[...]) + sh_ref[...]).astype(h_ref.dtype)


def _merge(ya, yb, yc, z, x, wb, wo, mod, ln_g, ln_b, t, h_dtype):
    n = x.shape[0]
    tm = 512
    gate = lambda i: pl.BlockSpec((tm, 1024), lambda r: (r, Z_GATE // 1024 + i))
    return pl.pallas_call(
        _merge_body,
        grid=(n // tm,),
        in_specs=[
            _row_spec(tm, 512), _row_spec(tm, 512), _row_spec(tm, 512), gate(0), gate(1), gate(2), _row_spec(tm),
            pl.BlockSpec((3, 512, D_MODEL), lambda r: (0, 0, 0)),
            pl.BlockSpec((D_MODEL, D_MODEL), lambda r: (0, 0)),
            _mod_spec(t, tm, 2), _vec_spec(), _vec_spec(), _mod_spec(t, tm, 4), _mod_spec(t, tm, 3),
        ],
        out_specs=[_row_spec(tm), _row_spec(tm)],
        out_shape=[jax.ShapeDtypeStruct((n, D_MODEL), f32), jax.ShapeDtypeStruct((n, D_MODEL), h_dtype)],
        compiler_params=_cparams("arbitrary"),
        name="merge",
    )(ya, yb, yc, z, z, z, x, wb, wo, mod, ln_g, ln_b, mod, mod)


def _ffn_body(h_ref, x_ref, wg_ref, wu_ref, wd_ref, gt_ref, g_ref, b_ref, sc_ref, sh_ref, xn_ref, hn_ref, acc_ref):
    f = pl.program_id(1)

    @pl.when(f == 0)
    def _():
        acc_ref[...] = jnp.zeros(acc_ref.shape, f32)

    h = h_ref[...]
    act = _silu(_dot(h, wg_ref[...])) * _dot(h, wu_ref[...])
    acc_ref[...] += _dot(act, wd_ref[...])

    @pl.when(f == pl.num_programs(1) - 1)
    def _():
        xn = _ln_rows(DEEPNORM_ALPHA * x_ref[...] + (1.0 + gt_ref[...]) * acc_ref[...], g_ref[...], b_ref[...])
        xn_ref[...] = xn
        hn_ref[...] = (xn * (1.0 + sc_ref[...]) + sh_ref[...]).astype(hn_ref.dtype)


def _ffn(h, x, wg, wu, wd, mod, mod_next, ln_g, ln_b, t):
    n = x.shape[0]
    dff = wg.shape[1]
    tm, tf = 1024, 256
    row = pl.BlockSpec((tm, D_MODEL), lambda i, f: (i, 0))
    vec = pl.BlockSpec((1, D_MODEL), lambda i, f: (0, 0))
    modspec = lambda slot: pl.BlockSpec((None, 1, D_MODEL), lambda i, f: ((i * tm // t) * 6 + slot, 0, 0))
    return pl.pallas_call(
        _ffn_body,
        grid=(n // tm, dff // tf),
        in_specs=[
            row, row,
            pl.BlockSpec((D_MODEL, tf), lambda i, f: (0, f)),
            pl.BlockSpec((D_MODEL, tf), lambda i, f: (0, f)),
            pl.BlockSpec((tf, D_MODEL), lambda i, f: (f, 0)),
            modspec(5), vec, vec, modspec(1), modspec(0),
        ],
        out_specs=[row, row],
        out_shape=[jax.ShapeDtypeStruct((n, D_MODEL), f32), jax.ShapeDtypeStruct((n, D_MODEL), bf16)],
        scratch_shapes=[pltpu.VMEM((tm, D_MODEL), f32)],
        compiler_params=_cparams("arbitrary", "arbitrary"),
        name="ffn",
    )(h, x, wg, wu, wd, mod, ln_g, ln_b, mod_next, mod_next)


def _router_body(h_ref, w_ref, b_ref, idx_ref, gate_ref):
    logits = jnp.dot(h_ref[...], w_ref[...], precision=lax.Precision.HIGHEST, preferred_element_type=f32) + b_ref[...]
    lane = _iota2(logits.shape, 1)
    logits = jnp.where(lane < N_EXPERTS, logits, -jnp.inf)
    m1 = jnp.max(logits, -1, keepdims=True)
    i1 = jnp.min(jnp.where(logits == m1, lane, LANES), -1, keepdims=True)
    rest = jnp.where(lane == i1, -jnp.inf, logits)
    m2 = jnp.max(rest, -1, keepdims=True)
    i2 = jnp.min(jnp.where(rest == m2, lane, LANES), -1, keepdims=True)
    e = jnp.exp(m2 - m1)
    g1 = 1.0 / (1.0 + e)
    g2 = e / (1.0 + e)
    idx_ref[...] = jnp.where(lane == 0, i1, jnp.where(lane == 1, i2, 0))
    gate_ref[...] = jnp.where(lane == 0, g1, jnp.where(lane == 1, g2, 0.0))


def _router(h, w_pad, b_pad):
    n = h.shape[0]
    tm = 512
    return pl.pallas_call(
        _router_body,
        grid=(n // tm,),
        in_specs=[_row_spec(tm), pl.BlockSpec((D_MODEL, LANES), lambda i: (0, 0)), _vec_spec(LANES)],
        out_specs=[_row_spec(tm, LANES), _row_spec(tm, LANES)],
        out_shape=[jax.ShapeDtypeStruct((n, LANES), i32), jax.ShapeDtypeStruct((n, LANES), f32)],
        compiler_params=_cparams("arbitrary"),
        name="router",
    )(h, w_pad, b_pad)


def _moe_body(be_ref, live_ref, tok_ref, dst_ref, gate_ref, h_hbm, wg_ref, wu_ref, wd_ref, y_hbm,
              xbuf, xb16, acc_ref, obuf, gsem, ssem, *, n_dst):
    j = pl.program_id(0)
    f = pl.program_id(1)
    nf = pl.num_programs(1)
    rows = xbuf.shape[0]
    live = live_ref[j] > 0

    def row_in(r):
        return pltpu.make_async_copy(h_hbm.at[pl.ds(tok_ref[0, 0, r], 1), :], xbuf.at[pl.ds(r, 1), :], gsem)

    def row_out(r):
        return pltpu.make_async_copy(obuf.at[pl.ds(r, 1), :], y_hbm.at[pl.ds(dst_ref[0, 0, r], 1), :], ssem)

    @pl.when(jnp.logical_and(live, f == 0))
    def _():
        def start(r, carry):
            row_in(r).start()
            return carry

        def wait(r, carry):
            row_in(r).wait()
            return carry

        lax.fori_loop(0, rows, start, 0)
        lax.fori_loop(0, rows, wait, 0)
        xb16[...] = xbuf[...].astype(bf16)
        acc_ref[...] = jnp.zeros(acc_ref.shape, f32)

    @pl.when(live)
    def _():
        x = xb16[...]
        act = _silu(_dot(x, wg_ref[...])) * _dot(x, wu_ref[...])
        acc_ref[...] += _dot(act, wd_ref[...])

    @pl.when(jnp.logical_and(live, f == nf - 1))
    def _():
        obuf[...] = acc_ref[...] * gate_ref[...]

        def start(r, carry):
            @pl.when(dst_ref[0, 0, r] < n_dst)
            def _():
                row_out(r).start()

            return carry

        def wait(r, carry):
            @pl.when(dst_ref[0, 0, r] < n_dst)
            def _():
                row_out(r).wait()

            return carry

        lax.fori_loop(0, rows, start, 0)
        lax.fori_loop(0, rows, wait, 0)


def _moe(h, block_e, block_live, row_tok, row_dst, row_gate, wg, wu, wd):
    n = h.shape[0]
    n_blocks = block_e.shape[0]
    rb = MOE_ROWS
    dff = wg.shape[2]
    tf = 512
    smem_blk = pl.BlockSpec((1, 1, rb), lambda j, f, be, lv: (j, 0, 0), memory_space=pltpu.SMEM)
    grid_spec = pltpu.PrefetchScalarGridSpec(
        num_scalar_prefetch=2,
        grid=(n_blocks, dff // tf),
        in_specs=[
            smem_blk,
            smem_blk,
            pl.BlockSpec((rb, 1), lambda j, f, be, lv: (j, 0)),
            pl.BlockSpec(memory_space=pl.ANY),
            pl.BlockSpec((None, D_MODEL, tf), lambda j, f, be, lv: (be[j], 0, f)),
            pl.BlockSpec((None, D_MODEL, tf), lambda j, f, be, lv: (be[j], 0, f)),
            pl.BlockSpec((None, tf, D_MODEL), lambda j, f, be, lv: (be[j], f, 0)),
        ],
        out_specs=pl.BlockSpec(memory_space=pl.ANY),
        scratch_shapes=[
            pltpu.VMEM((rb, D_MODEL), f32),
            pltpu.VMEM((rb, D_MODEL), bf16),
            pltpu.VMEM((rb, D_MODEL), f32),
            pltpu.VMEM((rb, D_MODEL), f32),
            pltpu.SemaphoreType.DMA,
            pltpu.SemaphoreType.DMA,
        ],
    )
    return pl.pallas_call(
        functools.partial(_moe_body, n_dst=TOP_K * n),
        grid_spec=grid_spec,
        out_shape=jax.ShapeDtypeStruct((TOP_K * n, D_MODEL), f32),
        compiler_params=_cparams("arbitrary", "arbitrary"),
        name="moe",
    )(block_e, block_live, row_tok, row_dst, row_gate, h, wg, wu, wd)


def _routing_tables(top_e, top_g, n):
    rb = MOE_ROWS
    n_assign = n * TOP_K
    n_blocks = -(-n_assign // rb) + N_EXPERTS
    rows = n_blocks * rb
    e_flat = top_e.reshape(-1)
    onehot = (e_flat[:, None] == jnp.arange(N_EXPERTS, dtype=i32)[None, :]).astype(i32)
    rank = jnp.sum((jnp.cumsum(onehot, axis=0) - onehot) * onehot, axis=1)
    counts = jnp.sum(onehot, axis=0)
    padded = (counts + rb - 1) // rb * rb
    pad_end = jnp.cumsum(padded)
    pad_start = pad_end - padded
    dest = pad_start[e_flat] + rank
    a = jnp.arange(n_assign, dtype=i32)
    row_tok = jnp.zeros((rows,), i32).at[dest].set(a // TOP_K)
    row_dst = jnp.full((rows,), n_assign, i32).at[dest].set((a % TOP_K) * n + a // TOP_K)
    row_gate = jnp.zeros((rows,), f32).at[dest].set(top_g.reshape(-1))
    starts = jnp.arange(n_blocks, dtype=i32) * rb
    block_e = jnp.minimum(jnp.searchsorted(pad_end, starts, side="right"), N_EXPERTS - 1).astype(i32)
    block_live = (starts < pad_end[-1]).astype(i32)
    return (block_e, block_live, row_tok.reshape(n_blocks, 1, rb), row_dst.reshape(n_blocks, 1, rb),
            row_gate.reshape(rows, 1))


def _final_body(x_ref, y0_ref, y1_ref, gt_ref, g_ref, b_ref, o_ref):
    y = y0_ref[...] + y1_ref[...]
    o_ref[...] = _ln_rows(DEEPNORM_ALPHA * x_ref[...] + (1.0 + gt_ref[...]) * y, g_ref[...], b_ref[...])


def _final_ln(x, y2, mod, ln_g, ln_b, t):
    n = x.shape[0]
    tm = 512
    nb = n // tm
    return pl.pallas_call(
        _final_body,
        grid=(nb,),
        in_specs=[_row_spec(tm), _row_spec(tm), pl.BlockSpec((tm, D_MODEL), lambda i: (i + nb, 0)),
                  _mod_spec(t, tm, 5), _vec_spec(), _vec_spec()],
        out_specs=_row_spec(tm),
        out_shape=jax.ShapeDtypeStruct((n, D_MODEL), f32),
        compiler_params=_cparams("arbitrary"),
        name="final_ln",
    )(x, y2, y2, mod, ln_g, ln_b)


def _pad_cols(w, width):
    return jnp.pad(w, ((0, 0), (0, width - w.shape[1])))


def _pad_rows(w, height):
    return jnp.pad(w, ((0, height - w.shape[0]), (0, 0)))


def _regroup_w_in(w_in, v1):
    d = w_in.shape[0]
    gdn, ml, rw = 0, 2056, 3600
    smalls = jnp.concatenate([w_in[:, gdn + 2048 : gdn + 2056], w_in[:, ml + 1536 : ml + 1544]], axis=1)
    vdown = jnp.zeros((d, LORA_V), w_in.dtype) if v1 is None else v1
    misc = jnp.concatenate([w_in[:, rw + 1536 : rw + 1792], _pad_cols(vdown, 128), _pad_cols(smalls, 128)], axis=1)
    cat = jnp.concatenate(
        [w_in[:, gdn : gdn + 2048], w_in[:, ml : ml + 1536], w_in[:, rw : rw + 1536], w_in[:, 5392:8464], misc], axis=1)
    assert cat.shape[1] == Z_WIDTH
    return cat.astype(bf16)


def _row(v, width=None):
    v = v.reshape(1, -1).astype(f32)
    return v if width is None else _pad_cols(v, width)


def kernel(x, c, ln_in_g, ln_in_b, ada_w, ada_b, w_in, gdn_conv, gdn_a_log, gdn_dt_bias, gdn_norm_g, mlstm_b_i, mlstm_b_f, mlstm_norm_g, rwkv_mu, rwkv_w0, rwkv_w2, rwkv_a0, rwkv_a2, rwkv_g2, rwkv_k_k, rwkv_k_a, rwkv_r_k, rwkv_lnx_g, rwkv_lnx_b, rwkv_v1, rwkv_mu_v1, rwkv_v0, rwkv_v2, w_branch, w_o, ln1_g, ln1_b, ln2_g, ln2_b, ffn_w_gate, ffn_w_up, ffn_w_down, moe_router, moe_router_b, moe_w_gate, moe_w_up, moe_w_down):
    b, t, d = x.shape
    n = b * t
    depth = w_in.shape[0]
    xf = x.reshape(n, d)

    c_pad = _pad_rows(c.astype(f32), 8)
    mods = []
    for layer in range(depth):
        mod = _ada_mod(c_pad, ada_w[layer].astype(bf16), _row(ada_b[layer]))[:b]
        mods.append(mod.reshape(b * 6, 1, d))

    xcur, h = _entry_ln(xf, _row(ln_in_g), _row(ln_in_b), mods[0], t)
    v_first = None
    out = None
    for layer in range(depth):
        mod = mods[layer]
        has_vres = layer > 0
        wcat = _regroup_w_in(w_in[layer], rwkv_v1[layer - 1] if has_vres else None)
        z = _matmul(h, wcat, 1024, 512, f32, "in_proj")

        gdn_par = jnp.zeros((8, LANES), f32).at[0, 0:4].set(gdn_a_log[layer]).at[1, 0:4].set(gdn_dt_bias[layer])
        y_a = _gdn(z, gdn_conv[layer].astype(f32), gdn_par, _row(gdn_norm_g[layer]), b, t)

        ml_par = jnp.zeros((8, LANES), f32).at[0, SM_ML_I : SM_ML_I + 4].set(mlstm_b_i[layer])
        ml_par = ml_par.at[0, SM_ML_F : SM_ML_F + 4].set(mlstm_b_f[layer])
        y_b = _mlstm(z, ml_par, _row(mlstm_norm_g[layer]), b, t)

        mu = rwkv_mu[layer].astype(f32)
        mu_v = rwkv_mu_v1[layer - 1].astype(f32) if has_vres else jnp.zeros((LORA_V,), f32)
        mu_misc = _row(jnp.concatenate([mu[1536:1792], mu_v]), 512)
        vecs = jnp.stack([
            rwkv_w0[layer], rwkv_a0[layer], rwkv_k_k[layer], rwkv_k_a[layer], rwkv_r_k[layer].reshape(-1),
            rwkv_lnx_g[layer], rwkv_lnx_b[layer],
            rwkv_v0[layer - 1] if has_vres else jnp.zeros((RWKV_WIDTH,), f32),
        ]).astype(f32)
        w2p = _pad_rows(rwkv_w2[layer], 128).astype(bf16)
        a2p = jnp.concatenate([jnp.zeros((LORA_W, RWKV_WIDTH), f32), rwkv_a2[layer]], axis=0).astype(bf16)
        v2p = _pad_rows(rwkv_v2[layer - 1], 128).astype(bf16) if has_vres else None
        y_c, v_first = _rwkv(z, v_first, _row(mu[:1536]), mu_misc, vecs, w2p, a2p, rwkv_g2[layer].astype(bf16), v2p, b, t)

        moe_layer = layer % 2 == 1
        xcur, h2 = _merge(y_a, y_b, y_c, z, xcur, w_branch[layer].astype(bf16), w_o[layer].astype(bf16), mod,
                          _row(ln1_g[layer]), _row(ln1_b[layer]), t, f32 if moe_layer else bf16)
        i = layer // 2
        if not moe_layer:
            mod_next = mods[layer + 1] if layer + 1 < depth else mod
            xcur, h = _ffn(h2, xcur, ffn_w_gate[i].astype(bf16), ffn_w_up[i].astype(bf16), ffn_w_down[i].astype(bf16),
                           mod, mod_next, _row(ln2_g[layer]), _row(ln2_b[layer]), t)
            out = xcur
        else:
            idx, gate = _router(h2, _pad_cols(moe_router[i].astype(f32), LANES), _row(moe_router_b[i], LANES))
            tables = _routing_tables(idx[:, :TOP_K], gate[:, :TOP_K], n)
            y2 = _moe(h2, *tables, moe_w_gate[i].astype(bf16), moe_w_up[i].astype(bf16), moe_w_down[i].astype(bf16))
            out = _final_ln(xcur, y2, mod, _row(ln2_g[layer]), _row(ln2_b[layer]), t)
            xcur = out
    return out.reshape(b, t, d)
```

```python
import functools
import math

import jax
import jax.numpy as jnp
from jax import lax
from jax.experimental import pallas as pl
from jax.experimental.pallas import tpu as pltpu

f32 = jnp.float32
bf16 = jnp.bfloat16
i32 = jnp.int32

D_MODEL = 1024
DEPTH = 2
CHUNK = 64
GDN_HEADS = 4
GDN_DK = 128
MLSTM_HEADS = 4
MLSTM_DQK = 64
MLSTM_DV = 128
MLSTM_SOFTCAP = 15.0
RWKV_HEADS = 8
RWKV_DH = 64
RWKV_WIDTH = RWKV_HEADS * RWKV_DH
LORA_W = 64
LORA_A = 64
LORA_V = 32
LORA_G = 128
N_EXPERTS = 8
TOP_K = 2
DEEPNORM_ALPHA = (2 * DEPTH) ** 0.25
LN_EPS = 1e-5
RWKV_GN_EPS = 64e-5

LANES = 128
HALF = LANES // 2

Z_GDN = 0
Z_MLSTM = 2048
Z_RWKV = 3584
Z_GATE = 5120
Z_MISC = 8192
Z_WIDTH = 8704
MISC_SMALLS = 384
SM_GDN_A, SM_GDN_B, SM_ML_I, SM_ML_F = 0, 4, 8, 12

MOE_ROWS = 1024
VMEM_LIMIT = 56 * 1024 * 1024


def _cparams(*sem):
    return pltpu.CompilerParams(dimension_semantics=sem, vmem_limit_bytes=VMEM_LIMIT)


def _dot(a, b):
    return jnp.dot(a.astype(bf16), b.astype(bf16), preferred_element_type=f32)


def _dot_nt(a, b):
    return lax.dot_general(a.astype(bf16), b.astype(bf16), (((1,), (1,)), ((), ())), preferred_element_type=f32)


def _dot_tn(a, b):
    return lax.dot_general(a.astype(bf16), b.astype(bf16), (((0,), (0,)), ((), ())), preferred_element_type=f32)


def _sigmoid(x):
    return 1.0 / (1.0 + jnp.exp(-x))


def _silu(x):
    return x * _sigmoid(x)


def _softplus(x):
    return jnp.maximum(x, 0.0) + jnp.log1p(jnp.exp(-jnp.abs(x)))


def _iota2(shape, axis):
    return lax.broadcasted_iota(i32, shape, axis)


def _cumsum_rows(x):
    n = x.shape[0]
    rows = _iota2(x.shape, 0)
    sh = 1
    while sh < n:
        x = x + jnp.where(rows >= sh, pltpu.roll(x, sh, 0), 0.0)
        sh *= 2
    return x


def _unit_lower_inverses(ps, nilpotent):
    n = ps[0].shape[0]
    eye = (_iota2((n, n), 0) == _iota2((n, n), 1)).astype(f32)
    ts = [eye + p for p in ps]
    for _ in range(int(math.log2(nilpotent)) - 1):
        ps = [_dot(p, p) for p in ps]
        ts = [t + _dot(t, p) for t, p in zip(ts, ps)]
    return ts


def _rows(*xs):
    return jnp.concatenate(xs, axis=0)


def _lanes(*xs):
    return jnp.concatenate(xs, axis=1)


def _half_masks(shape):
    lane = _iota2(shape, len(shape) - 1)
    return lane < HALF, lane >= HALF


def _stack2(x):
    lo, _ = _half_masks(x.shape)
    return _rows(jnp.where(lo, x, 0.0), jnp.where(lo, 0.0, x))


def _unstack2(x):
    c = x.shape[0] // 2
    return x[:c] + x[c:]


def _pair_masks(c):
    r = _iota2((2 * c, 2 * c), 0)
    q = _iota2((2 * c, 2 * c), 1)
    same = (r < c) == (q < c)
    rr = jnp.where(r < c, r, r - c)
    qq = jnp.where(q < c, q, q - c)
    return jnp.logical_and(same, rr >= qq), jnp.logical_and(same, rr > qq)


def _half_col(c, v0, v1):
    return jnp.where(_iota2((2 * c, 1), 0) < c, v0, v1)


def _ln_rows(r, g, b):
    mu = jnp.mean(r, -1, keepdims=True)
    d = r - mu
    var = jnp.mean(d * d, -1, keepdims=True)
    return d * lax.rsqrt(var + LN_EPS) * g + b


def _head64_sums(x):
    lo, _ = _half_masks((x.shape[0], LANES))
    outs = []
    for gi in range(x.shape[1] // LANES):
        xs = x[:, gi * LANES : (gi + 1) * LANES]
        s_lo = jnp.sum(jnp.where(lo, xs, 0.0), -1, keepdims=True)
        s_hi = jnp.sum(jnp.where(lo, 0.0, xs), -1, keepdims=True)
        outs.append(jnp.where(lo, s_lo, s_hi))
    return jnp.concatenate(outs, axis=1)


def _ada_body(c_ref, w_ref, b_ref, o_ref):
    o_ref[...] = _dot(_silu(c_ref[...]), w_ref[...]) + b_ref[...]


def _ada_mod(c_pad, w, b):
    n = w.shape[1]
    tn = n // 4
    return pl.pallas_call(
        _ada_body,
        grid=(n // tn,),
        in_specs=[
            pl.BlockSpec(c_pad.shape, lambda j: (0, 0)),
            pl.BlockSpec((w.shape[0], tn), lambda j: (0, j)),
            pl.BlockSpec((1, tn), lambda j: (0, j)),
        ],
        out_specs=pl.BlockSpec((c_pad.shape[0], tn), lambda j: (0, j)),
        out_shape=jax.ShapeDtypeStruct((c_pad.shape[0], n), f32),
        compiler_params=_cparams("arbitrary"),
        name="ada_mod",
    )(c_pad, w, b)


def _entry_body(x_ref, g_ref, b_ref, sc_ref, sh_ref, xn_ref, h_ref):
    xn = _ln_rows(x_ref[...], g_ref[...], b_ref[...])
    xn_ref[...] = xn
    h_ref[...] = (xn * (1.0 + sc_ref[...]) + sh_ref[...]).astype(h_ref.dtype)


def _mod_spec(rows_per_batch, tm, slot):
    return pl.BlockSpec((None, 1, D_MODEL), lambda i: ((i * tm // rows_per_batch) * 6 + slot, 0, 0))


def _row_spec(tm, width=D_MODEL):
    return pl.BlockSpec((tm, width), lambda i: (i, 0))


def _vec_spec(width=D_MODEL):
    return pl.BlockSpec((1, width), lambda i: (0, 0))


def _entry_ln(x, g, b, mod, t):
    n = x.shape[0]
    tm = 512
    return pl.pallas_call(
        _entry_body,
        grid=(n // tm,),
        in_specs=[_row_spec(tm), _vec_spec(), _vec_spec(), _mod_spec(t, tm, 1), _mod_spec(t, tm, 0)],
        out_specs=[_row_spec(tm), _row_spec(tm)],
        out_shape=[jax.ShapeDtypeStruct((n, D_MODEL), f32), jax.ShapeDtypeStruct((n, D_MODEL), bf16)],
        compiler_params=_cparams("arbitrary"),
        name="entry_ln",
    )(x, g, b, mod, mod)


def _mm_body(x_ref, w_ref, o_ref):
    o_ref[...] = jnp.dot(x_ref[...], w_ref[...], preferred_element_type=f32).astype(o_ref.dtype)


def _matmul(x, w, tm, tn, out_dtype, name):
    m, k = x.shape
    n = w.shape[1]
    return pl.pallas_call(
        _mm_body,
        grid=(m // tm, n // tn),
        in_specs=[pl.BlockSpec((tm, k), lambda i, j: (i, 0)), pl.BlockSpec((k, tn), lambda i, j: (0, j))],
        out_specs=pl.BlockSpec((tm, tn), lambda i, j: (i, j)),
        out_shape=jax.ShapeDtypeStruct((m, n), out_dtype),
        compiler_params=_cparams("arbitrary", "arbitrary"),
        name=name,
    )(x, w)


def _seq_spec(nb, c, width, col_block):
    return pl.BlockSpec((nb, c, width), lambda n: (0, n, col_block))


def _bcast_spec(shape):
    return pl.BlockSpec(shape, lambda n: (0,) * len(shape))


def _lockstep(*chains):
    pending = [(g, None) for g in chains]
    while pending:
        requests = []
        for g, value in pending:
            try:
                requests.append((g, g.send(value)))
            except StopIteration:
                pass
        pending = [(g, [thunk() for thunk in thunks]) for g, thunks in requests]


def _unit_lower_inverses_staged(ps, nilpotent):
    n = ps[0].shape[0]
    eye = (_iota2((n, n), 0) == _iota2((n, n), 1)).astype(f32)
    ts = [eye + p for p in ps]
    for _ in range(int(math.log2(nilpotent)) - 1):
        ps = yield [functools.partial(_dot, p, p) for p in ps]
        prods = yield [functools.partial(_dot, t, p) for t, p in zip(ts, ps)]
        ts = [t + d for t, d in zip(ts, prods)]
    return ts


def _gdn_body(qkv_ref, zg_ref, sm_ref, conv_ref, par_ref, ng_ref, o_ref, s_ref, buf_ref):
    nb, c, w3 = qkv_ref.shape

    @pl.when(pl.program_id(0) == 0)
    def _():
        s_ref[...] = jnp.zeros(s_ref.shape, f32)
        buf_ref[:, 0:8, :] = jnp.zeros((nb, 8, w3), f32)

    incl, strict = _pair_masks(c)
    pairs = range(GDN_HEADS // 2)
    heads = [(p, e) for p in pairs for e in range(2)]

    def col(a, lane):
        return a[:, lane : lane + 1]

    def chain(bi):
        raw = qkv_ref[bi]
        buf_ref[bi, 8 : 8 + c, :] = raw
        conv = conv_ref[3:4, :] * raw
        for j in range(3):
            conv = conv + conv_ref[j : j + 1, :] * buf_ref[bi, 5 + j : 5 + j + c, :]
        buf_ref[bi, 0:8, :] = buf_ref[bi, c : c + 8, :]
        x = _silu(conv)

        sm = sm_ref[bi]
        gcum = _cumsum_rows(-jnp.exp(par_ref[0:1, :]) * _softplus(sm + par_ref[1:2, :]))
        gcum_t = gcum.T
        beta_all = _sigmoid(sm)
        zg = zg_ref[bi]

        def head(i, h):
            return x[:, i * 512 + h * 128 : i * 512 + (h + 1) * 128]

        k_st, kb_st, q_st, v_st, g_st, eg_st, beta_st, decay, g_last = [], [], [], [], [], [], [], [], []
        for p in pairs:
            h0, h1 = 2 * p, 2 * p + 1
            qs, ks = [], []
            for h in (h0, h1):
                q = head(0, h)
                k = head(1, h)
                qs.append(q * (lax.rsqrt(jnp.sum(q * q, -1, keepdims=True) + 1e-6) * GDN_DK**-0.5))
                ks.append(k * lax.rsqrt(jnp.sum(k * k, -1, keepdims=True) + 1e-6))
            g = _rows(col(gcum, SM_GDN_A + h0), col(gcum, SM_GDN_A + h1))
            g_row = _lanes(gcum_t[SM_GDN_A + h0 : SM_GDN_A + h0 + 1, :], gcum_t[SM_GDN_A + h1 : SM_GDN_A + h1 + 1, :])
            beta = _rows(col(beta_all, SM_GDN_B + h0), col(beta_all, SM_GDN_B + h1))
            k2 = _rows(*ks)
            k_st.append(k2)
            kb_st.append(k2 * beta)
            q_st.append(_rows(*qs))
            v_st.append(_rows(head(2, h0), head(2, h1)))
            g_st.append(g)
            eg_st.append(jnp.exp(g))
            beta_st.append(beta)
            decay.append(jnp.where(incl, jnp.exp(g - g_row), 0.0))
            g_last.append((gcum[c - 1 : c, SM_GDN_A + h0 : SM_GDN_A + h0 + 1], gcum[c - 1 : c, SM_GDN_A + h1 : SM_GDN_A + h1 + 1]))

        m1 = yield [functools.partial(_dot_nt, _rows(kb_st[p], q_st[p]), k_st[p]) for p in pairs]
        t_inv = yield from _unit_lower_inverses_staged(
            [-jnp.where(strict, m1[p][: 2 * c] * decay[p], 0.0) for p in pairs], c)
        qk = [m1[p][2 * c :] * decay[p] for p in pairs]
        uw = yield [functools.partial(_dot, t_inv[p], _lanes(v_st[p] * beta_st[p], kb_st[p] * eg_st[p])) for p in pairs]
        s_old = [s_ref[bi, 2 * p + e] for p, e in heads]
        qd = [q_st[p] * eg_st[p] for p in pairs]
        ws_qs = yield [functools.partial(_dot, _rows(uw[p][e * c : (e + 1) * c, 128:], qd[p][e * c : (e + 1) * c]), s)
                       for (p, e), s in zip(heads, s_old)]
        v_new = [uw[p][e * c : (e + 1) * c, :128] - m[:c] for (p, e), m in zip(heads, ws_qs)]
        o_st = yield [functools.partial(_dot, qk[p], _rows(v_new[2 * p], v_new[2 * p + 1])) for p in pairs]
        k_dec = [k_st[p][e * c : (e + 1) * c] * jnp.exp(g_last[p][e] - g_st[p][e * c : (e + 1) * c]) for p, e in heads]
        s_add = yield [functools.partial(_dot_tn, kd, vn) for kd, vn in zip(k_dec, v_new)]
        for i, (p, e) in enumerate(heads):
            h = 2 * p + e
            s_ref[bi, h] = s_old[i] * jnp.exp(g_last[p][e]) + s_add[i]
            o = ws_qs[i][c:] + o_st[p][e * c : (e + 1) * c]
            o = o * lax.rsqrt(jnp.mean(o * o, -1, keepdims=True) + 1e-6) * ng_ref[...]
            o_ref[bi, :, h * 128 : (h + 1) * 128] = o * _silu(zg[:, h * 128 : (h + 1) * 128])

    _lockstep(*(chain(bi) for bi in range(nb)))


def _gdn(z, conv_w, par, norm_g, b, t):
    c = CHUNK
    z3 = z.reshape(b, t, z.shape[1])
    out = pl.pallas_call(
        _gdn_body,
        grid=(t // c,),
        in_specs=[
            _seq_spec(b, c, 1536, Z_GDN // 1536),
            _seq_spec(b, c, 512, (Z_GDN + 1536) // 512),
            _seq_spec(b, c, 128, (Z_MISC + MISC_SMALLS) // 128),
            _bcast_spec((4, 1536)),
            _bcast_spec((8, 128)),
            _bcast_spec((1, 128)),
        ],
        out_specs=_seq_spec(b, c, 512, 0),
        out_shape=jax.ShapeDtypeStruct((b, t, 512), f32),
        scratch_shapes=[pltpu.VMEM((b, GDN_HEADS, 128, 128), f32), pltpu.VMEM((b, c + 8, 1536), f32)],
        compiler_params=_cparams("arbitrary"),
        name="gdn",
    )(z3, z3, z3, conv_w, par, norm_g)
    return out.reshape(b * t, 512)


def _mlstm_body(qk_ref, v_ref, og_ref, sm_ref, par_ref, ng_ref, o_ref, cs_ref, n_ref, m_ref):
    nb, c, _ = qk_ref.shape
    assert c == HALF

    @pl.when(pl.program_id(0) == 0)
    def _():
        cs_ref[...] = jnp.zeros(cs_ref.shape, f32)
        n_ref[...] = jnp.zeros(n_ref.shape, f32)
        m_ref[...] = jnp.zeros(m_ref.shape, f32)

    def cap(x):
        return MLSTM_SOFTCAP * jnp.tanh(x / MLSTM_SOFTCAP)

    incl, _ = _pair_masks(c)
    lane_lo = _iota2((1, LANES), 1) < HALF
    pairs = range(MLSTM_HEADS // 2)

    def col(a, lane):
        return a[:, lane : lane + 1]

    def row(a, lane):
        return a[lane : lane + 1, :]

    def chain(bi):
        capped = cap(sm_ref[bi] + par_ref[0:1, :])
        log_i = capped
        bcum = _cumsum_rows(-_softplus(-capped))
        log_i_t = log_i.T
        bcum_t = bcum.T
        qk = qk_ref[bi]
        v_all = v_ref[bi]
        og = og_ref[bi]
        q_st, k_st, v_st, bc, li, b_last, m_old, dmat, m_intra = [], [], [], [], [], [], [], [], []
        for p in pairs:
            h0, h1 = 2 * p, 2 * p + 1
            q_st.append(_stack2(qk[:, p * 128 : (p + 1) * 128]))
            k_st.append(_stack2(qk[:, 256 + p * 128 : 256 + (p + 1) * 128] * MLSTM_DQK**-0.5))
            v_st.append(_rows(v_all[:, h0 * 128 : (h0 + 1) * 128], v_all[:, h1 * 128 : (h1 + 1) * 128]))
            bc.append(_rows(col(bcum, SM_ML_F + h0), col(bcum, SM_ML_F + h1)))
            li.append(_rows(col(log_i, SM_ML_I + h0), col(log_i, SM_ML_I + h1)))
            b_row = _lanes(row(bcum_t, SM_ML_F + h0), row(bcum_t, SM_ML_F + h1))
            li_row = _lanes(row(log_i_t, SM_ML_I + h0), row(log_i_t, SM_ML_I + h1))
            b_last.append(_half_col(c, bcum[c - 1 : c, SM_ML_F + h0 : SM_ML_F + h0 + 1],
                                    bcum[c - 1 : c, SM_ML_F + h1 : SM_ML_F + h1 + 1]))
            m_old.append(_half_col(c, m_ref[bi, h0 : h0 + 1, 0:1], m_ref[bi, h1 : h1 + 1, 0:1]))
            log_d = jnp.where(incl, bc[p] - b_row + li_row, -jnp.inf)
            m_intra.append(jnp.max(log_d, -1, keepdims=True))
            dmat.append(jnp.exp(log_d - m_intra[p]))

        cs_old = [cs_ref[bi, p] for p in pairs]
        prods = yield ([functools.partial(_dot_nt, q_st[p], k_st[p]) for p in pairs]
                       + [functools.partial(_dot, q_st[p], cs_old[p]) for p in pairs])
        pm = [dmat[p] * prods[p] for p in pairs]
        q_cs = prods[len(pairs):]
        kw_st, s_old, m_new = [], [], []
        for p in pairs:
            log_e = b_last[p] - bc[p] + li[p]
            m_end = _half_col(c, jnp.max(log_e[:c], 0, keepdims=True), jnp.max(log_e[c:], 0, keepdims=True))
            m_new.append(jnp.maximum(b_last[p] + m_old[p], m_end))
            s_old.append(jnp.exp(b_last[p] + m_old[p] - m_new[p]))
            kw_st.append(k_st[p] * (jnp.exp(log_e - m_end) * jnp.exp(m_end - m_new[p])))
        prods = yield ([functools.partial(_dot, pm[p], v_st[p]) for p in pairs]
                       + [functools.partial(_dot_tn, kw_st[p], v_st[p]) for p in pairs])
        num_intra, cs_add = prods[: len(pairs)], prods[len(pairs):]
        for p in pairs:
            nvec = n_ref[bi, p]
            den_intra = jnp.sum(pm[p], -1, keepdims=True)
            m_t = jnp.maximum(bc[p] + m_old[p], m_intra[p])
            s_inter = jnp.exp(bc[p] + m_old[p] - m_t)
            s_intra = jnp.exp(m_intra[p] - m_t)
            num = s_inter * q_cs[p] + s_intra * num_intra[p]
            den = s_inter * jnp.sum(q_st[p] * nvec, -1, keepdims=True) + s_intra * den_intra
            hc = num / jnp.maximum(jnp.abs(den), jnp.exp(-m_t))
            so0, so1 = s_old[p][0:1], s_old[p][c : c + 1]
            cs_ref[bi, p] = s_old[p] * cs_old[p] + cs_add[p]
            n_ref[bi, p] = jnp.where(lane_lo, so0, so1) * nvec + jnp.sum(kw_st[p], 0, keepdims=True)
            for e in range(2):
                h = 2 * p + e
                m_ref[bi, h : h + 1, :] = jnp.broadcast_to(m_new[p][e * c : e * c + 1], (1, LANES))
                hs = hc[e * c : (e + 1) * c]
                hs = hs * lax.rsqrt(jnp.mean(hs * hs, -1, keepdims=True) + 1e-6) * ng_ref[:, h * 128 : (h + 1) * 128]
                o_ref[bi, :, h * 128 : (h + 1) * 128] = hs * _sigmoid(og[:, h * 128 : (h + 1) * 128])

    _lockstep(*(chain(bi) for bi in range(nb)))


def _mlstm(z, par, norm_g, b, t):
    c = CHUNK
    z3 = z.reshape(b, t, z.shape[1])
    out = pl.pallas_call(
        _mlstm_body,
        grid=(t // c,),
        in_specs=[
            _seq_spec(b, c, 512, Z_MLSTM // 512),
            _seq_spec(b, c, 512, (Z_MLSTM + 512) // 512),
            _seq_spec(b, c, 512, (Z_MLSTM + 1024) // 512),
            _seq_spec(b, c, 128, (Z_MISC + MISC_SMALLS) // 128),
            _bcast_spec((8, 128)),
            _bcast_spec((1, 512)),
        ],
        out_specs=_seq_spec(b, c, 512, 0),
        out_shape=jax.ShapeDtypeStruct((b, t, 512), f32),
        scratch_shapes=[
            pltpu.VMEM((b, MLSTM_HEADS // 2, 128, 128), f32),
            pltpu.VMEM((b, MLSTM_HEADS // 2, 1, 128), f32),
            pltpu.VMEM((b, 8, 128), f32),
        ],
        compiler_params=_cparams("arbitrary"),
        name="mlstm",
    )(z3, z3, z3, z3, par, norm_g)
    return out.reshape(b * t, 512)


def _rwkv_body(*refs, has_vres):
    if has_vres:
        (r_ref, k_ref, v_ref, misc_ref, vf_ref, mu_ref, mum_ref, vecs_ref, w2_ref, a2_ref, g2_ref, v2_ref,
         y_ref, st_ref, tail_ref, tailm_ref) = refs
    else:
        (r_ref, k_ref, v_ref, misc_ref, mu_ref, mum_ref, vecs_ref, w2_ref, a2_ref, g2_ref,
         y_ref, vf_out_ref, st_ref, tail_ref, tailm_ref) = refs
    nb, c, _ = r_ref.shape

    @pl.when(pl.program_id(0) == 0)
    def _():
        st_ref[...] = jnp.zeros(st_ref.shape, f32)
        tail_ref[:, 0:8, :] = jnp.zeros((nb, 8, tail_ref.shape[2]), f32)
        tailm_ref[:, 0:8, :] = jnp.zeros((nb, 8, tailm_ref.shape[2]), f32)

    incl, strict = _pair_masks(c)
    same_head = (_iota2((LANES, LANES), 0) < HALF) == (_iota2((LANES, LANES), 1) < HALF)
    pairs = range(RWKV_HEADS // 2)
    sl = [slice(p * 128, (p + 1) * 128) for p in pairs]
    c2 = 2 * c
    w0, a0, k_k, k_a, r_k, lnx_g, lnx_b, v0 = (vecs_ref[i : i + 1, :] for i in range(8))

    def chain(bi):
        raw = jnp.concatenate([r_ref[bi], k_ref[bi], v_ref[bi]], axis=1)
        tail_ref[bi, 8 : 8 + c, :] = raw
        prev = tail_ref[bi, 7 : 7 + c, :]
        tail_ref[bi, 0:8, :] = tail_ref[bi, c : c + 8, :]
        rkv = raw + mu_ref[...] * (prev - raw)
        rawm = misc_ref[bi]
        tailm_ref[bi, 8 : 8 + c, :] = rawm
        prevm = tailm_ref[bi, 7 : 7 + c, :]
        tailm_ref[bi, 0:8, :] = tailm_ref[bi, c : c + 8, :]
        misc = rawm + mum_ref[...] * (prevm - rawm)

        r = rkv[:, 0:512]
        k = rkv[:, 512:1024]
        v = rkv[:, 1024:1536]
        wa = misc[:, 0:128]
        lora = [functools.partial(_dot, jnp.tanh(wa), w2_ref[...]), functools.partial(_dot, wa, a2_ref[...]),
                functools.partial(_dot, _sigmoid(misc[:, 128:256]), g2_ref[...])]
        if has_vres:
            lora.append(functools.partial(_dot, misc[:, 256:384], v2_ref[...]))
        lora = yield lora
        log_w = -jnp.exp(-_softplus(-(w0 + lora[0])) - 0.5)
        a = _sigmoid(a0 + lora[1])
        g = lora[2]
        if has_vres:
            v = v + (vf_ref[bi] - v) * _sigmoid(v0 + lora[3])
        else:
            vf_out_ref[bi] = v
        kk = k * k_k
        kk = kk * lax.rsqrt(_head64_sums(kk * kk) + 1e-6)
        k = k * (1.0 + (a - 1.0) * k_a)
        alpha = -kk
        beta = kk * a
        lw = _cumsum_rows(log_w)
        lw_last = lw[c - 1 : c, :]
        inv = jnp.exp(-lw)
        al_bar = alpha * jnp.exp(lw - log_w)
        be_hat = beta * inv
        k_hat = k * inv
        r_bar = r * jnp.exp(lw)
        to_end = jnp.exp(lw_last - lw)
        be_end = beta * to_end
        k_end = k * to_end
        g_chunk = jnp.exp(lw_last)

        al_st = [_stack2(al_bar[:, s]) for s in sl]
        v_st = [_stack2(v[:, s]) for s in sl]
        m1 = yield [functools.partial(_dot_nt, _rows(al_st[p], _stack2(r_bar[:, s])),
                                      _rows(_stack2(be_hat[:, s]), _stack2(k_hat[:, s]))) for p, s in zip(pairs, sl)]
        t_inv = yield from _unit_lower_inverses_staged([jnp.where(strict, m[:c2, :c2], 0.0) for m in m1], c)
        l_ak = [jnp.where(strict, m[:c2, c2:], 0.0) for m in m1]
        q_rb = [jnp.where(incl, m[c2:, :c2], 0.0) for m in m1]
        q_rk = [jnp.where(incl, m[c2:, c2:], 0.0) for m in m1]
        m2 = yield [functools.partial(_dot, _rows(l_ak[p], q_rk[p]), v_st[p]) for p in pairs]
        m3 = yield [functools.partial(_dot, t_inv[p], _lanes(al_st[p], m2[p][:c2])) for p in pairs]
        st_old = [st_ref[bi, p] for p in pairs]
        m4 = yield [functools.partial(_dot_nt, _rows(_unstack2(m3[p][:, :128]), r_bar[:, sl[p]]), st_old[p]) for p in pairs]
        u = [m4[p][:c] + _unstack2(m3[p][:, 128:]) for p in pairs]
        m56 = yield ([functools.partial(_dot, q_rb[p], _stack2(u[p])) for p in pairs]
                     + [functools.partial(_dot_tn, _rows(u[p], v[:, sl[p]]), _rows(be_end[:, sl[p]], k_end[:, sl[p]]))
                        for p in pairs])
        m5, m6 = m56[: len(pairs)], m56[len(pairs):]
        ys = []
        for p in pairs:
            st_ref[bi, p] = g_chunk[:, sl[p]] * st_old[p] + jnp.where(same_head, m6[p], 0.0)
            ys.append(m4[p][c:] + _unstack2(m5[p]) + _unstack2(m2[p][c2:]))
        y = jnp.concatenate(ys, axis=1)
        mean = _head64_sums(y) * (1.0 / RWKV_DH)
        d = y - mean
        var = _head64_sums(d * d) * (1.0 / RWKV_DH)
        y = d * lax.rsqrt(var + RWKV_GN_EPS) * lnx_g + lnx_b
        bonus = _head64_sums(r * k * r_k) * v
        y_ref[bi] = (y + bonus) * g

    _lockstep(*(chain(bi) for bi in range(nb)))


def _rwkv(z, v_first, mu, mu_misc, vecs, w2p, a2p, g2, v2p, b, t):
    c = CHUNK
    has_vres = v_first is not None
    z3 = z.reshape(b, t, z.shape[1])
    blk = _seq_spec(b, c, 512, 0)
    in_specs = [_seq_spec(b, c, 512, (Z_RWKV + 512 * i) // 512) for i in range(3)] + [_seq_spec(b, c, 512, Z_MISC // 512)]
    args = [z3, z3, z3, z3]
    if has_vres:
        in_specs.append(blk)
        args.append(v_first.reshape(b, t, 512))
    in_specs += [_bcast_spec(s) for s in ((1, 1536), (1, 512), (8, 512), (128, 512), (128, 512), (128, 512))]
    args += [mu, mu_misc, vecs, w2p, a2p, g2]
    if has_vres:
        in_specs.append(_bcast_spec((128, 512)))
        args.append(v2p)
    out_sd = jax.ShapeDtypeStruct((b, t, 512), f32)
    res = pl.pallas_call(
        functools.partial(_rwkv_body, has_vres=has_vres),
        grid=(t // c,),
        in_specs=in_specs,
        out_specs=blk if has_vres else [blk, blk],
        out_shape=out_sd if has_vres else [out_sd, out_sd],
        scratch_shapes=[
            pltpu.VMEM((b, RWKV_HEADS // 2, 128, 128), f32),
            pltpu.VMEM((b, c + 8, 1536), f32),
            pltpu.VMEM((b, c + 8, 512), f32),
        ],
        compiler_params=_cparams("arbitrary"),
        name="rwkv_res" if has_vres else "rwkv",
    )(*args)
    if has_vres:
        return res.reshape(b * t, 512), v_first
    return res[0].reshape(b * t, 512), res[1].reshape(b * t, 512)


def _merge_body(ya_ref, yb_ref, yc_ref, ga_ref, gb_ref, gc_ref, x_ref, wb_ref, wo_ref, gt_ref, g_ref, b_ref,
                sc_ref, sh_ref, xn_ref, h_ref):
    merged = _sigmoid(ga_ref[...]) * _dot(ya_ref[...], wb_ref[0])
    merged = merged + _sigmoid(gb_ref[...]) * _dot(yb_ref[...], wb_ref[1])
    merged = merged + _sigmoid(gc_ref[...]) * _dot(yc_ref[...], wb_ref[2])
    y = _dot(merged, wo_ref[...])
    xn = _ln_rows(DEEPNORM_ALPHA * x_ref[...] + (1.0 + gt_ref[...]) * y, g_ref[...], b_ref[...])
    xn_ref[...] = xn
    h_ref[...] = (xn * (1.0 + sc_ref[...]) + sh_ref[...]).astype(h_ref.dtype)


def _merge(ya, yb, yc, z, x, wb, wo, mod, ln_g, ln_b, t, h_dtype):
    n = x.shape[0]
    tm = 512
    gate = lambda i: pl.BlockSpec((tm, 1024), lambda r: (r, Z_GATE // 1024 + i))
    return pl.pallas_call(
        _merge_body,
        grid=(n // tm,),
        in_specs=[
            _row_spec(tm, 512), _row_spec(tm, 512), _row_spec(tm, 512), gate(0), gate(1), gate(2), _row_spec(tm),
            pl.BlockSpec((3, 512, D_MODEL), lambda r: (0, 0, 0)),
            pl.BlockSpec((D_MODEL, D_MODEL), lambda r: (0, 0)),
            _mod_spec(t, tm, 2), _vec_spec(), _vec_spec(), _mod_spec(t, tm, 4), _mod_spec(t, tm, 3),
        ],
        out_specs=[_row_spec(tm), _row_spec(tm)],
        out_shape=[jax.ShapeDtypeStruct((n, D_MODEL), f32), jax.ShapeDtypeStruct((n, D_MODEL), h_dtype)],
        compiler_params=_cparams("arbitrary"),
        name="merge",
    )(ya, yb, yc, z, z, z, x, wb, wo, mod, ln_g, ln_b, mod, mod)


def _ffn_body(h_ref, x_ref, wg_ref, wu_ref, wd_ref, gt_ref, g_ref, b_ref, sc_ref, sh_ref, xn_ref, hn_ref, acc_ref):
    f = pl.program_id(1)

    @pl.when(f == 0)
    def _():
        acc_ref[...] = jnp.zeros(acc_ref.shape, f32)

    h = h_ref[...]
    act = _silu(_dot(h, wg_ref[...])) * _dot(h, wu_ref[...])
    acc_ref[...] += _dot(act, wd_ref[...])

    @pl.when(f == pl.num_programs(1) - 1)
    def _():
        xn = _ln_rows(DEEPNORM_ALPHA * x_ref[...] + (1.0 + gt_ref[...]) * acc_ref[...], g_ref[...], b_ref[...])
        xn_ref[...] = xn
        hn_ref[...] = (xn * (1.0 + sc_ref[...]) + sh_ref[...]).astype(hn_ref.dtype)


def _ffn(h, x, wg, wu, wd, mod, mod_next, ln_g, ln_b, t):
    n = x.shape[0]
    dff = wg.shape[1]
    tm, tf = 1024, 256
    row = pl.BlockSpec((tm, D_MODEL), lambda i, f: (i, 0))
    vec = pl.BlockSpec((1, D_MODEL), lambda i, f: (0, 0))
    modspec = lambda slot: pl.BlockSpec((None, 1, D_MODEL), lambda i, f: ((i * tm // t) * 6 + slot, 0, 0))
    return pl.pallas_call(
        _ffn_body,
        grid=(n // tm, dff // tf),
        in_specs=[
            row, row,
            pl.BlockSpec((D_MODEL, tf), lambda i, f: (0, f)),
            pl.BlockSpec((D_MODEL, tf), lambda i, f: (0, f)),
            pl.BlockSpec((tf, D_MODEL), lambda i, f: (f, 0)),
            modspec(5), vec, vec, modspec(1), modspec(0),
        ],
        out_specs=[row, row],
        out_shape=[jax.ShapeDtypeStruct((n, D_MODEL), f32), jax.ShapeDtypeStruct((n, D_MODEL), bf16)],
        scratch_shapes=[pltpu.VMEM((tm, D_MODEL), f32)],
        compiler_params=_cparams("arbitrary", "arbitrary"),
        name="ffn",
    )(h, x, wg, wu, wd, mod, ln_g, ln_b, mod_next, mod_next)


def _router_body(h_ref, w_ref, b_ref, idx_ref, gate_ref):
    logits = jnp.dot(h_ref[...], w_ref[...], precision=lax.Precision.HIGHEST, preferred_element_type=f32) + b_ref[...]
    lane = _iota2(logits.shape, 1)
    logits = jnp.where(lane < N_EXPERTS, logits, -jnp.inf)
    m1 = jnp.max(logits, -1, keepdims=True)
    i1 = jnp.min(jnp.where(logits == m1, lane, LANES), -1, keepdims=True)
    rest = jnp.where(lane == i1, -jnp.inf, logits)
    m2 = jnp.max(rest, -1, keepdims=True)
    i2 = jnp.min(jnp.where(rest == m2, lane, LANES), -1, keepdims=True)
    e = jnp.exp(m2 - m1)
    g1 = 1.0 / (1.0 + e)
    g2 = e / (1.0 + e)
    idx_ref[...] = jnp.where(lane == 0, i1, jnp.where(lane == 1, i2, 0))
    gate_ref[...] = jnp.where(lane == 0, g1, jnp.where(lane == 1, g2, 0.0))


def _router(h, w_pad, b_pad):
    n = h.shape[0]
    tm = 512
    return pl.pallas_call(
        _router_body,
        grid=(n // tm,),
        in_specs=[_row_spec(tm), pl.BlockSpec((D_MODEL, LANES), lambda i: (0, 0)), _vec_spec(LANES)],
        out_specs=[_row_spec(tm, LANES), _row_spec(tm, LANES)],
        out_shape=[jax.ShapeDtypeStruct((n, LANES), i32), jax.ShapeDtypeStruct((n, LANES), f32)],
        compiler_params=_cparams("arbitrary"),
        name="router",
    )(h, w_pad, b_pad)


def _moe_body(be_ref, live_ref, tok_ref, dst_ref, gate_ref, h_hbm, wg_ref, wu_ref, wd_ref, y_hbm,
              xbuf, xb16, acc_ref, obuf, gsem, ssem, *, n_dst):
    j = pl.program_id(0)
    f = pl.program_id(1)
    nf = pl.num_programs(1)
    rows = xbuf.shape[0]
    live = live_ref[j] > 0

    def row_in(r):
        return pltpu.make_async_copy(h_hbm.at[pl.ds(tok_ref[0, 0, r], 1), :], xbuf.at[pl.ds(r, 1), :], gsem)

    def row_out(r):
        return pltpu.make_async_copy(obuf.at[pl.ds(r, 1), :], y_hbm.at[pl.ds(dst_ref[0, 0, r], 1), :], ssem)

    @pl.when(jnp.logical_and(live, f == 0))
    def _():
        def start(r, carry):
            row_in(r).start()
            return carry

        def wait(r, carry):
            row_in(r).wait()
            return carry

        lax.fori_loop(0, rows, start, 0)
        lax.fori_loop(0, rows, wait, 0)
        xb16[...] = xbuf[...].astype(bf16)
        acc_ref[...] = jnp.zeros(acc_ref.shape, f32)

    @pl.when(live)
    def _():
        x = xb16[...]
        act = _silu(_dot(x, wg_ref[...])) * _dot(x, wu_ref[...])
        acc_ref[...] += _dot(act, wd_ref[...])

    @pl.when(jnp.logical_and(live, f == nf - 1))
    def _():
        obuf[...] = acc_ref[...] * gate_ref[...]

        def start(r, carry):
            @pl.when(dst_ref[0, 0, r] < n_dst)
            def _():
                row_out(r).start()

            return carry

        def wait(r, carry):
            @pl.when(dst_ref[0, 0, r] < n_dst)
            def _():
                row_out(r).wait()

            return carry

        lax.fori_loop(0, rows, start, 0)
        lax.fori_loop(0, rows, wait, 0)


def _moe(h, block_e, block_live, row_tok, row_dst, row_gate, wg, wu, wd):
    n = h.shape[0]
    n_blocks = block_e.shape[0]
    rb = MOE_ROWS
    dff = wg.shape[2]
    tf = 512
    smem_blk = pl.BlockSpec((1, 1, rb), lambda j, f, be, lv: (j, 0, 0), memory_space=pltpu.SMEM)
    grid_spec = pltpu.PrefetchScalarGridSpec(
        num_scalar_prefetch=2,
        grid=(n_blocks, dff // tf),
        in_specs=[
            smem_blk,
            smem_blk,
            pl.BlockSpec((rb, 1), lambda j, f, be, lv: (j, 0)),
            pl.BlockSpec(memory_space=pl.ANY),
            pl.BlockSpec((None, D_MODEL, tf), lambda j, f, be, lv: (be[j], 0, f)),
            pl.BlockSpec((None, D_MODEL, tf), lambda j, f, be, lv: (be[j], 0, f)),
            pl.BlockSpec((None, tf, D_MODEL), lambda j, f, be, lv: (be[j], f, 0)),
        ],
        out_specs=pl.BlockSpec(memory_space=pl.ANY),
        scratch_shapes=[
            pltpu.VMEM((rb, D_MODEL), f32),
            pltpu.VMEM((rb, D_MODEL), bf16),
            pltpu.VMEM((rb, D_MODEL), f32),
            pltpu.VMEM((rb, D_MODEL), f32),
            pltpu.SemaphoreType.DMA,
            pltpu.SemaphoreType.DMA,
        ],
    )
    return pl.pallas_call(
        functools.partial(_moe_body, n_dst=TOP_K * n),
        grid_spec=grid_spec,
        out_shape=jax.ShapeDtypeStruct((TOP_K * n, D_MODEL), f32),
        compiler_params=_cparams("arbitrary", "arbitrary"),
        name="moe",
    )(block_e, block_live, row_tok, row_dst, row_gate, h, wg, wu, wd)


def _routing_tables(top_e, top_g, n):
    rb = MOE_ROWS
    n_assign = n * TOP_K
    n_blocks = -(-n_assign // rb) + N_EXPERTS
    rows = n_blocks * rb
    e_flat = top_e.reshape(-1)
    onehot = (e_flat[:, None] == jnp.arange(N_EXPERTS, dtype=i32)[None, :]).astype(i32)
    rank = jnp.sum((jnp.cumsum(onehot, axis=0) - onehot) * onehot, axis=1)
    counts = jnp.sum(onehot, axis=0)
    padded = (counts + rb - 1) // rb * rb
    pad_end = jnp.cumsum(padded)
    pad_start = pad_end - padded
    dest = jnp.sum(onehot * pad_start[None, :], axis=1) + rank
    a = jnp.arange(n_assign, dtype=i32)
    row_tok = jnp.zeros((rows,), i32).at[dest].set(a // TOP_K)
    row_dst = jnp.full((rows,), n_assign, i32).at[dest].set((a % TOP_K) * n + a // TOP_K)
    row_gate = jnp.zeros((rows,), f32).at[dest].set(top_g.reshape(-1))
    starts = jnp.arange(n_blocks, dtype=i32) * rb
    block_e = jnp.minimum(jnp.sum((pad_end[None, :] <= starts[:, None]).astype(i32), axis=1), N_EXPERTS - 1)
    block_live = (starts < pad_end[-1]).astype(i32)
    return (block_e, block_live, row_tok.reshape(n_blocks, 1, rb), row_dst.reshape(n_blocks, 1, rb),
            row_gate.reshape(rows, 1))


def _final_body(x_ref, y0_ref, y1_ref, gt_ref, g_ref, b_ref, o_ref):
    y = y0_ref[...] + y1_ref[...]
    o_ref[...] = _ln_rows(DEEPNORM_ALPHA * x_ref[...] + (1.0 + gt_ref[...]) * y, g_ref[...], b_ref[...])


def _final_ln(x, y2, mod, ln_g, ln_b, t):
    n = x.shape[0]
    tm = 512
    nb = n // tm
    return pl.pallas_call(
        _final_body,
        grid=(nb,),
        in_specs=[_row_spec(tm), _row_spec(tm), pl.BlockSpec((tm, D_MODEL), lambda i: (i + nb, 0)),
                  _mod_spec(t, tm, 5), _vec_spec(), _vec_spec()],
        out_specs=_row_spec(tm),
        out_shape=jax.ShapeDtypeStruct((n, D_MODEL), f32),
        compiler_params=_cparams("arbitrary"),
        name="final_ln",
    )(x, y2, y2, mod, ln_g, ln_b)


def _pad_cols(w, width):
    return jnp.pad(w, ((0, 0), (0, width - w.shape[1])))


def _pad_rows(w, height):
    return jnp.pad(w, ((0, height - w.shape[0]), (0, 0)))


def _regroup_w_in(w_in, v1):
    d = w_in.shape[0]
    gdn, ml, rw = 0, 2056, 3600
    smalls = jnp.concatenate([w_in[:, gdn + 2048 : gdn + 2056], w_in[:, ml + 1536 : ml + 1544]], axis=1)
    vdown = jnp.zeros((d, LORA_V), w_in.dtype) if v1 is None else v1
    misc = jnp.concatenate([w_in[:, rw + 1536 : rw + 1792], _pad_cols(vdown, 128), _pad_cols(smalls, 128)], axis=1)
    cat = jnp.concatenate(
        [w_in[:, gdn : gdn + 2048], w_in[:, ml : ml + 1536], w_in[:, rw : rw + 1536], w_in[:, 5392:8464], misc], axis=1)
    assert cat.shape[1] == Z_WIDTH
    return cat.astype(bf16)


def _row(v, width=None):
    v = v.reshape(1, -1).astype(f32)
    return v if width is None else _pad_cols(v, width)


def kernel(x, c, ln_in_g, ln_in_b, ada_w, ada_b, w_in, gdn_conv, gdn_a_log, gdn_dt_bias, gdn_norm_g, mlstm_b_i, mlstm_b_f, mlstm_norm_g, rwkv_mu, rwkv_w0, rwkv_w2, rwkv_a0, rwkv_a2, rwkv_g2, rwkv_k_k, rwkv_k_a, rwkv_r_k, rwkv_lnx_g, rwkv_lnx_b, rwkv_v1, rwkv_mu_v1, rwkv_v0, rwkv_v2, w_branch, w_o, ln1_g, ln1_b, ln2_g, ln2_b, ffn_w_gate, ffn_w_up, ffn_w_down, moe_router, moe_router_b, moe_w_gate, moe_w_up, moe_w_down):
    b, t, d = x.shape
    n = b * t
    depth = w_in.shape[0]
    xf = x.reshape(n, d)

    c_pad = _pad_rows(c.astype(f32), 8)
    mods = []
    for layer in range(depth):
        mod = _ada_mod(c_pad, ada_w[layer].astype(bf16), _row(ada_b[layer]))[:b]
        mods.append(mod.reshape(b * 6, 1, d))

    xcur, h = _entry_ln(xf, _row(ln_in_g), _row(ln_in_b), mods[0], t)
    v_first = None
    out = None
    for layer in range(depth):
        mod = mods[layer]
        has_vres = layer > 0
        wcat = _regroup_w_in(w_in[layer], rwkv_v1[layer - 1] if has_vres else None)
        z = _matmul(h, wcat, 1024, 512, f32, "in_proj")

        gdn_par = jnp.zeros((8, LANES), f32).at[0, 0:4].set(gdn_a_log[layer]).at[1, 0:4].set(gdn_dt_bias[layer])
        y_a = _gdn(z, gdn_conv[layer].astype(f32), gdn_par, _row(gdn_norm_g[layer]), b, t)

        ml_par = jnp.zeros((8, LANES), f32).at[0, SM_ML_I : SM_ML_I + 4].set(mlstm_b_i[layer])
        ml_par = ml_par.at[0, SM_ML_F : SM_ML_F + 4].set(mlstm_b_f[layer])
        y_b = _mlstm(z, ml_par, _row(mlstm_norm_g[layer]), b, t)

        mu = rwkv_mu[layer].astype(f32)
        mu_v = rwkv_mu_v1[layer - 1].astype(f32) if has_vres else jnp.zeros((LORA_V,), f32)
        mu_misc = _row(jnp.concatenate([mu[1536:1792], mu_v]), 512)
        vecs = jnp.stack([
            rwkv_w0[layer], rwkv_a0[layer], rwkv_k_k[layer], rwkv_k_a[layer], rwkv_r_k[layer].reshape(-1),
            rwkv_lnx_g[layer], rwkv_lnx_b[layer],
            rwkv_v0[layer - 1] if has_vres else jnp.zeros((RWKV_WIDTH,), f32),
        ]).astype(f32)
        w2p = _pad_rows(rwkv_w2[layer], 128).astype(bf16)
        a2p = jnp.concatenate([jnp.zeros((LORA_W, RWKV_WIDTH), f32), rwkv_a2[layer]], axis=0).astype(bf16)
        v2p = _pad_rows(rwkv_v2[layer - 1], 128).astype(bf16) if has_vres else None
        y_c, v_first = _rwkv(z, v_first, _row(mu[:1536]), mu_misc, vecs, w2p, a2p, rwkv_g2[layer].astype(bf16), v2p, b, t)

        moe_layer = layer % 2 == 1
        xcur, h2 = _merge(y_a, y_b, y_c, z, xcur, w_branch[layer].astype(bf16), w_o[layer].astype(bf16), mod,
                          _row(ln1_g[layer]), _row(ln1_b[layer]), t, f32 if moe_layer else bf16)
        i = layer // 2
        if not moe_layer:
            mod_next = mods[layer + 1] if layer + 1 < depth else mod
            xcur, h = _ffn(h2, xcur, ffn_w_gate[i].astype(bf16), ffn_w_up[i].astype(bf16), ffn_w_down[i].astype(bf16),
                           mod, mod_next, _row(ln2_g[layer]), _row(ln2_b[layer]), t)
            out = xcur
        else:
            idx, gate = _router(h2, _pad_cols(moe_router[i].astype(f32), LANES), _row(moe_router_b[i], LANES))
            tables = _routing_tables(idx[:, :TOP_K], gate[:, :TOP_K], n)
            y2 = _moe(h2, *tables, moe_w_gate[i].astype(bf16), moe_w_up[i].astype(bf16), moe_w_down[i].astype(bf16))
            out = _final_ln(xcur, y2, mod, _row(ln2_g[layer]), _row(ln2_b[layer]), t)
            xcur = out
    return out.reshape(b, t, d)
```

```python
import functools
import math

import jax
import jax.numpy as jnp
from jax import lax
from jax.experimental import pallas as pl
from jax.experimental.pallas import tpu as pltpu

f32 = jnp.float32
bf16 = jnp.bfloat16
i32 = jnp.int32

D_MODEL = 1024
DEPTH = 2
CHUNK = 64
GDN_HEADS = 4
GDN_DK = 128
MLSTM_HEADS = 4
MLSTM_DQK = 64
MLSTM_DV = 128
MLSTM_SOFTCAP = 15.0
RWKV_HEADS = 8
RWKV_DH = 64
RWKV_WIDTH = RWKV_HEADS * RWKV_DH
LORA_W = 64
LORA_A = 64
LORA_V = 32
LORA_G = 128
N_EXPERTS = 8
TOP_K = 2
DEEPNORM_ALPHA = (2 * DEPTH) ** 0.25
LN_EPS = 1e-5
RWKV_GN_EPS = 64e-5

LANES = 128
HALF = LANES // 2

Z_GDN = 0
Z_MLSTM = 2048
Z_RWKV = 3584
Z_GATE = 5120
Z_WIDTH = 8192
MISC_WIDTH = 512
MISC_SMALLS = 384
SM_GDN_A, SM_GDN_B, SM_ML_I, SM_ML_F = 0, 4, 8, 12

MOE_ROWS = 1024
VMEM_LIMIT = 56 * 1024 * 1024


def _cparams(*sem):
    return pltpu.CompilerParams(dimension_semantics=sem, vmem_limit_bytes=VMEM_LIMIT)


def _dot(a, b):
    return jnp.dot(a.astype(bf16), b.astype(bf16), preferred_element_type=f32)


def _dot_nt(a, b):
    return lax.dot_general(a.astype(bf16), b.astype(bf16), (((1,), (1,)), ((), ())), preferred_element_type=f32)


def _dot_tn(a, b):
    return lax.dot_general(a.astype(bf16), b.astype(bf16), (((0,), (0,)), ((), ())), preferred_element_type=f32)


def _sigmoid(x):
    return 1.0 / (1.0 + jnp.exp(-x))


def _silu(x):
    return x * _sigmoid(x)


def _softplus(x):
    return jnp.maximum(x, 0.0) + jnp.log1p(jnp.exp(-jnp.abs(x)))


def _iota2(shape, axis):
    return lax.broadcasted_iota(i32, shape, axis)


def _cumsum_rows(x):
    n = x.shape[0]
    rows = _iota2(x.shape, 0)
    sh = 1
    while sh < n:
        x = x + jnp.where(rows >= sh, pltpu.roll(x, sh, 0), 0.0)
        sh *= 2
    return x


def _unit_lower_inverses(ps, nilpotent):
    n = ps[0].shape[0]
    eye = (_iota2((n, n), 0) == _iota2((n, n), 1)).astype(f32)
    ts = [eye + p for p in ps]
    for _ in range(int(math.log2(nilpotent)) - 1):
        ps = [_dot(p, p) for p in ps]
        ts = [t + _dot(t, p) for t, p in zip(ts, ps)]
    return ts


def _rows(*xs):
    return jnp.concatenate(xs, axis=0)


def _lanes(*xs):
    return jnp.concatenate(xs, axis=1)


def _half_masks(shape):
    lane = _iota2(shape, len(shape) - 1)
    return lane < HALF, lane >= HALF


def _stack2(x):
    lo, _ = _half_masks(x.shape)
    return _rows(jnp.where(lo, x, 0.0), jnp.where(lo, 0.0, x))


def _unstack2(x):
    c = x.shape[0] // 2
    return x[:c] + x[c:]


def _pair_masks(c):
    r = _iota2((2 * c, 2 * c), 0)
    q = _iota2((2 * c, 2 * c), 1)
    same = (r < c) == (q < c)
    rr = jnp.where(r < c, r, r - c)
    qq = jnp.where(q < c, q, q - c)
    return jnp.logical_and(same, rr >= qq), jnp.logical_and(same, rr > qq)


def _half_col(c, v0, v1):
    return jnp.where(_iota2((2 * c, 1), 0) < c, v0, v1)


def _ln_rows(r, g, b):
    mu = jnp.mean(r, -1, keepdims=True)
    d = r - mu
    var = jnp.mean(d * d, -1, keepdims=True)
    return d * lax.rsqrt(var + LN_EPS) * g + b


def _head64_sums(x):
    lo, _ = _half_masks((x.shape[0], LANES))
    outs = []
    for gi in range(x.shape[1] // LANES):
        xs = x[:, gi * LANES : (gi + 1) * LANES]
        s_lo = jnp.sum(jnp.where(lo, xs, 0.0), -1, keepdims=True)
        s_hi = jnp.sum(jnp.where(lo, 0.0, xs), -1, keepdims=True)
        outs.append(jnp.where(lo, s_lo, s_hi))
    return jnp.concatenate(outs, axis=1)


def _ada_body(c_ref, w_ref, b_ref, o_ref):
    o_ref[...] = _dot(_silu(c_ref[...]), w_ref[...]) + b_ref[...]


def _ada_mod(c_pad, w, b):
    n = w.shape[1]
    tn = n // 4
    return pl.pallas_call(
        _ada_body,
        grid=(n // tn,),
        in_specs=[
            pl.BlockSpec(c_pad.shape, lambda j: (0, 0)),
            pl.BlockSpec((w.shape[0], tn), lambda j: (0, j)),
            pl.BlockSpec((1, tn), lambda j: (0, j)),
        ],
        out_specs=pl.BlockSpec((c_pad.shape[0], tn), lambda j: (0, j)),
        out_shape=jax.ShapeDtypeStruct((c_pad.shape[0], n), f32),
        compiler_params=_cparams("arbitrary"),
        name="ada_mod",
    )(c_pad, w, b)


def _entry_body(x_ref, g_ref, b_ref, sc_ref, sh_ref, xn_ref, h_ref):
    xn = _ln_rows(x_ref[...], g_ref[...], b_ref[...])
    xn_ref[...] = xn
    h_ref[...] = (xn * (1.0 + sc_ref[...]) + sh_ref[...]).astype(h_ref.dtype)


def _mod_spec(rows_per_batch, tm, slot):
    return pl.BlockSpec((None, 1, D_MODEL), lambda i: ((i * tm // rows_per_batch) * 6 + slot, 0, 0))


def _row_spec(tm, width=D_MODEL):
    return pl.BlockSpec((tm, width), lambda i: (i, 0))


def _vec_spec(width=D_MODEL):
    return pl.BlockSpec((1, width), lambda i: (0, 0))


def _entry_ln(x, g, b, mod, t):
    n = x.shape[0]
    tm = 512
    return pl.pallas_call(
        _entry_body,
        grid=(n // tm,),
        in_specs=[_row_spec(tm), _vec_spec(), _vec_spec(), _mod_spec(t, tm, 1), _mod_spec(t, tm, 0)],
        out_specs=[_row_spec(tm), _row_spec(tm)],
        out_shape=[jax.ShapeDtypeStruct((n, D_MODEL), f32), jax.ShapeDtypeStruct((n, D_MODEL), bf16)],
        compiler_params=_cparams("arbitrary"),
        name="entry_ln",
    )(x, g, b, mod, mod)


def _mm_body(x_ref, w_ref, o_ref):
    o_ref[...] = jnp.dot(x_ref[...], w_ref[...], preferred_element_type=f32).astype(o_ref.dtype)


def _matmul(x, w, tm, tn, out_dtype, name):
    m, k = x.shape
    n = w.shape[1]
    return pl.pallas_call(
        _mm_body,
        grid=(m // tm, n // tn),
        in_specs=[pl.BlockSpec((tm, k), lambda i, j: (i, 0)), pl.BlockSpec((k, tn), lambda i, j: (0, j))],
        out_specs=pl.BlockSpec((tm, tn), lambda i, j: (i, j)),
        out_shape=jax.ShapeDtypeStruct((m, n), out_dtype),
        compiler_params=_cparams("arbitrary", "arbitrary"),
        name=name,
    )(x, w)


def _seq_spec(nb, c, width, col_block):
    return pl.BlockSpec((nb, c, width), lambda n: (0, n, col_block))


def _bcast_spec(shape):
    return pl.BlockSpec(shape, lambda n: (0,) * len(shape))


def _lockstep(*chains):
    pending = [(g, None) for g in chains]
    while pending:
        requests = []
        for g, value in pending:
            try:
                requests.append((g, g.send(value)))
            except StopIteration:
                pass
        pending = [(g, [thunk() for thunk in thunks]) for g, thunks in requests]


def _unit_lower_inverses_staged(ps, nilpotent):
    n = ps[0].shape[0]
    eye = (_iota2((n, n), 0) == _iota2((n, n), 1)).astype(f32)
    ts = [eye + p for p in ps]
    for _ in range(int(math.log2(nilpotent)) - 1):
        ps = yield [functools.partial(_dot, p, p) for p in ps]
        prods = yield [functools.partial(_dot, t, p) for t, p in zip(ts, ps)]
        ts = [t + d for t, d in zip(ts, prods)]
    return ts


def _gdn_body(qkv_ref, zg_ref, sm_ref, conv_ref, par_ref, ng_ref, o_ref, s_ref, buf_ref):
    nb, c, w3 = qkv_ref.shape

    @pl.when(pl.program_id(0) == 0)
    def _():
        s_ref[...] = jnp.zeros(s_ref.shape, f32)
        buf_ref[:, 0:8, :] = jnp.zeros((nb, 8, w3), f32)

    incl, strict = _pair_masks(c)
    pairs = range(GDN_HEADS // 2)
    heads = [(p, e) for p in pairs for e in range(2)]

    def col(a, lane):
        return a[:, lane : lane + 1]

    def chain(bi):
        raw = qkv_ref[bi].astype(f32)
        buf_ref[bi, 8 : 8 + c, :] = raw
        conv = conv_ref[3:4, :] * raw
        for j in range(3):
            conv = conv + conv_ref[j : j + 1, :] * buf_ref[bi, 5 + j : 5 + j + c, :]
        buf_ref[bi, 0:8, :] = buf_ref[bi, c : c + 8, :]
        x = _silu(conv)

        sm = sm_ref[bi]
        gcum = _cumsum_rows(-jnp.exp(par_ref[0:1, :]) * _softplus(sm + par_ref[1:2, :]))
        gcum_t = gcum.T
        beta_all = _sigmoid(sm)
        zg = zg_ref[bi].astype(f32)

        def head(i, h):
            return x[:, i * 512 + h * 128 : i * 512 + (h + 1) * 128]

        k_st, kb_st, q_st, v_st, g_st, eg_st, beta_st, decay, g_last = [], [], [], [], [], [], [], [], []
        for p in pairs:
            h0, h1 = 2 * p, 2 * p + 1
            qs, ks = [], []
            for h in (h0, h1):
                q = head(0, h)
                k = head(1, h)
                qs.append(q * (lax.rsqrt(jnp.sum(q * q, -1, keepdims=True) + 1e-6) * GDN_DK**-0.5))
                ks.append(k * lax.rsqrt(jnp.sum(k * k, -1, keepdims=True) + 1e-6))
            g = _rows(col(gcum, SM_GDN_A + h0), col(gcum, SM_GDN_A + h1))
            g_row = _lanes(gcum_t[SM_GDN_A + h0 : SM_GDN_A + h0 + 1, :], gcum_t[SM_GDN_A + h1 : SM_GDN_A + h1 + 1, :])
            beta = _rows(col(beta_all, SM_GDN_B + h0), col(beta_all, SM_GDN_B + h1))
            k2 = _rows(*ks)
            k_st.append(k2)
            kb_st.append(k2 * beta)
            q_st.append(_rows(*qs))
            v_st.append(_rows(head(2, h0), head(2, h1)))
            g_st.append(g)
            eg_st.append(jnp.exp(g))
            beta_st.append(beta)
            decay.append(jnp.where(incl, jnp.exp(g - g_row), 0.0))
            g_last.append((gcum[c - 1 : c, SM_GDN_A + h0 : SM_GDN_A + h0 + 1], gcum[c - 1 : c, SM_GDN_A + h1 : SM_GDN_A + h1 + 1]))

        m1 = yield [functools.partial(_dot_nt, _rows(kb_st[p], q_st[p]), k_st[p]) for p in pairs]
        t_inv = yield from _unit_lower_inverses_staged(
            [-jnp.where(strict, m1[p][: 2 * c] * decay[p], 0.0) for p in pairs], c)
        qk = [m1[p][2 * c :] * decay[p] for p in pairs]
        uw = yield [functools.partial(_dot, t_inv[p], _lanes(v_st[p] * beta_st[p], kb_st[p] * eg_st[p])) for p in pairs]
        s_old = [s_ref[bi, 2 * p + e] for p, e in heads]
        qd = [q_st[p] * eg_st[p] for p in pairs]
        ws_qs = yield [functools.partial(_dot, _rows(uw[p][e * c : (e + 1) * c, 128:], qd[p][e * c : (e + 1) * c]), s)
                       for (p, e), s in zip(heads, s_old)]
        v_new = [uw[p][e * c : (e + 1) * c, :128] - m[:c] for (p, e), m in zip(heads, ws_qs)]
        o_st = yield [functools.partial(_dot, qk[p], _rows(v_new[2 * p], v_new[2 * p + 1])) for p in pairs]
        k_dec = [k_st[p][e * c : (e + 1) * c] * jnp.exp(g_last[p][e] - g_st[p][e * c : (e + 1) * c]) for p, e in heads]
        s_add = yield [functools.partial(_dot_tn, kd, vn) for kd, vn in zip(k_dec, v_new)]
        for i, (p, e) in enumerate(heads):
            h = 2 * p + e
            s_ref[bi, h] = s_old[i] * jnp.exp(g_last[p][e]) + s_add[i]
            o = ws_qs[i][c:] + o_st[p][e * c : (e + 1) * c]
            o = o * lax.rsqrt(jnp.mean(o * o, -1, keepdims=True) + 1e-6) * ng_ref[...]
            o_ref[bi, :, h * 128 : (h + 1) * 128] = o * _silu(zg[:, h * 128 : (h + 1) * 128])

    _lockstep(*(chain(bi) for bi in range(nb)))


def _gdn(z, zm, conv_w, par, norm_g, b, t):
    c = CHUNK
    z3 = z.reshape(b, t, z.shape[1])
    zm3 = zm.reshape(b, t, zm.shape[1])
    out = pl.pallas_call(
        _gdn_body,
        grid=(t // c,),
        in_specs=[
            _seq_spec(b, c, 1536, Z_GDN // 1536),
            _seq_spec(b, c, 512, (Z_GDN + 1536) // 512),
            _seq_spec(b, c, 128, MISC_SMALLS // 128),
            _bcast_spec((4, 1536)),
            _bcast_spec((8, 128)),
            _bcast_spec((1, 128)),
        ],
        out_specs=_seq_spec(b, c, 512, 0),
        out_shape=jax.ShapeDtypeStruct((b, t, 512), f32),
        scratch_shapes=[pltpu.VMEM((b, GDN_HEADS, 128, 128), f32), pltpu.VMEM((b, c + 8, 1536), f32)],
        compiler_params=_cparams("arbitrary"),
        name="gdn",
    )(z3, z3, zm3, conv_w, par, norm_g)
    return out.reshape(b * t, 512)


def _mlstm_body(qk_ref, v_ref, og_ref, sm_ref, par_ref, ng_ref, o_ref, cs_ref, n_ref, m_ref):
    nb, c, _ = qk_ref.shape
    assert c == HALF

    @pl.when(pl.program_id(0) == 0)
    def _():
        cs_ref[...] = jnp.zeros(cs_ref.shape, f32)
        n_ref[...] = jnp.zeros(n_ref.shape, f32)
        m_ref[...] = jnp.zeros(m_ref.shape, f32)

    def cap(x):
        return MLSTM_SOFTCAP * jnp.tanh(x / MLSTM_SOFTCAP)

    incl, _ = _pair_masks(c)
    lane_lo = _iota2((1, LANES), 1) < HALF
    pairs = range(MLSTM_HEADS // 2)

    def col(a, lane):
        return a[:, lane : lane + 1]

    def row(a, lane):
        return a[lane : lane + 1, :]

    def chain(bi):
        capped = cap(sm_ref[bi] + par_ref[0:1, :])
        log_i = capped
        bcum = _cumsum_rows(-_softplus(-capped))
        log_i_t = log_i.T
        bcum_t = bcum.T
        qk = qk_ref[bi].astype(f32)
        v_all = v_ref[bi].astype(f32)
        og = og_ref[bi].astype(f32)
        q_st, k_st, v_st, bc, li, b_last, m_old, dmat, m_intra = [], [], [], [], [], [], [], [], []
        for p in pairs:
            h0, h1 = 2 * p, 2 * p + 1
            q_st.append(_stack2(qk[:, p * 128 : (p + 1) * 128]))
            k_st.append(_stack2(qk[:, 256 + p * 128 : 256 + (p + 1) * 128] * MLSTM_DQK**-0.5))
            v_st.append(_rows(v_all[:, h0 * 128 : (h0 + 1) * 128], v_all[:, h1 * 128 : (h1 + 1) * 128]))
            bc.append(_rows(col(bcum, SM_ML_F + h0), col(bcum, SM_ML_F + h1)))
            li.append(_rows(col(log_i, SM_ML_I + h0), col(log_i, SM_ML_I + h1)))
            b_row = _lanes(row(bcum_t, SM_ML_F + h0), row(bcum_t, SM_ML_F + h1))
            li_row = _lanes(row(log_i_t, SM_ML_I + h0), row(log_i_t, SM_ML_I + h1))
            b_last.append(_half_col(c, bcum[c - 1 : c, SM_ML_F + h0 : SM_ML_F + h0 + 1],
                                    bcum[c - 1 : c, SM_ML_F + h1 : SM_ML_F + h1 + 1]))
            m_old.append(_half_col(c, m_ref[bi, h0 : h0 + 1, 0:1], m_ref[bi, h1 : h1 + 1, 0:1]))
            log_d = jnp.where(incl, bc[p] - b_row + li_row, -jnp.inf)
            m_intra.append(jnp.max(log_d, -1, keepdims=True))
            dmat.append(jnp.exp(log_d - m_intra[p]))

        cs_old = [cs_ref[bi, p] for p in pairs]
        prods = yield ([functools.partial(_dot_nt, q_st[p], k_st[p]) for p in pairs]
                       + [functools.partial(_dot, q_st[p], cs_old[p]) for p in pairs])
        pm = [dmat[p] * prods[p] for p in pairs]
        q_cs = prods[len(pairs):]
        kw_st, s_old, m_new = [], [], []
        for p in pairs:
            log_e = b_last[p] - bc[p] + li[p]
            m_end = _half_col(c, jnp.max(log_e[:c], 0, keepdims=True), jnp.max(log_e[c:], 0, keepdims=True))
            m_new.append(jnp.maximum(b_last[p] + m_old[p], m_end))
            s_old.append(jnp.exp(b_last[p] + m_old[p] - m_new[p]))
            kw_st.append(k_st[p] * (jnp.exp(log_e - m_end) * jnp.exp(m_end - m_new[p])))
        prods = yield ([functools.partial(_dot, pm[p], v_st[p]) for p in pairs]
                       + [functools.partial(_dot_tn, kw_st[p], v_st[p]) for p in pairs])
        num_intra, cs_add = prods[: len(pairs)], prods[len(pairs):]
        for p in pairs:
            nvec = n_ref[bi, p]
            den_intra = jnp.sum(pm[p], -1, keepdims=True)
            m_t = jnp.maximum(bc[p] + m_old[p], m_intra[p])
            s_inter = jnp.exp(bc[p] + m_old[p] - m_t)
            s_intra = jnp.exp(m_intra[p] - m_t)
            num = s_inter * q_cs[p] + s_intra * num_intra[p]
            den = s_inter * jnp.sum(q_st[p] * nvec, -1, keepdims=True) + s_intra * den_intra
            hc = num / jnp.maximum(jnp.abs(den), jnp.exp(-m_t))
            so0, so1 = s_old[p][0:1], s_old[p][c : c + 1]
            cs_ref[bi, p] = s_old[p] * cs_old[p] + cs_add[p]
            n_ref[bi, p] = jnp.where(lane_lo, so0, so1) * nvec + jnp.sum(kw_st[p], 0, keepdims=True)
            for e in range(2):
                h = 2 * p + e
                m_ref[bi, h : h + 1, :] = jnp.broadcast_to(m_new[p][e * c : e * c + 1], (1, LANES))
                hs = hc[e * c : (e + 1) * c]
                hs = hs * lax.rsqrt(jnp.mean(hs * hs, -1, keepdims=True) + 1e-6) * ng_ref[:, h * 128 : (h + 1) * 128]
                o_ref[bi, :, h * 128 : (h + 1) * 128] = hs * _sigmoid(og[:, h * 128 : (h + 1) * 128])

    _lockstep(*(chain(bi) for bi in range(nb)))


def _mlstm(z, zm, par, norm_g, b, t):
    c = CHUNK
    z3 = z.reshape(b, t, z.shape[1])
    zm3 = zm.reshape(b, t, zm.shape[1])
    out = pl.pallas_call(
        _mlstm_body,
        grid=(t // c,),
        in_specs=[
            _seq_spec(b, c, 512, Z_MLSTM // 512),
            _seq_spec(b, c, 512, (Z_MLSTM + 512) // 512),
            _seq_spec(b, c, 512, (Z_MLSTM + 1024) // 512),
            _seq_spec(b, c, 128, MISC_SMALLS // 128),
            _bcast_spec((8, 128)),
            _bcast_spec((1, 512)),
        ],
        out_specs=_seq_spec(b, c, 512, 0),
        out_shape=jax.ShapeDtypeStruct((b, t, 512), f32),
        scratch_shapes=[
            pltpu.VMEM((b, MLSTM_HEADS // 2, 128, 128), f32),
            pltpu.VMEM((b, MLSTM_HEADS // 2, 1, 128), f32),
            pltpu.VMEM((b, 8, 128), f32),
        ],
        compiler_params=_cparams("arbitrary"),
        name="mlstm",
    )(z3, z3, z3, zm3, par, norm_g)
    return out.reshape(b * t, 512)


def _rwkv_body(*refs, has_vres):
    if has_vres:
        (r_ref, k_ref, v_ref, misc_ref, vf_ref, mu_ref, mum_ref, vecs_ref, w2_ref, a2_ref, g2_ref, v2_ref,
         y_ref, st_ref, tail_ref, tailm_ref) = refs
    else:
        (r_ref, k_ref, v_ref, misc_ref, mu_ref, mum_ref, vecs_ref, w2_ref, a2_ref, g2_ref,
         y_ref, vf_out_ref, st_ref, tail_ref, tailm_ref) = refs
    nb, c, _ = r_ref.shape

    @pl.when(pl.program_id(0) == 0)
    def _():
        st_ref[...] = jnp.zeros(st_ref.shape, f32)
        tail_ref[:, 0:8, :] = jnp.zeros((nb, 8, tail_ref.shape[2]), f32)
        tailm_ref[:, 0:8, :] = jnp.zeros((nb, 8, tailm_ref.shape[2]), f32)

    incl, strict = _pair_masks(c)
    same_head = (_iota2((LANES, LANES), 0) < HALF) == (_iota2((LANES, LANES), 1) < HALF)
    pairs = range(RWKV_HEADS // 2)
    sl = [slice(p * 128, (p + 1) * 128) for p in pairs]
    c2 = 2 * c
    w0, a0, k_k, k_a, r_k, lnx_g, lnx_b, v0 = (vecs_ref[i : i + 1, :] for i in range(8))

    def chain(bi):
        raw = jnp.concatenate([r_ref[bi], k_ref[bi], v_ref[bi]], axis=1).astype(f32)
        tail_ref[bi, 8 : 8 + c, :] = raw
        prev = tail_ref[bi, 7 : 7 + c, :]
        tail_ref[bi, 0:8, :] = tail_ref[bi, c : c + 8, :]
        rkv = raw + mu_ref[...] * (prev - raw)
        rawm = misc_ref[bi]
        tailm_ref[bi, 8 : 8 + c, :] = rawm
        prevm = tailm_ref[bi, 7 : 7 + c, :]
        tailm_ref[bi, 0:8, :] = tailm_ref[bi, c : c + 8, :]
        misc = rawm + mum_ref[...] * (prevm - rawm)

        r = rkv[:, 0:512]
        k = rkv[:, 512:1024]
        v = rkv[:, 1024:1536]
        wa = misc[:, 0:128]
        lora = [functools.partial(_dot, jnp.tanh(wa), w2_ref[...]), functools.partial(_dot, wa, a2_ref[...]),
                functools.partial(_dot, _sigmoid(misc[:, 128:256]), g2_ref[...])]
        if has_vres:
            lora.append(functools.partial(_dot, misc[:, 256:384], v2_ref[...]))
        lora = yield lora
        log_w = -jnp.exp(-_softplus(-(w0 + lora[0])) - 0.5)
        a = _sigmoid(a0 + lora[1])
        g = lora[2]
        if has_vres:
            v = v + (vf_ref[bi] - v) * _sigmoid(v0 + lora[3])
        else:
            vf_out_ref[bi] = v
        kk = k * k_k
        kk = kk * lax.rsqrt(_head64_sums(kk * kk) + 1e-6)
        k = k * (1.0 + (a - 1.0) * k_a)
        alpha = -kk
        beta = kk * a
        lw = _cumsum_rows(log_w)
        lw_last = lw[c - 1 : c, :]
        inv = jnp.exp(-lw)
        al_bar = alpha * jnp.exp(lw - log_w)
        be_hat = beta * inv
        k_hat = k * inv
        r_bar = r * jnp.exp(lw)
        to_end = jnp.exp(lw_last - lw)
        be_end = beta * to_end
        k_end = k * to_end
        g_chunk = jnp.exp(lw_last)

        al_st = [_stack2(al_bar[:, s]) for s in sl]
        v_st = [_stack2(v[:, s]) for s in sl]
        m1 = yield [functools.partial(_dot_nt, _rows(al_st[p], _stack2(r_bar[:, s])),
                                      _rows(_stack2(be_hat[:, s]), _stack2(k_hat[:, s]))) for p, s in zip(pairs, sl)]
        t_inv = yield from _unit_lower_inverses_staged([jnp.where(strict, m[:c2, :c2], 0.0) for m in m1], c)
        l_ak = [jnp.where(strict, m[:c2, c2:], 0.0) for m in m1]
        q_rb = [jnp.where(incl, m[c2:, :c2], 0.0) for m in m1]
        q_rk = [jnp.where(incl, m[c2:, c2:], 0.0) for m in m1]
        m2 = yield [functools.partial(_dot, _rows(l_ak[p], q_rk[p]), v_st[p]) for p in pairs]
        m3 = yield [functools.partial(_dot, t_inv[p], _lanes(al_st[p], m2[p][:c2])) for p in pairs]
        st_old = [st_ref[bi, p] for p in pairs]
        m4 = yield [functools.partial(_dot_nt, _rows(_unstack2(m3[p][:, :128]), r_bar[:, sl[p]]), st_old[p]) for p in pairs]
        u = [m4[p][:c] + _unstack2(m3[p][:, 128:]) for p in pairs]
        m56 = yield ([functools.partial(_dot, q_rb[p], _stack2(u[p])) for p in pairs]
                     + [functools.partial(_dot_tn, _rows(u[p], v[:, sl[p]]), _rows(be_end[:, sl[p]], k_end[:, sl[p]]))
                        for p in pairs])
        m5, m6 = m56[: len(pairs)], m56[len(pairs):]
        ys = []
        for p in pairs:
            st_ref[bi, p] = g_chunk[:, sl[p]] * st_old[p] + jnp.where(same_head, m6[p], 0.0)
            ys.append(m4[p][c:] + _unstack2(m5[p]) + _unstack2(m2[p][c2:]))
        y = jnp.concatenate(ys, axis=1)
        mean = _head64_sums(y) * (1.0 / RWKV_DH)
        d = y - mean
        var = _head64_sums(d * d) * (1.0 / RWKV_DH)
        y = d * lax.rsqrt(var + RWKV_GN_EPS) * lnx_g + lnx_b
        bonus = _head64_sums(r * k * r_k) * v
        y_ref[bi] = (y + bonus) * g

    _lockstep(*(chain(bi) for bi in range(nb)))


def _rwkv(z, zm, v_first, mu, mu_misc, vecs, w2p, a2p, g2, v2p, b, t):
    c = CHUNK
    has_vres = v_first is not None
    z3 = z.reshape(b, t, z.shape[1])
    blk = _seq_spec(b, c, 512, 0)
    in_specs = [_seq_spec(b, c, 512, (Z_RWKV + 512 * i) // 512) for i in range(3)] + [blk]
    args = [z3, z3, z3, zm.reshape(b, t, zm.shape[1])]
    if has_vres:
        in_specs.append(blk)
        args.append(v_first.reshape(b, t, 512))
    in_specs += [_bcast_spec(s) for s in ((1, 1536), (1, 512), (8, 512), (128, 512), (128, 512), (128, 512))]
    args += [mu, mu_misc, vecs, w2p, a2p, g2]
    if has_vres:
        in_specs.append(_bcast_spec((128, 512)))
        args.append(v2p)
    out_sd = jax.ShapeDtypeStruct((b, t, 512), f32)
    res = pl.pallas_call(
        functools.partial(_rwkv_body, has_vres=has_vres),
        grid=(t // c,),
        in_specs=in_specs,
        out_specs=blk if has_vres else [blk, blk],
        out_shape=out_sd if has_vres else [out_sd, out_sd],
        scratch_shapes=[
            pltpu.VMEM((b, RWKV_HEADS // 2, 128, 128), f32),
            pltpu.VMEM((b, c + 8, 1536), f32),
            pltpu.VMEM((b, c + 8, 512), f32),
        ],
        compiler_params=_cparams("arbitrary"),
        name="rwkv_res" if has_vres else "rwkv",
    )(*args)
    if has_vres:
        return res.reshape(b * t, 512), v_first
    return res[0].reshape(b * t, 512), res[1].reshape(b * t, 512)


def _merge_body(ya_ref, yb_ref, yc_ref, ga_ref, gb_ref, gc_ref, x_ref, wb_ref, wo_ref, gt_ref, g_ref, b_ref,
                sc_ref, sh_ref, xn_ref, h_ref):
    merged = _sigmoid(ga_ref[...].astype(f32)) * _dot(ya_ref[...], wb_ref[0])
    merged = merged + _sigmoid(gb_ref[...].astype(f32)) * _dot(yb_ref[...], wb_ref[1])
    merged = merged + _sigmoid(gc_ref[...].astype(f32)) * _dot(yc_ref[...], wb_ref[2])
    y = _dot(merged, wo_ref[...])
    xn = _ln_rows(DEEPNORM_ALPHA * x_ref[...] + (1.0 + gt_ref[...]) * y, g_ref[...], b_ref[...])
    xn_ref[...] = xn
    h_ref[...] = (xn * (1.0 + sc_ref[...]) + sh_ref[...]).astype(h_ref.dtype)


def _merge(ya, yb, yc, z, x, wb, wo, mod, ln_g, ln_b, t, h_dtype):
    n = x.shape[0]
    tm = 512
    gate = lambda i: pl.BlockSpec((tm, 1024), lambda r: (r, Z_GATE // 1024 + i))
    return pl.pallas_call(
        _merge_body,
        grid=(n // tm,),
        in_specs=[
            _row_spec(tm, 512), _row_spec(tm, 512), _row_spec(tm, 512), gate(0), gate(1), gate(2), _row_spec(tm),
            pl.BlockSpec((3, 512, D_MODEL), lambda r: (0, 0, 0)),
            pl.BlockSpec((D_MODEL, D_MODEL), lambda r: (0, 0)),
            _mod_spec(t, tm, 2), _vec_spec(), _vec_spec(), _mod_spec(t, tm, 4), _mod_spec(t, tm, 3),
        ],
        out_specs=[_row_spec(tm), _row_spec(tm)],
        out_shape=[jax.ShapeDtypeStruct((n, D_MODEL), f32), jax.ShapeDtypeStruct((n, D_MODEL), h_dtype)],
        compiler_params=_cparams("arbitrary"),
        name="merge",
    )(ya, yb, yc, z, z, z, x, wb, wo, mod, ln_g, ln_b, mod, mod)


def _ffn_body(h_ref, x_ref, wg_ref, wu_ref, wd_ref, gt_ref, g_ref, b_ref, sc_ref, sh_ref, xn_ref, hn_ref, acc_ref):
    f = pl.program_id(1)

    @pl.when(f == 0)
    def _():
        acc_ref[...] = jnp.zeros(acc_ref.shape, f32)

    h = h_ref[...]
    act = _silu(_dot(h, wg_ref[...])) * _dot(h, wu_ref[...])
    acc_ref[...] += _dot(act, wd_ref[...])

    @pl.when(f == pl.num_programs(1) - 1)
    def _():
        xn = _ln_rows(DEEPNORM_ALPHA * x_ref[...] + (1.0 + gt_ref[...]) * acc_ref[...], g_ref[...], b_ref[...])
        xn_ref[...] = xn
        hn_ref[...] = (xn * (1.0 + sc_ref[...]) + sh_ref[...]).astype(hn_ref.dtype)


def _ffn(h, x, wg, wu, wd, mod, mod_next, ln_g, ln_b, t):
    n = x.shape[0]
    dff = wg.shape[1]
    tm, tf = 512, dff // 2
    row = pl.BlockSpec((tm, D_MODEL), lambda i, f: (i, 0))
    vec = pl.BlockSpec((1, D_MODEL), lambda i, f: (0, 0))
    modspec = lambda slot: pl.BlockSpec((None, 1, D_MODEL), lambda i, f: ((i * tm // t) * 6 + slot, 0, 0))
    return pl.pallas_call(
        _ffn_body,
        grid=(n // tm, dff // tf),
        in_specs=[
            row, row,
            pl.BlockSpec((D_MODEL, tf), lambda i, f: (0, f)),
            pl.BlockSpec((D_MODEL, tf), lambda i, f: (0, f)),
            pl.BlockSpec((tf, D_MODEL), lambda i, f: (f, 0)),
            modspec(5), vec, vec, modspec(1), modspec(0),
        ],
        out_specs=[row, row],
        out_shape=[jax.ShapeDtypeStruct((n, D_MODEL), f32), jax.ShapeDtypeStruct((n, D_MODEL), bf16)],
        scratch_shapes=[pltpu.VMEM((tm, D_MODEL), f32)],
        compiler_params=_cparams("arbitrary", "arbitrary"),
        name="ffn",
    )(h, x, wg, wu, wd, mod, ln_g, ln_b, mod_next, mod_next)


def _router_body(h_ref, w_ref, b_ref, idx_ref, gate_ref):
    logits = jnp.dot(h_ref[...], w_ref[...], precision=lax.Precision.HIGHEST, preferred_element_type=f32) + b_ref[...]
    lane = _iota2(logits.shape, 1)
    logits = jnp.where(lane < N_EXPERTS, logits, -jnp.inf)
    m1 = jnp.max(logits, -1, keepdims=True)
    i1 = jnp.min(jnp.where(logits == m1, lane, LANES), -1, keepdims=True)
    rest = jnp.where(lane == i1, -jnp.inf, logits)
    m2 = jnp.max(rest, -1, keepdims=True)
    i2 = jnp.min(jnp.where(rest == m2, lane, LANES), -1, keepdims=True)
    e = jnp.exp(m2 - m1)
    g1 = 1.0 / (1.0 + e)
    g2 = e / (1.0 + e)
    idx_ref[...] = jnp.where(lane == 0, i1, jnp.where(lane == 1, i2, 0))
    gate_ref[...] = jnp.where(lane == 0, g1, jnp.where(lane == 1, g2, 0.0))


def _router(h, w_pad, b_pad):
    n = h.shape[0]
    tm = 512
    return pl.pallas_call(
        _router_body,
        grid=(n // tm,),
        in_specs=[_row_spec(tm), pl.BlockSpec((D_MODEL, LANES), lambda i: (0, 0)), _vec_spec(LANES)],
        out_specs=[_row_spec(tm, LANES), _row_spec(tm, LANES)],
        out_shape=[jax.ShapeDtypeStruct((n, LANES), i32), jax.ShapeDtypeStruct((n, LANES), f32)],
        compiler_params=_cparams("arbitrary"),
        name="router",
    )(h, w_pad, b_pad)


def _moe_body(be_ref, live_ref, tok_ref, tokn_ref, dst_ref, gate_ref, h_hbm, wg_ref, wu_ref, wd_ref, y_hbm,
              xbuf, xb16, acc_ref, obuf, gsem, ssem):
    j = pl.program_id(0)
    f = pl.program_id(1)
    nb = pl.num_programs(0)
    nf = pl.num_programs(1)
    rows = xbuf.shape[1]
    slot = j % 2
    live = live_ref[j] > 0
    next_live = live_ref[jnp.minimum(j + 1, nb - 1)] > 0

    def gather(ref, dst_slot):
        def start(r, carry):
            pltpu.make_async_copy(h_hbm.at[pl.ds(ref[0, 0, r], 1), :], xbuf.at[dst_slot, pl.ds(r, 1), :],
                                  gsem.at[dst_slot]).start()
            return carry

        lax.fori_loop(0, rows, start, 0, unroll=8)

    def gather_wait(src_slot):
        pltpu.make_async_copy(h_hbm.at[pl.ds(0, rows), :], xbuf.at[src_slot], gsem.at[src_slot]).wait()

    def scatter_wait(src_slot):
        pltpu.make_async_copy(obuf.at[src_slot], y_hbm.at[pl.ds(0, rows), :], ssem.at[src_slot]).wait()

    @pl.when(jnp.logical_and(f == 0, j == 0))
    def _():
        obuf[1] = jnp.zeros(obuf.shape[1:], f32)
        spare = pltpu.make_async_copy(obuf.at[1], y_hbm.at[pl.ds(y_hbm.shape[0] - rows, rows), :], ssem.at[1])
        spare.start()
        spare.wait()

    @pl.when(jnp.logical_and(f == 0, jnp.logical_and(j == 0, live)))
    def _():
        gather(tok_ref, 0)

    @pl.when(jnp.logical_and(f == 0, live))
    def _():
        gather_wait(slot)
        xb16[...] = xbuf[slot].astype(bf16)
        acc_ref[...] = jnp.zeros(acc_ref.shape, f32)

    @pl.when(jnp.logical_and(f == 0, jnp.logical_and(j + 1 < nb, next_live)))
    def _():
        gather(tokn_ref, 1 - slot)

    @pl.when(live)
    def _():
        x = xb16[...]
        act = _silu(_dot(x, wg_ref[...])) * _dot(x, wu_ref[...])
        acc_ref[...] += _dot(act, wd_ref[...])

    @pl.when(jnp.logical_and(f == nf - 1, live))
    def _():
        obuf[slot] = acc_ref[...] * gate_ref[...]

        def start(r, carry):
            pltpu.make_async_copy(obuf.at[slot, pl.ds(r, 1), :], y_hbm.at[pl.ds(dst_ref[0, 0, r], 1), :],
                                  ssem.at[slot]).start()
            return carry

        lax.fori_loop(0, rows, start, 0, unroll=8)

    @pl.when(jnp.logical_and(f == nf - 1, jnp.logical_and(j > 0, live_ref[jnp.maximum(j - 1, 0)] > 0)))
    def _():
        scatter_wait(1 - slot)

    @pl.when(jnp.logical_and(f == nf - 1, jnp.logical_and(j == nb - 1, live)))
    def _():
        scatter_wait(slot)


def _moe(h, block_e, block_live, row_tok, row_dst, row_gate, wg, wu, wd):
    n = h.shape[0]
    n_blocks = block_e.shape[0]
    rb = MOE_ROWS
    dff = wg.shape[2]
    tf = 512
    smem_blk = pl.BlockSpec((1, 1, rb), lambda j, f, be, lv: (j, 0, 0), memory_space=pltpu.SMEM)
    smem_next = pl.BlockSpec((1, 1, rb), lambda j, f, be, lv: (jnp.minimum(j + 1, n_blocks - 1), 0, 0),
                             memory_space=pltpu.SMEM)
    grid_spec = pltpu.PrefetchScalarGridSpec(
        num_scalar_prefetch=2,
        grid=(n_blocks, dff // tf),
        in_specs=[
            smem_blk,
            smem_next,
            smem_blk,
            pl.BlockSpec((rb, 1), lambda j, f, be, lv: (j, 0)),
            pl.BlockSpec(memory_space=pl.ANY),
            pl.BlockSpec((None, D_MODEL, tf), lambda j, f, be, lv: (be[j], 0, f)),
            pl.BlockSpec((None, D_MODEL, tf), lambda j, f, be, lv: (be[j], 0, f)),
            pl.BlockSpec((None, tf, D_MODEL), lambda j, f, be, lv: (be[j], f, 0)),
        ],
        out_specs=pl.BlockSpec(memory_space=pl.ANY),
        scratch_shapes=[
            pltpu.VMEM((2, rb, D_MODEL), f32),
            pltpu.VMEM((rb, D_MODEL), bf16),
            pltpu.VMEM((rb, D_MODEL), f32),
            pltpu.VMEM((2, rb, D_MODEL), f32),
            pltpu.SemaphoreType.DMA((2,)),
            pltpu.SemaphoreType.DMA((2,)),
        ],
    )
    return pl.pallas_call(
        _moe_body,
        grid_spec=grid_spec,
        out_shape=jax.ShapeDtypeStruct((TOP_K * n + rb, D_MODEL), f32),
        compiler_params=_cparams("arbitrary", "arbitrary"),
        name="moe",
    )(block_e, block_live, row_tok, row_tok, row_dst, row_gate, h, wg, wu, wd)


def _routing_tables(top_e, top_g, n):
    rb = MOE_ROWS
    n_assign = n * TOP_K
    n_blocks = -(-n_assign // rb) + N_EXPERTS
    rows = n_blocks * rb
    e_flat = top_e.reshape(-1)
    onehot = (e_flat[:, None] == jnp.arange(N_EXPERTS, dtype=i32)[None, :]).astype(i32)
    rank = jnp.sum((jnp.cumsum(onehot, axis=0) - onehot) * onehot, axis=1)
    counts = jnp.sum(onehot, axis=0)
    padded = (counts + rb - 1) // rb * rb
    pad_end = jnp.cumsum(padded)
    pad_start = pad_end - padded
    dest = jnp.sum(onehot * pad_start[None, :], axis=1) + rank
    row_a = jnp.full((rows,), n_assign, i32).at[dest].set(jnp.arange(n_assign, dtype=i32))
    valid = row_a < n_assign
    tok, slot_k = row_a // TOP_K, row_a % TOP_K
    row_tok = jnp.where(valid, tok, 0)
    row_dst = jnp.where(valid, slot_k * n + tok, n_assign + jnp.arange(rows, dtype=i32) % rb)
    row_gate = jnp.where(valid, top_g.reshape(-1)[jnp.minimum(row_a, n_assign - 1)], 0.0)
    starts = jnp.arange(n_blocks, dtype=i32) * rb
    block_e = jnp.minimum(jnp.sum((pad_end[None, :] <= starts[:, None]).astype(i32), axis=1), N_EXPERTS - 1)
    block_live = (starts < pad_end[-1]).astype(i32)
    return (block_e, block_live, row_tok.reshape(n_blocks, 1, rb), row_dst.reshape(n_blocks, 1, rb),
            row_gate.reshape(rows, 1))


def _final_body(x_ref, y0_ref, y1_ref, gt_ref, g_ref, b_ref, o_ref):
    y = y0_ref[...] + y1_ref[...]
    o_ref[...] = _ln_rows(DEEPNORM_ALPHA * x_ref[...] + (1.0 + gt_ref[...]) * y, g_ref[...], b_ref[...])


def _final_ln(x, y2, mod, ln_g, ln_b, t):
    n = x.shape[0]
    tm = 512
    nb = n // tm
    return pl.pallas_call(
        _final_body,
        grid=(nb,),
        in_specs=[_row_spec(tm), _row_spec(tm), pl.BlockSpec((tm, D_MODEL), lambda i: (i + nb, 0)),
                  _mod_spec(t, tm, 5), _vec_spec(), _vec_spec()],
        out_specs=_row_spec(tm),
        out_shape=jax.ShapeDtypeStruct((n, D_MODEL), f32),
        compiler_params=_cparams("arbitrary"),
        name="final_ln",
    )(x, y2, y2, mod, ln_g, ln_b)


def _pad_cols(w, width):
    return jnp.pad(w, ((0, 0), (0, width - w.shape[1])))


def _pad_rows(w, height):
    return jnp.pad(w, ((0, height - w.shape[0]), (0, 0)))


def _regroup_w_in(w_in, v1):
    d = w_in.shape[0]
    gdn, ml, rw = 0, 2056, 3600
    smalls = jnp.concatenate([w_in[:, gdn + 2048 : gdn + 2056], w_in[:, ml + 1536 : ml + 1544]], axis=1)
    vdown = jnp.zeros((d, LORA_V), w_in.dtype) if v1 is None else v1
    misc = jnp.concatenate([w_in[:, rw + 1536 : rw + 1792], _pad_cols(vdown, 128), _pad_cols(smalls, 128)], axis=1)
    cat = jnp.concatenate(
        [w_in[:, gdn : gdn + 2048], w_in[:, ml : ml + 1536], w_in[:, rw : rw + 1536], w_in[:, 5392:8464]], axis=1)
    assert cat.shape[1] == Z_WIDTH and misc.shape[1] == MISC_WIDTH
    return cat.astype(bf16), misc.astype(bf16)


def _row(v, width=None):
    v = v.reshape(1, -1).astype(f32)
    return v if width is None else _pad_cols(v, width)


def kernel(x, c, ln_in_g, ln_in_b, ada_w, ada_b, w_in, gdn_conv, gdn_a_log, gdn_dt_bias, gdn_norm_g, mlstm_b_i, mlstm_b_f, mlstm_norm_g, rwkv_mu, rwkv_w0, rwkv_w2, rwkv_a0, rwkv_a2, rwkv_g2, rwkv_k_k, rwkv_k_a, rwkv_r_k, rwkv_lnx_g, rwkv_lnx_b, rwkv_v1, rwkv_mu_v1, rwkv_v0, rwkv_v2, w_branch, w_o, ln1_g, ln1_b, ln2_g, ln2_b, ffn_w_gate, ffn_w_up, ffn_w_down, moe_router, moe_router_b, moe_w_gate, moe_w_up, moe_w_down):
    b, t, d = x.shape
    n = b * t
    depth = w_in.shape[0]
    xf = x.reshape(n, d)

    c_pad = _pad_rows(c.astype(f32), 8)
    mods = []
    for layer in range(depth):
        mod = _ada_mod(c_pad, ada_w[layer].astype(bf16), _row(ada_b[layer]))[:b]
        mods.append(mod.reshape(b * 6, 1, d))

    xcur, h = _entry_ln(xf, _row(ln_in_g), _row(ln_in_b), mods[0], t)
    v_first = None
    out = None
    for layer in range(depth):
        mod = mods[layer]
        has_vres = layer > 0
        w_main, w_misc = _regroup_w_in(w_in[layer], rwkv_v1[layer - 1] if has_vres else None)
        z = _matmul(h, w_main, 1024, 1024, bf16, "in_proj")
        zm = _matmul(h, w_misc, 1024, MISC_WIDTH, f32, "in_proj_misc")

        gdn_par = jnp.zeros((8, LANES), f32).at[0, 0:4].set(gdn_a_log[layer]).at[1, 0:4].set(gdn_dt_bias[layer])
        y_a = _gdn(z, zm, gdn_conv[layer].astype(f32), gdn_par, _row(gdn_norm_g[layer]), b, t)

        ml_par = jnp.zeros((8, LANES), f32).at[0, SM_ML_I : SM_ML_I + 4].set(mlstm_b_i[layer])
        ml_par = ml_par.at[0, SM_ML_F : SM_ML_F + 4].set(mlstm_b_f[layer])
        y_b = _mlstm(z, zm, ml_par, _row(mlstm_norm_g[layer]), b, t)

        mu = rwkv_mu[layer].astype(f32)
        mu_v = rwkv_mu_v1[layer - 1].astype(f32) if has_vres else jnp.zeros((LORA_V,), f32)
        mu_misc = _row(jnp.concatenate([mu[1536:1792], mu_v]), 512)
        vecs = jnp.stack([
            rwkv_w0[layer], rwkv_a0[layer], rwkv_k_k[layer], rwkv_k_a[layer], rwkv_r_k[layer].reshape(-1),
            rwkv_lnx_g[layer], rwkv_lnx_b[layer],
            rwkv_v0[layer - 1] if has_vres else jnp.zeros((RWKV_WIDTH,), f32),
        ]).astype(f32)
        w2p = _pad_rows(rwkv_w2[layer], 128).astype(bf16)
        a2p = jnp.concatenate([jnp.zeros((LORA_W, RWKV_WIDTH), f32), rwkv_a2[layer]], axis=0).astype(bf16)
        v2p = _pad_rows(rwkv_v2[layer - 1], 128).astype(bf16) if has_vres else None
        y_c, v_first = _rwkv(z, zm, v_first, _row(mu[:1536]), mu_misc, vecs, w2p, a2p, rwkv_g2[layer].astype(bf16), v2p, b, t)

        moe_layer = layer % 2 == 1
        xcur, h2 = _merge(y_a, y_b, y_c, z, xcur, w_branch[layer].astype(bf16), w_o[layer].astype(bf16), mod,
                          _row(ln1_g[layer]), _row(ln1_b[layer]), t, f32 if moe_layer else bf16)
        i = layer // 2
        if not moe_layer:
            mod_next = mods[layer + 1] if layer + 1 < depth else mod
            xcur, h = _ffn(h2, xcur, ffn_w_gate[i].astype(bf16), ffn_w_up[i].astype(bf16), ffn_w_down[i].astype(bf16),
                           mod, mod_next, _row(ln2_g[layer]), _row(ln2_b[layer]), t)
            out = xcur
        else:
            idx, gate = _router(h2, _pad_cols(moe_router[i].astype(f32), LANES), _row(moe_router_b[i], LANES))
            tables = _routing_tables(idx[:, :TOP_K], gate[:, :TOP_K], n)
            y2 = _moe(h2, *tables, moe_w_gate[i], moe_w_up[i], moe_w_down[i])
            out = _final_ln(xcur, y2, mod, _row(ln2_g[layer]), _row(ln2_b[layer]), t)
            xcur = out
    return out.reshape(b, t, d)
```

```python
import functools
import math

import jax
import jax.numpy as jnp
from jax import lax
from jax.experimental import pallas as pl
from jax.experimental.pallas import tpu as pltpu

f32 = jnp.float32
bf16 = jnp.bfloat16
i32 = jnp.int32

D_MODEL = 1024
DEPTH = 2
CHUNK = 64
GDN_HEADS = 4
GDN_DK = 128
MLSTM_HEADS = 4
MLSTM_DQK = 64
MLSTM_DV = 128
MLSTM_SOFTCAP = 15.0
RWKV_HEADS = 8
RWKV_DH = 64
RWKV_WIDTH = RWKV_HEADS * RWKV_DH
LORA_W = 64
LORA_A = 64
LORA_V = 32
LORA_G = 128
N_EXPERTS = 8
TOP_K = 2
DEEPNORM_ALPHA = (2 * DEPTH) ** 0.25
LN_EPS = 1e-5
RWKV_GN_EPS = 64e-5

LANES = 128
HALF = LANES // 2

Z_GDN = 0
Z_MLSTM = 2048
Z_RWKV = 3584
Z_GATE = 5120
Z_WIDTH = 8192
MISC_WIDTH = 512
MISC_SMALLS = 384
SM_GDN_A, SM_GDN_B, SM_ML_I, SM_ML_F = 0, 4, 8, 12

STEP_CHUNKS = 4
MOE_ROWS = 1024
VMEM_LIMIT = 56 * 1024 * 1024


def _cparams(*sem):
    return pltpu.CompilerParams(dimension_semantics=sem, vmem_limit_bytes=VMEM_LIMIT)


def _dot(a, b):
    return jnp.dot(a.astype(bf16), b.astype(bf16), preferred_element_type=f32)


def _dot_nt(a, b):
    return lax.dot_general(a.astype(bf16), b.astype(bf16), (((1,), (1,)), ((), ())), preferred_element_type=f32)


def _dot_tn(a, b):
    return lax.dot_general(a.astype(bf16), b.astype(bf16), (((0,), (0,)), ((), ())), preferred_element_type=f32)


def _sigmoid(x):
    return 1.0 / (1.0 + jnp.exp(-x))


def _silu(x):
    return x * _sigmoid(x)


def _softplus(x):
    return jnp.maximum(x, 0.0) + jnp.log1p(jnp.exp(-jnp.abs(x)))


def _iota2(shape, axis):
    return lax.broadcasted_iota(i32, shape, axis)


def _cumsum_rows(x):
    n = x.shape[0]
    rows = _iota2(x.shape, 0)
    sh = 1
    while sh < n:
        x = x + jnp.where(rows >= sh, pltpu.roll(x, sh, 0), 0.0)
        sh *= 2
    return x


def _unit_lower_inverses(ps, nilpotent):
    n = ps[0].shape[0]
    eye = (_iota2((n, n), 0) == _iota2((n, n), 1)).astype(f32)
    ts = [eye + p for p in ps]
    for _ in range(int(math.log2(nilpotent)) - 1):
        ps = [_dot(p, p) for p in ps]
        ts = [t + _dot(t, p) for t, p in zip(ts, ps)]
    return ts


def _rows(*xs):
    return jnp.concatenate(xs, axis=0)


def _lanes(*xs):
    return jnp.concatenate(xs, axis=1)


def _half_masks(shape):
    lane = _iota2(shape, len(shape) - 1)
    return lane < HALF, lane >= HALF


def _stack2(x):
    lo, _ = _half_masks(x.shape)
    return _rows(jnp.where(lo, x, 0.0), jnp.where(lo, 0.0, x))


def _unstack2(x):
    c = x.shape[0] // 2
    return x[:c] + x[c:]


def _pair_masks(c):
    r = _iota2((2 * c, 2 * c), 0)
    q = _iota2((2 * c, 2 * c), 1)
    same = (r < c) == (q < c)
    rr = jnp.where(r < c, r, r - c)
    qq = jnp.where(q < c, q, q - c)
    return jnp.logical_and(same, rr >= qq), jnp.logical_and(same, rr > qq)


def _half_col(c, v0, v1):
    return jnp.where(_iota2((2 * c, 1), 0) < c, v0, v1)


def _ln_rows(r, g, b):
    mu = jnp.mean(r, -1, keepdims=True)
    d = r - mu
    var = jnp.mean(d * d, -1, keepdims=True)
    return d * lax.rsqrt(var + LN_EPS) * g + b


def _head64_sums(x):
    lo, _ = _half_masks((x.shape[0], LANES))
    outs = []
    for gi in range(x.shape[1] // LANES):
        xs = x[:, gi * LANES : (gi + 1) * LANES]
        s_lo = jnp.sum(jnp.where(lo, xs, 0.0), -1, keepdims=True)
        s_hi = jnp.sum(jnp.where(lo, 0.0, xs), -1, keepdims=True)
        outs.append(jnp.where(lo, s_lo, s_hi))
    return jnp.concatenate(outs, axis=1)


def _ada_body(c_ref, w_ref, b_ref, o_ref):
    o_ref[...] = _dot(_silu(c_ref[...]), w_ref[...]) + b_ref[...]


def _ada_mod(c_pad, w, b):
    n = w.shape[1]
    tn = n // 4
    return pl.pallas_call(
        _ada_body,
        grid=(n // tn,),
        in_specs=[
            pl.BlockSpec(c_pad.shape, lambda j: (0, 0)),
            pl.BlockSpec((w.shape[0], tn), lambda j: (0, j)),
            pl.BlockSpec((1, tn), lambda j: (0, j)),
        ],
        out_specs=pl.BlockSpec((c_pad.shape[0], tn), lambda j: (0, j)),
        out_shape=jax.ShapeDtypeStruct((c_pad.shape[0], n), f32),
        compiler_params=_cparams("arbitrary"),
        name="ada_mod",
    )(c_pad, w, b)


def _entry_body(x_ref, g_ref, b_ref, sc_ref, sh_ref, xn_ref, h_ref):
    xn = _ln_rows(x_ref[...], g_ref[...], b_ref[...])
    xn_ref[...] = xn
    h_ref[...] = (xn * (1.0 + sc_ref[...]) + sh_ref[...]).astype(h_ref.dtype)


def _mod_spec(rows_per_batch, tm, slot):
    return pl.BlockSpec((None, 1, D_MODEL), lambda i: ((i * tm // rows_per_batch) * 6 + slot, 0, 0))


def _row_spec(tm, width=D_MODEL):
    return pl.BlockSpec((tm, width), lambda i: (i, 0))


def _vec_spec(width=D_MODEL):
    return pl.BlockSpec((1, width), lambda i: (0, 0))


def _entry_ln(x, g, b, mod, t):
    n = x.shape[0]
    tm = 512
    return pl.pallas_call(
        _entry_body,
        grid=(n // tm,),
        in_specs=[_row_spec(tm), _vec_spec(), _vec_spec(), _mod_spec(t, tm, 1), _mod_spec(t, tm, 0)],
        out_specs=[_row_spec(tm), _row_spec(tm)],
        out_shape=[jax.ShapeDtypeStruct((n, D_MODEL), f32), jax.ShapeDtypeStruct((n, D_MODEL), bf16)],
        compiler_params=_cparams("arbitrary"),
        name="entry_ln",
    )(x, g, b, mod, mod)


def _mm_body(x_ref, w_ref, o_ref):
    o_ref[...] = jnp.dot(x_ref[...], w_ref[...], preferred_element_type=f32).astype(o_ref.dtype)


def _matmul(x, w, tm, tn, out_dtype, name):
    m, k = x.shape
    n = w.shape[1]
    return pl.pallas_call(
        _mm_body,
        grid=(m // tm, n // tn),
        in_specs=[pl.BlockSpec((tm, k), lambda i, j: (i, 0)), pl.BlockSpec((k, tn), lambda i, j: (0, j))],
        out_specs=pl.BlockSpec((tm, tn), lambda i, j: (i, j)),
        out_shape=jax.ShapeDtypeStruct((m, n), out_dtype),
        compiler_params=_cparams("arbitrary", "arbitrary"),
        name=name,
    )(x, w)


def _seq_spec(nb, c, width, col_block):
    return pl.BlockSpec((nb, c, width), lambda n: (0, n, col_block))


def _bcast_spec(shape):
    return pl.BlockSpec(shape, lambda n: (0,) * len(shape))


def _lockstep(*chains):
    pending = [(g, None) for g in chains]
    while pending:
        requests = []
        for g, value in pending:
            try:
                requests.append((g, g.send(value)))
            except StopIteration:
                pass
        pending = [(g, [thunk() for thunk in thunks]) for g, thunks in requests]


def _unit_lower_inverses_staged(ps, nilpotent):
    n = ps[0].shape[0]
    eye = (_iota2((n, n), 0) == _iota2((n, n), 1)).astype(f32)
    ts = [eye + p for p in ps]
    for _ in range(int(math.log2(nilpotent)) - 1):
        ps = yield [functools.partial(_dot, p, p) for p in ps]
        prods = yield [functools.partial(_dot, t, p) for t, p in zip(ts, ps)]
        ts = [t + d for t, d in zip(ts, prods)]
    return ts


def _gdn_body(qkv_ref, zg_ref, sm_ref, conv_ref, par_ref, ng_ref, o_ref, s_ref, buf_ref):
    nb, step_rows, w3 = qkv_ref.shape
    c = CHUNK

    @pl.when(pl.program_id(0) == 0)
    def _():
        s_ref[...] = jnp.zeros(s_ref.shape, f32)
        buf_ref[:, 0:8, :] = jnp.zeros((nb, 8, w3), f32)

    incl, strict = _pair_masks(c)
    pairs = range(GDN_HEADS // 2)
    heads = [(p, e) for p in pairs for e in range(2)]

    def col(a, lane):
        return a[:, lane : lane + 1]

    def chain(bi, s):
        rs = slice(s * c, (s + 1) * c)
        raw = qkv_ref[bi, rs, :].astype(f32)
        buf_ref[bi, 8 : 8 + c, :] = raw
        conv = conv_ref[3:4, :] * raw
        for j in range(3):
            conv = conv + conv_ref[j : j + 1, :] * buf_ref[bi, 5 + j : 5 + j + c, :]
        buf_ref[bi, 0:8, :] = buf_ref[bi, c : c + 8, :]
        x = _silu(conv)

        sm = sm_ref[bi, rs, :]
        gcum = _cumsum_rows(-jnp.exp(par_ref[0:1, :]) * _softplus(sm + par_ref[1:2, :]))
        gcum_t = gcum.T
        beta_all = _sigmoid(sm)
        zg = zg_ref[bi, rs, :].astype(f32)

        def head(i, h):
            return x[:, i * 512 + h * 128 : i * 512 + (h + 1) * 128]

        k_st, kb_st, q_st, v_st, g_st, eg_st, beta_st, decay, g_last = [], [], [], [], [], [], [], [], []
        for p in pairs:
            h0, h1 = 2 * p, 2 * p + 1
            qs, ks = [], []
            for h in (h0, h1):
                q = head(0, h)
                k = head(1, h)
                qs.append(q * (lax.rsqrt(jnp.sum(q * q, -1, keepdims=True) + 1e-6) * GDN_DK**-0.5))
                ks.append(k * lax.rsqrt(jnp.sum(k * k, -1, keepdims=True) + 1e-6))
            g = _rows(col(gcum, SM_GDN_A + h0), col(gcum, SM_GDN_A + h1))
            g_row = _lanes(gcum_t[SM_GDN_A + h0 : SM_GDN_A + h0 + 1, :], gcum_t[SM_GDN_A + h1 : SM_GDN_A + h1 + 1, :])
            beta = _rows(col(beta_all, SM_GDN_B + h0), col(beta_all, SM_GDN_B + h1))
            k2 = _rows(*ks)
            k_st.append(k2)
            kb_st.append(k2 * beta)
            q_st.append(_rows(*qs))
            v_st.append(_rows(head(2, h0), head(2, h1)))
            g_st.append(g)
            eg_st.append(jnp.exp(g))
            beta_st.append(beta)
            decay.append(jnp.where(incl, jnp.exp(g - g_row), 0.0))
            g_last.append((gcum[c - 1 : c, SM_GDN_A + h0 : SM_GDN_A + h0 + 1], gcum[c - 1 : c, SM_GDN_A + h1 : SM_GDN_A + h1 + 1]))

        m1 = yield [functools.partial(_dot_nt, _rows(kb_st[p], q_st[p]), k_st[p]) for p in pairs]
        t_inv = yield from _unit_lower_inverses_staged(
            [-jnp.where(strict, m1[p][: 2 * c] * decay[p], 0.0) for p in pairs], c)
        qk = [m1[p][2 * c :] * decay[p] for p in pairs]
        uw = yield [functools.partial(_dot, t_inv[p], _lanes(v_st[p] * beta_st[p], kb_st[p] * eg_st[p])) for p in pairs]
        qd = [q_st[p] * eg_st[p] for p in pairs]
        k_dec = [k_st[p][e * c : (e + 1) * c] * jnp.exp(g_last[p][e] - g_st[p][e * c : (e + 1) * c]) for p, e in heads]
        for _ in range(2 * s):
            yield []
        s_old = [s_ref[bi, 2 * p + e] for p, e in heads]
        ws_qs = yield [functools.partial(_dot, _rows(uw[p][e * c : (e + 1) * c, 128:], qd[p][e * c : (e + 1) * c]), st)
                       for (p, e), st in zip(heads, s_old)]
        v_new = [uw[p][e * c : (e + 1) * c, :128] - m[:c] for (p, e), m in zip(heads, ws_qs)]
        prods = yield ([functools.partial(_dot, qk[p], _rows(v_new[2 * p], v_new[2 * p + 1])) for p in pairs]
                       + [functools.partial(_dot_tn, kd, vn) for kd, vn in zip(k_dec, v_new)])
        o_st, s_add = prods[: len(pairs)], prods[len(pairs):]
        for i, (p, e) in enumerate(heads):
            h = 2 * p + e
            s_ref[bi, h] = s_old[i] * jnp.exp(g_last[p][e]) + s_add[i]
            o = ws_qs[i][c:] + o_st[p][e * c : (e + 1) * c]
            o = o * lax.rsqrt(jnp.mean(o * o, -1, keepdims=True) + 1e-6) * ng_ref[...]
            o_ref[bi, rs, h * 128 : (h + 1) * 128] = o * _silu(zg[:, h * 128 : (h + 1) * 128])

    _lockstep(*(chain(bi, s) for s in range(step_rows // c) for bi in range(nb)))


def _gdn(z, zm, conv_w, par, norm_g, b, t):
    c, rows = CHUNK, STEP_CHUNKS * CHUNK
    z3 =z.reshape(b, t, z.shape[1])
    zm3 = zm.reshape(b, t, zm.shape[1])
    out = pl.pallas_call(
        _gdn_body,
        grid=(t // rows,),
        in_specs=[
            _seq_spec(b, rows, 1536, Z_GDN // 1536),
            _seq_spec(b, rows, 512, (Z_GDN + 1536) // 512),
            _seq_spec(b, rows, 128, MISC_SMALLS // 128),
            _bcast_spec((4, 1536)),
            _bcast_spec((8, 128)),
            _bcast_spec((1, 128)),
        ],
        out_specs=_seq_spec(b, rows, 512, 0),
        out_shape=jax.ShapeDtypeStruct((b, t, 512), f32),
        scratch_shapes=[pltpu.VMEM((b, GDN_HEADS, 128, 128), f32), pltpu.VMEM((b, c + 8, 1536), f32)],
        compiler_params=_cparams("arbitrary"),
        name="gdn",
    )(z3, z3, zm3, conv_w, par, norm_g)
    return out.reshape(b * t, 512)


def _mlstm_body(qk_ref, v_ref, og_ref, sm_ref, par_ref, ng_ref, o_ref, cs_ref, n_ref, m_ref):
    nb, step_rows, _ = qk_ref.shape
    c = CHUNK
    assert c == HALF

    @pl.when(pl.program_id(0) == 0)
    def _():
        cs_ref[...] = jnp.zeros(cs_ref.shape, f32)
        n_ref[...] = jnp.zeros(n_ref.shape, f32)
        m_ref[...] = jnp.zeros(m_ref.shape, f32)

    def cap(x):
        return MLSTM_SOFTCAP * jnp.tanh(x / MLSTM_SOFTCAP)

    incl, _ = _pair_masks(c)
    lane_lo = _iota2((1, LANES), 1) < HALF
    pairs = range(MLSTM_HEADS // 2)

    def col(a, lane):
        return a[:, lane : lane + 1]

    def row(a, lane):
        return a[lane : lane + 1, :]

    def chain(bi, s):
        rs = slice(s * c, (s + 1) * c)
        capped = cap(sm_ref[bi, rs, :] + par_ref[0:1, :])
        log_i = capped
        bcum = _cumsum_rows(-_softplus(-capped))
        log_i_t = log_i.T
        bcum_t = bcum.T
        qk = qk_ref[bi, rs, :].astype(f32)
        v_all = v_ref[bi, rs, :].astype(f32)
        og = og_ref[bi, rs, :].astype(f32)
        q_st, k_st, v_st, bc, li, b_last, dmat, m_intra = [], [], [], [], [], [], [], []
        for p in pairs:
            h0, h1 = 2 * p, 2 * p + 1
            q_st.append(_stack2(qk[:, p * 128 : (p + 1) * 128]))
            k_st.append(_stack2(qk[:, 256 + p * 128 : 256 + (p + 1) * 128] * MLSTM_DQK**-0.5))
            v_st.append(_rows(v_all[:, h0 * 128 : (h0 + 1) * 128], v_all[:, h1 * 128 : (h1 + 1) * 128]))
            bc.append(_rows(col(bcum, SM_ML_F + h0), col(bcum, SM_ML_F + h1)))
            li.append(_rows(col(log_i, SM_ML_I + h0), col(log_i, SM_ML_I + h1)))
            b_row = _lanes(row(bcum_t, SM_ML_F + h0), row(bcum_t, SM_ML_F + h1))
            li_row = _lanes(row(log_i_t, SM_ML_I + h0), row(log_i_t, SM_ML_I + h1))
            b_last.append(_half_col(c, bcum[c - 1 : c, SM_ML_F + h0 : SM_ML_F + h0 + 1],
                                    bcum[c - 1 : c, SM_ML_F + h1 : SM_ML_F + h1 + 1]))
            log_d = jnp.where(incl, bc[p] - b_row + li_row, -jnp.inf)
            m_intra.append(jnp.max(log_d, -1, keepdims=True))
            dmat.append(jnp.exp(log_d - m_intra[p]))

        qk_t = yield [functools.partial(_dot_nt, q_st[p], k_st[p]) for p in pairs]
        pm = [dmat[p] * qk_t[p] for p in pairs]
        num_intra = yield [functools.partial(_dot, pm[p], v_st[p]) for p in pairs]
        for _ in range(s):
            yield []
        cs_old = [cs_ref[bi, p] for p in pairs]
        m_old = [_half_col(c, m_ref[bi, 2 * p : 2 * p + 1, 0:1], m_ref[bi, 2 * p + 1 : 2 * p + 2, 0:1]) for p in pairs]
        kw_st, s_old, m_new = [], [], []
        for p in pairs:
            log_e = b_last[p] - bc[p] + li[p]
            m_end = _half_col(c, jnp.max(log_e[:c], 0, keepdims=True), jnp.max(log_e[c:], 0, keepdims=True))
            m_new.append(jnp.maximum(b_last[p] + m_old[p], m_end))
            s_old.append(jnp.exp(b_last[p] + m_old[p] - m_new[p]))
            kw_st.append(k_st[p] * (jnp.exp(log_e - m_end) * jnp.exp(m_end - m_new[p])))
        prods = yield ([functools.partial(_dot, q_st[p], cs_old[p]) for p in pairs]
                       + [functools.partial(_dot_tn, kw_st[p], v_st[p]) for p in pairs])
        q_cs, cs_add = prods[: len(pairs)], prods[len(pairs):]
        for p in pairs:
            nvec = n_ref[bi, p]
            den_intra = jnp.sum(pm[p], -1, keepdims=True)
            m_t = jnp.maximum(bc[p] + m_old[p], m_intra[p])
            s_inter = jnp.exp(bc[p] + m_old[p] - m_t)
            s_intra = jnp.exp(m_intra[p] - m_t)
            num = s_inter * q_cs[p] + s_intra * num_intra[p]
            den = s_inter * jnp.sum(q_st[p] * nvec, -1, keepdims=True) + s_intra * den_intra
            hc = num / jnp.maximum(jnp.abs(den), jnp.exp(-m_t))
            so0, so1 = s_old[p][0:1], s_old[p][c : c + 1]
            cs_ref[bi, p] = s_old[p] * cs_old[p] + cs_add[p]
            n_ref[bi, p] = jnp.where(lane_lo, so0, so1) * nvec + jnp.sum(kw_st[p], 0, keepdims=True)
            for e in range(2):
                h = 2 * p + e
                m_ref[bi, h : h + 1, :] = jnp.broadcast_to(m_new[p][e * c : e * c + 1], (1, LANES))
                hs = hc[e * c : (e + 1) * c]
                hs = hs * lax.rsqrt(jnp.mean(hs * hs, -1, keepdims=True) + 1e-6) * ng_ref[:, h * 128 : (h + 1) * 128]
                o_ref[bi, rs, h * 128 : (h + 1) * 128] = hs * _sigmoid(og[:, h * 128 : (h + 1) * 128])

    _lockstep(*(chain(bi, s) for s in range(step_rows // c) for bi in range(nb)))


def _mlstm(z, zm, par, norm_g, b, t):
    c, rows = CHUNK, STEP_CHUNKS * CHUNK
    z3 =z.reshape(b, t, z.shape[1])
    zm3 = zm.reshape(b, t, zm.shape[1])
    out = pl.pallas_call(
        _mlstm_body,
        grid=(t // rows,),
        in_specs=[
            _seq_spec(b, rows, 512, Z_MLSTM // 512),
            _seq_spec(b, rows, 512, (Z_MLSTM + 512) // 512),
            _seq_spec(b, rows, 512, (Z_MLSTM + 1024) // 512),
            _seq_spec(b, rows, 128, MISC_SMALLS // 128),
            _bcast_spec((8, 128)),
            _bcast_spec((1, 512)),
        ],
        out_specs=_seq_spec(b, rows, 512, 0),
        out_shape=jax.ShapeDtypeStruct((b, t, 512), f32),
        scratch_shapes=[
            pltpu.VMEM((b, MLSTM_HEADS // 2, 128, 128), f32),
            pltpu.VMEM((b, MLSTM_HEADS // 2, 1, 128), f32),
            pltpu.VMEM((b, 8, 128), f32),
        ],
        compiler_params=_cparams("arbitrary"),
        name="mlstm",
    )(z3, z3, z3, zm3, par, norm_g)
    return out.reshape(b * t, 512)


def _rwkv_body(*refs, has_vres):
    if has_vres:
        (r_ref, k_ref, v_ref, misc_ref, vf_ref, mu_ref, mum_ref, vecs_ref, w2_ref, a2_ref, g2_ref, v2_ref,
         y_ref, st_ref, tail_ref, tailm_ref) = refs
    else:
        (r_ref, k_ref, v_ref, misc_ref, mu_ref, mum_ref, vecs_ref, w2_ref, a2_ref, g2_ref,
         y_ref, vf_out_ref, st_ref, tail_ref, tailm_ref) = refs
    nb, step_rows, _ = r_ref.shape
    c = CHUNK

    @pl.when(pl.program_id(0) == 0)
    def _():
        st_ref[...] = jnp.zeros(st_ref.shape, f32)
        tail_ref[:, 0:8, :] = jnp.zeros((nb, 8, tail_ref.shape[2]), f32)
        tailm_ref[:, 0:8, :] = jnp.zeros((nb, 8, tailm_ref.shape[2]), f32)

    incl, strict = _pair_masks(c)
    same_head = (_iota2((LANES, LANES), 0) < HALF) == (_iota2((LANES, LANES), 1) < HALF)
    pairs = range(RWKV_HEADS // 2)
    sl = [slice(p * 128, (p + 1) * 128) for p in pairs]
    c2 = 2 * c
    w0, a0, k_k, k_a, r_k, lnx_g, lnx_b, v0 = (vecs_ref[i : i + 1, :] for i in range(8))

    def chain(bi, s):
        rs = slice(s * c, (s + 1) * c)
        raw = jnp.concatenate([r_ref[bi, rs, :], k_ref[bi, rs, :], v_ref[bi, rs, :]], axis=1).astype(f32)
        tail_ref[bi, 8 : 8 + c, :] = raw
        prev = tail_ref[bi, 7 : 7 + c, :]
        tail_ref[bi, 0:8, :] = tail_ref[bi, c : c + 8, :]
        rkv = raw + mu_ref[...] * (prev - raw)
        rawm = misc_ref[bi, rs, :]
        tailm_ref[bi, 8 : 8 + c, :] = rawm
        prevm = tailm_ref[bi, 7 : 7 + c, :]
        tailm_ref[bi, 0:8, :] = tailm_ref[bi, c : c + 8, :]
        misc = rawm + mum_ref[...] * (prevm - rawm)

        r = rkv[:, 0:512]
        k = rkv[:, 512:1024]
        v = rkv[:, 1024:1536]
        wa = misc[:, 0:128]
        lora = [functools.partial(_dot, jnp.tanh(wa), w2_ref[...]), functools.partial(_dot, wa, a2_ref[...]),
                functools.partial(_dot, _sigmoid(misc[:, 128:256]), g2_ref[...])]
        if has_vres:
            lora.append(functools.partial(_dot, misc[:, 256:384], v2_ref[...]))
        lora = yield lora
        log_w = -jnp.exp(-_softplus(-(w0 + lora[0])) - 0.5)
        a = _sigmoid(a0 + lora[1])
        g = lora[2]
        if has_vres:
            v = v + (vf_ref[bi, rs, :] - v) * _sigmoid(v0 + lora[3])
        else:
            vf_out_ref[bi, rs, :] = v
        kk = k * k_k
        kk = kk * lax.rsqrt(_head64_sums(kk * kk) + 1e-6)
        k = k * (1.0 + (a - 1.0) * k_a)
        alpha = -kk
        beta = kk * a
        lw = _cumsum_rows(log_w)
        lw_last = lw[c - 1 : c, :]
        inv = jnp.exp(-lw)
        al_bar = alpha * jnp.exp(lw - log_w)
        be_hat = beta * inv
        k_hat = k * inv
        r_bar = r * jnp.exp(lw)
        to_end = jnp.exp(lw_last - lw)
        be_end = beta * to_end
        k_end = k * to_end
        g_chunk = jnp.exp(lw_last)

        al_st = [_stack2(al_bar[:, s]) for s in sl]
        v_st = [_stack2(v[:, s]) for s in sl]
        m1 = yield [functools.partial(_dot_nt, _rows(al_st[p], _stack2(r_bar[:, s])),
                                      _rows(_stack2(be_hat[:, s]), _stack2(k_hat[:, s]))) for p, s in zip(pairs, sl)]
        t_inv = yield from _unit_lower_inverses_staged([jnp.where(strict, m[:c2, :c2], 0.0) for m in m1], c)
        l_ak = [jnp.where(strict, m[:c2, c2:], 0.0) for m in m1]
        q_rb = [jnp.where(incl, m[c2:, :c2], 0.0) for m in m1]
        q_rk = [jnp.where(incl, m[c2:, c2:], 0.0) for m in m1]
        m2 = yield [functools.partial(_dot, _rows(l_ak[p], q_rk[p]), v_st[p]) for p in pairs]
        m3 = yield [functools.partial(_dot, t_inv[p], _lanes(al_st[p], m2[p][:c2])) for p in pairs]
        for _ in range(2 * s):
            yield []
        st_old = [st_ref[bi, p] for p in pairs]
        m4 = yield [functools.partial(_dot_nt, _rows(_unstack2(m3[p][:, :128]), r_bar[:, sl[p]]), st_old[p]) for p in pairs]
        u = [m4[p][:c] + _unstack2(m3[p][:, 128:]) for p in pairs]
        m56 = yield ([functools.partial(_dot, q_rb[p], _stack2(u[p])) for p in pairs]
                     + [functools.partial(_dot_tn, _rows(u[p], v[:, sl[p]]), _rows(be_end[:, sl[p]], k_end[:, sl[p]]))
                        for p in pairs])
        m5, m6 = m56[: len(pairs)], m56[len(pairs):]
        ys = []
        for p in pairs:
            st_ref[bi, p] = g_chunk[:, sl[p]] * st_old[p] + jnp.where(same_head, m6[p], 0.0)
            ys.append(m4[p][c:] + _unstack2(m5[p]) + _unstack2(m2[p][c2:]))
        y = jnp.concatenate(ys, axis=1)
        mean = _head64_sums(y) * (1.0 / RWKV_DH)
        d = y - mean
        var = _head64_sums(d * d) * (1.0 / RWKV_DH)
        y = d * lax.rsqrt(var + RWKV_GN_EPS) * lnx_g + lnx_b
        bonus = _head64_sums(r * k * r_k) * v
        y_ref[bi, rs, :] = (y + bonus) * g

    _lockstep(*(chain(bi, s) for s in range(step_rows // c) for bi in range(nb)))


def _rwkv(z, zm, v_first, mu, mu_misc, vecs, w2p, a2p, g2, v2p, b, t):
    c, rows = CHUNK, STEP_CHUNKS * CHUNK
    has_vres = v_first is not None
    z3 = z.reshape(b, t, z.shape[1])
    blk = _seq_spec(b, rows, 512, 0)
    in_specs = [_seq_spec(b, rows, 512, (Z_RWKV + 512 * i) // 512) for i in range(3)] + [blk]
    args = [z3, z3, z3, zm.reshape(b, t, zm.shape[1])]
    if has_vres:
        in_specs.append(blk)
        args.append(v_first.reshape(b, t, 512))
    in_specs += [_bcast_spec(s) for s in ((1, 1536), (1, 512), (8, 512), (128, 512), (128, 512), (128, 512))]
    args += [mu, mu_misc, vecs, w2p, a2p, g2]
    if has_vres:
        in_specs.append(_bcast_spec((128, 512)))
        args.append(v2p)
    out_sd = jax.ShapeDtypeStruct((b, t, 512), f32)
    res = pl.pallas_call(
        functools.partial(_rwkv_body, has_vres=has_vres),
        grid=(t // rows,),
        in_specs=in_specs,
        out_specs=blk if has_vres else [blk, blk],
        out_shape=out_sd if has_vres else [out_sd, out_sd],
        scratch_shapes=[
            pltpu.VMEM((b, RWKV_HEADS // 2, 128, 128), f32),
            pltpu.VMEM((b, c + 8, 1536), f32),
            pltpu.VMEM((b, c + 8, 512), f32),
        ],
        compiler_params=_cparams("arbitrary"),
        name="rwkv_res" if has_vres else "rwkv",
    )(*args)
    if has_vres:
        return res.reshape(b * t, 512), v_first
    return res[0].reshape(b * t, 512), res[1].reshape(b * t, 512)


def _merge_body(ya_ref, yb_ref, yc_ref, ga_ref, gb_ref, gc_ref, x_ref, wb_ref, wo_ref, gt_ref, g_ref, b_ref,
                sc_ref, sh_ref, xn_ref, h_ref):
    merged = _sigmoid(ga_ref[...].astype(f32)) * _dot(ya_ref[...], wb_ref[0])
    merged = merged + _sigmoid(gb_ref[...].astype(f32)) * _dot(yb_ref[...], wb_ref[1])
    merged = merged + _sigmoid(gc_ref[...].astype(f32)) * _dot(yc_ref[...], wb_ref[2])
    y = _dot(merged, wo_ref[...])
    xn = _ln_rows(DEEPNORM_ALPHA * x_ref[...] + (1.0 + gt_ref[...]) * y, g_ref[...], b_ref[...])
    xn_ref[...] = xn
    h_ref[...] = (xn * (1.0 + sc_ref[...]) + sh_ref[...]).astype(h_ref.dtype)


def _merge(ya, yb, yc, z, x, wb, wo, mod, ln_g, ln_b, t, h_dtype):
    n = x.shape[0]
    tm = 512
    gate = lambda i: pl.BlockSpec((tm, 1024), lambda r: (r, Z_GATE // 1024 + i))
    return pl.pallas_call(
        _merge_body,
        grid=(n // tm,),
        in_specs=[
            _row_spec(tm, 512), _row_spec(tm, 512), _row_spec(tm, 512), gate(0), gate(1), gate(2), _row_spec(tm),
            pl.BlockSpec((3, 512, D_MODEL), lambda r: (0, 0, 0)),
            pl.BlockSpec((D_MODEL, D_MODEL), lambda r: (0, 0)),
            _mod_spec(t, tm, 2), _vec_spec(), _vec_spec(), _mod_spec(t, tm, 4), _mod_spec(t, tm, 3),
        ],
        out_specs=[_row_spec(tm), _row_spec(tm)],
        out_shape=[jax.ShapeDtypeStruct((n, D_MODEL), f32), jax.ShapeDtypeStruct((n, D_MODEL), h_dtype)],
        compiler_params=_cparams("arbitrary"),
        name="merge",
    )(ya, yb, yc, z, z, z, x, wb, wo, mod, ln_g, ln_b, mod, mod)


def _ffn_body(h_ref, x_ref, wg_ref, wu_ref, wd_ref, gt_ref, g_ref, b_ref, sc_ref, sh_ref, xn_ref, hn_ref, acc_ref):
    f = pl.program_id(1)

    @pl.when(f == 0)
    def _():
        acc_ref[...] = jnp.zeros(acc_ref.shape, f32)

    h = h_ref[...]
    act = _silu(_dot(h, wg_ref[...])) * _dot(h, wu_ref[...])
    acc_ref[...] += _dot(act, wd_ref[...])

    @pl.when(f == pl.num_programs(1) - 1)
    def _():
        xn = _ln_rows(DEEPNORM_ALPHA * x_ref[...] + (1.0 + gt_ref[...]) * acc_ref[...], g_ref[...], b_ref[...])
        xn_ref[...] = xn
        hn_ref[...] = (xn * (1.0 + sc_ref[...]) + sh_ref[...]).astype(hn_ref.dtype)


def _ffn(h, x, wg, wu, wd, mod, mod_next, ln_g, ln_b, t):
    n = x.shape[0]
    dff = wg.shape[1]
    tm, tf = 512, dff // 2
    row = pl.BlockSpec((tm, D_MODEL), lambda i, f: (i, 0))
    vec = pl.BlockSpec((1, D_MODEL), lambda i, f: (0, 0))
    modspec = lambda slot: pl.BlockSpec((None, 1, D_MODEL), lambda i, f: ((i * tm // t) * 6 + slot, 0, 0))
    return pl.pallas_call(
        _ffn_body,
        grid=(n // tm, dff // tf),
        in_specs=[
            row, row,
            pl.BlockSpec((D_MODEL, tf), lambda i, f: (0, f)),
            pl.BlockSpec((D_MODEL, tf), lambda i, f: (0, f)),
            pl.BlockSpec((tf, D_MODEL), lambda i, f: (f, 0)),
            modspec(5), vec, vec, modspec(1), modspec(0),
        ],
        out_specs=[row, row],
        out_shape=[jax.ShapeDtypeStruct((n, D_MODEL), f32), jax.ShapeDtypeStruct((n, D_MODEL), bf16)],
        scratch_shapes=[pltpu.VMEM((tm, D_MODEL), f32)],
        compiler_params=_cparams("arbitrary", "arbitrary"),
        name="ffn",
    )(h, x, wg, wu, wd, mod, ln_g, ln_b, mod_next, mod_next)


def _router_body(h_ref, w_ref, b_ref, idx_ref, gate_ref):
    logits = jnp.dot(h_ref[...], w_ref[...], precision=lax.Precision.HIGHEST, preferred_element_type=f32) + b_ref[...]
    lane = _iota2(logits.shape, 1)
    logits = jnp.where(lane < N_EXPERTS, logits, -jnp.inf)
    m1 = jnp.max(logits, -1, keepdims=True)
    i1 = jnp.min(jnp.where(logits == m1, lane, LANES), -1, keepdims=True)
    rest = jnp.where(lane == i1, -jnp.inf, logits)
    m2 = jnp.max(rest, -1, keepdims=True)
    i2 = jnp.min(jnp.where(rest == m2, lane, LANES), -1, keepdims=True)
    e = jnp.exp(m2 - m1)
    g1 = 1.0 / (1.0 + e)
    g2 = e / (1.0 + e)
    idx_ref[...] = jnp.where(lane == 0, i1, jnp.where(lane == 1, i2, 0))
    gate_ref[...] = jnp.where(lane == 0, g1, jnp.where(lane == 1, g2, 0.0))


def _router(h, w_pad, b_pad):
    n = h.shape[0]
    tm = 512
    return pl.pallas_call(
        _router_body,
        grid=(n // tm,),
        in_specs=[_row_spec(tm), pl.BlockSpec((D_MODEL, LANES), lambda i: (0, 0)), _vec_spec(LANES)],
        out_specs=[_row_spec(tm, LANES), _row_spec(tm, LANES)],
        out_shape=[jax.ShapeDtypeStruct((n, LANES), i32), jax.ShapeDtypeStruct((n, LANES), f32)],
        compiler_params=_cparams("arbitrary"),
        name="router",
    )(h, w_pad, b_pad)


def _moe_body(be_ref, live_ref, tok_ref, tokn_ref, dst_ref, gate_ref, h_hbm, wg_ref, wu_ref, wd_ref, y_hbm,
              xbuf, xb16, acc_ref, obuf, gsem, ssem):
    j = pl.program_id(0)
    f = pl.program_id(1)
    nb = pl.num_programs(0)
    nf = pl.num_programs(1)
    rows = xbuf.shape[1]
    slot = j % 2
    live = live_ref[j] > 0
    next_live = live_ref[jnp.minimum(j + 1, nb - 1)] > 0

    def gather(ref, dst_slot):
        def start(r, carry):
            pltpu.make_async_copy(h_hbm.at[pl.ds(ref[0, 0, r], 1), :], xbuf.at[dst_slot, pl.ds(r, 1), :],
                                  gsem.at[dst_slot]).start()
            return carry

        lax.fori_loop(0, rows, start, 0, unroll=8)

    def gather_wait(src_slot):
        pltpu.make_async_copy(h_hbm.at[pl.ds(0, rows), :], xbuf.at[src_slot], gsem.at[src_slot]).wait()

    def scatter_wait(src_slot):
        pltpu.make_async_copy(obuf.at[src_slot], y_hbm.at[pl.ds(0, rows), :], ssem.at[src_slot]).wait()

    @pl.when(jnp.logical_and(f == 0, j == 0))
    def _():
        obuf[1] = jnp.zeros(obuf.shape[1:], f32)
        spare = pltpu.make_async_copy(obuf.at[1], y_hbm.at[pl.ds(y_hbm.shape[0] - rows, rows), :], ssem.at[1])
        spare.start()
        spare.wait()

    @pl.when(jnp.logical_and(f == 0, jnp.logical_and(j == 0, live)))
    def _():
        gather(tok_ref, 0)

    @pl.when(jnp.logical_and(f == 0, live))
    def _():
        gather_wait(slot)
        xb16[...] = xbuf[slot].astype(bf16)
        acc_ref[...] = jnp.zeros(acc_ref.shape, f32)

    @pl.when(jnp.logical_and(f == 0, jnp.logical_and(j + 1 < nb, next_live)))
    def _():
        gather(tokn_ref, 1 - slot)

    @pl.when(live)
    def _():
        x = xb16[...]
        act = _silu(_dot(x, wg_ref[...])) * _dot(x, wu_ref[...])
        acc_ref[...] += _dot(act, wd_ref[...])

    @pl.when(jnp.logical_and(f == nf - 1, live))
    def _():
        obuf[slot] = acc_ref[...] * gate_ref[...]

        def start(r, carry):
            pltpu.make_async_copy(obuf.at[slot, pl.ds(r, 1), :], y_hbm.at[pl.ds(dst_ref[0, 0, r], 1), :],
                                  ssem.at[slot]).start()
            return carry

        lax.fori_loop(0, rows, start, 0, unroll=8)

    @pl.when(jnp.logical_and(f == nf - 1, jnp.logical_and(j > 0, live_ref[jnp.maximum(j - 1, 0)] > 0)))
    def _():
        scatter_wait(1 - slot)

    @pl.when(jnp.logical_and(f == nf - 1, jnp.logical_and(j == nb - 1, live)))
    def _():
        scatter_wait(slot)


def _moe(h, block_e, block_live, row_tok, row_dst, row_gate, wg, wu, wd):
    n = h.shape[0]
    n_blocks = block_e.shape[0]
    rb = MOE_ROWS
    dff = wg.shape[2]
    tf = 512
    smem_blk = pl.BlockSpec((1, 1, rb), lambda j, f, be, lv: (j, 0, 0), memory_space=pltpu.SMEM)
    smem_next = pl.BlockSpec((1, 1, rb), lambda j, f, be, lv: (jnp.minimum(j + 1, n_blocks - 1), 0, 0),
                             memory_space=pltpu.SMEM)
    grid_spec = pltpu.PrefetchScalarGridSpec(
        num_scalar_prefetch=2,
        grid=(n_blocks, dff // tf),
        in_specs=[
            smem_blk,
            smem_next,
            smem_blk,
            pl.BlockSpec((rb, 1), lambda j, f, be, lv: (j, 0)),
            pl.BlockSpec(memory_space=pl.ANY),
            pl.BlockSpec((None, D_MODEL, tf), lambda j, f, be, lv: (be[j], 0, f)),
            pl.BlockSpec((None, D_MODEL, tf), lambda j, f, be, lv: (be[j], 0, f)),
            pl.BlockSpec((None, tf, D_MODEL), lambda j, f, be, lv: (be[j], f, 0)),
        ],
        out_specs=pl.BlockSpec(memory_space=pl.ANY),
        scratch_shapes=[
            pltpu.VMEM((2, rb, D_MODEL), f32),
            pltpu.VMEM((rb, D_MODEL), bf16),
            pltpu.VMEM((rb, D_MODEL), f32),
            pltpu.VMEM((2, rb, D_MODEL), f32),
            pltpu.SemaphoreType.DMA((2,)),
            pltpu.SemaphoreType.DMA((2,)),
        ],
    )
    return pl.pallas_call(
        _moe_body,
        grid_spec=grid_spec,
        out_shape=jax.ShapeDtypeStruct((TOP_K * n + rb, D_MODEL), f32),
        compiler_params=_cparams("arbitrary", "arbitrary"),
        name="moe",
    )(block_e, block_live, row_tok, row_tok, row_dst, row_gate, h, wg, wu, wd)


def _routing_tables(top_e, top_g, n):
    rb = MOE_ROWS
    n_assign = n * TOP_K
    n_blocks = -(-n_assign // rb) + N_EXPERTS
    rows = n_blocks * rb
    e_flat = top_e.reshape(-1)
    onehot = (e_flat[:, None] == jnp.arange(N_EXPERTS, dtype=i32)[None, :]).astype(i32)
    rank = jnp.sum((jnp.cumsum(onehot, axis=0) - onehot) * onehot, axis=1)
    counts = jnp.sum(onehot, axis=0)
    padded = (counts + rb - 1) // rb * rb
    pad_end = jnp.cumsum(padded)
    pad_start = pad_end - padded
    dest = jnp.sum(onehot * pad_start[None, :], axis=1) + rank
    row_a = jnp.full((rows,), n_assign, i32).at[dest].set(jnp.arange(n_assign, dtype=i32))
    valid = row_a < n_assign
    tok, slot_k = row_a // TOP_K, row_a % TOP_K
    row_tok = jnp.where(valid, tok, 0)
    row_dst = jnp.where(valid, slot_k * n + tok, n_assign + jnp.arange(rows, dtype=i32) % rb)
    row_gate = jnp.where(valid, top_g.reshape(-1)[jnp.minimum(row_a, n_assign - 1)], 0.0)
    starts = jnp.arange(n_blocks, dtype=i32) * rb
    block_e = jnp.minimum(jnp.sum((pad_end[None, :] <= starts[:, None]).astype(i32), axis=1), N_EXPERTS - 1)
    block_live = (starts < pad_end[-1]).astype(i32)
    return (block_e, block_live, row_tok.reshape(n_blocks, 1, rb), row_dst.reshape(n_blocks, 1, rb),
            row_gate.reshape(rows, 1))


def _final_body(x_ref, y0_ref, y1_ref, gt_ref, g_ref, b_ref, o_ref):
    y = y0_ref[...] + y1_ref[...]
    o_ref[...] = _ln_rows(DEEPNORM_ALPHA * x_ref[...] + (1.0 + gt_ref[...]) * y, g_ref[...], b_ref[...])


def _final_ln(x, y2, mod, ln_g, ln_b, t):
    n = x.shape[0]
    tm = 512
    nb = n // tm
    return pl.pallas_call(
        _final_body,
        grid=(nb,),
        in_specs=[_row_spec(tm), _row_spec(tm), pl.BlockSpec((tm, D_MODEL), lambda i: (i + nb, 0)),
                  _mod_spec(t, tm, 5), _vec_spec(), _vec_spec()],
        out_specs=_row_spec(tm),
        out_shape=jax.ShapeDtypeStruct((n, D_MODEL), f32),
        compiler_params=_cparams("arbitrary"),
        name="final_ln",
    )(x, y2, y2, mod, ln_g, ln_b)


def _pad_cols(w, width):
    return jnp.pad(w, ((0, 0), (0, width - w.shape[1])))


def _pad_rows(w, height):
    return jnp.pad(w, ((0, height - w.shape[0]), (0, 0)))


def _regroup_w_in(w_in, v1):
    d = w_in.shape[0]
    gdn, ml, rw = 0, 2056, 3600
    smalls = jnp.concatenate([w_in[:, gdn + 2048 : gdn + 2056], w_in[:, ml + 1536 : ml + 1544]], axis=1)
    vdown = jnp.zeros((d, LORA_V), w_in.dtype) if v1 is None else v1
    misc = jnp.concatenate([w_in[:, rw + 1536 : rw + 1792], _pad_cols(vdown, 128), _pad_cols(smalls, 128)], axis=1)
    cat = jnp.concatenate(
        [w_in[:, gdn : gdn + 2048], w_in[:, ml : ml + 1536], w_in[:, rw : rw + 1536], w_in[:, 5392:8464]], axis=1)
    assert cat.shape[1] == Z_WIDTH and misc.shape[1] == MISC_WIDTH
    return cat.astype(bf16), misc.astype(bf16)


def _row(v, width=None):
    v = v.reshape(1, -1).astype(f32)
    return v if width is None else _pad_cols(v, width)


def kernel(x, c, ln_in_g, ln_in_b, ada_w, ada_b, w_in, gdn_conv, gdn_a_log, gdn_dt_bias, gdn_norm_g, mlstm_b_i, mlstm_b_f, mlstm_norm_g, rwkv_mu, rwkv_w0, rwkv_w2, rwkv_a0, rwkv_a2, rwkv_g2, rwkv_k_k, rwkv_k_a, rwkv_r_k, rwkv_lnx_g, rwkv_lnx_b, rwkv_v1, rwkv_mu_v1, rwkv_v0, rwkv_v2, w_branch, w_o, ln1_g, ln1_b, ln2_g, ln2_b, ffn_w_gate, ffn_w_up, ffn_w_down, moe_router, moe_router_b, moe_w_gate, moe_w_up, moe_w_down):
    b, t, d = x.shape
    n = b * t
    depth = w_in.shape[0]
    xf = x.reshape(n, d)

    c_pad = _pad_rows(c.astype(f32), 8)
    mods = []
    for layer in range(depth):
        mod = _ada_mod(c_pad, ada_w[layer].astype(bf16), _row(ada_b[layer]))[:b]
        mods.append(mod.reshape(b * 6, 1, d))

    xcur, h = _entry_ln(xf, _row(ln_in_g), _row(ln_in_b), mods[0], t)
    v_first = None
    out = None
    for layer in range(depth):
        mod = mods[layer]
        has_vres = layer > 0
        w_main, w_misc = _regroup_w_in(w_in[layer], rwkv_v1[layer - 1] if has_vres else None)
        z = _matmul(h, w_main, 1024, 1024, bf16, "in_proj")
        zm = _matmul(h, w_misc, 1024, MISC_WIDTH, f32, "in_proj_misc")

        gdn_par = jnp.zeros((8, LANES), f32).at[0, 0:4].set(gdn_a_log[layer]).at[1, 0:4].set(gdn_dt_bias[layer])
        y_a = _gdn(z, zm, gdn_conv[layer].astype(f32), gdn_par, _row(gdn_norm_g[layer]), b, t)

        ml_par = jnp.zeros((8, LANES), f32).at[0, SM_ML_I : SM_ML_I + 4].set(mlstm_b_i[layer])
        ml_par = ml_par.at[0, SM_ML_F : SM_ML_F + 4].set(mlstm_b_f[layer])
        y_b = _mlstm(z, zm, ml_par, _row(mlstm_norm_g[layer]), b, t)

        mu = rwkv_mu[layer].astype(f32)
        mu_v = rwkv_mu_v1[layer - 1].astype(f32) if has_vres else jnp.zeros((LORA_V,), f32)
        mu_misc = _row(jnp.concatenate([mu[1536:1792], mu_v]), 512)
        vecs = jnp.stack([
            rwkv_w0[layer], rwkv_a0[layer], rwkv_k_k[layer], rwkv_k_a[layer], rwkv_r_k[layer].reshape(-1),
            rwkv_lnx_g[layer], rwkv_lnx_b[layer],
            rwkv_v0[layer - 1] if has_vres else jnp.zeros((RWKV_WIDTH,), f32),
        ]).astype(f32)
        w2p = _pad_rows(rwkv_w2[layer], 128).astype(bf16)
        a2p = jnp.concatenate([jnp.zeros((LORA_W, RWKV_WIDTH), f32), rwkv_a2[layer]], axis=0).astype(bf16)
        v2p = _pad_rows(rwkv_v2[layer - 1], 128).astype(bf16) if has_vres else None
        y_c, v_first = _rwkv(z, zm, v_first, _row(mu[:1536]), mu_misc, vecs, w2p, a2p, rwkv_g2[layer].astype(bf16), v2p, b, t)

        moe_layer = layer % 2 == 1
        xcur, h2 = _merge(y_a, y_b, y_c, z, xcur, w_branch[layer].astype(bf16), w_o[layer].astype(bf16), mod,
                          _row(ln1_g[layer]), _row(ln1_b[layer]), t, f32 if moe_layer else bf16)
        i = layer // 2
        if not moe_layer:
            mod_next = mods[layer + 1] if layer + 1 < depth else mod
            xcur, h = _ffn(h2, xcur, ffn_w_gate[i].astype(bf16), ffn_w_up[i].astype(bf16), ffn_w_down[i].astype(bf16),
                           mod, mod_next, _row(ln2_g[layer]), _row(ln2_b[layer]), t)
            out = xcur
        else:
            idx, gate = _router(h2, _pad_cols(moe_router[i].astype(f32), LANES), _row(moe_router_b[i], LANES))
            tables = _routing_tables(idx[:, :TOP_K], gate[:, :TOP_K], n)
            y2 = _moe(h2, *tables, moe_w_gate[i], moe_w_up[i], moe_w_down[i])
            out = _final_ln(xcur, y2, mod, _row(ln2_g[layer]), _row(ln2_b[layer]), t)
            xcur = out
    return out.reshape(b, t, d)
```

```python
import functools
import math

import jax
import jax.numpy as jnp
from jax import lax
from jax.experimental import pallas as pl
from jax.experimental.pallas import tpu as pltpu

f32 = jnp.float32
bf16 = jnp.bfloat16
i32 = jnp.int32

D_MODEL = 1024
DEPTH = 2
CHUNK = 64
GDN_HEADS = 4
GDN_DK = 128
MLSTM_HEADS = 4
MLSTM_DQK = 64
MLSTM_DV = 128
MLSTM_SOFTCAP = 15.0
RWKV_HEADS = 8
RWKV_DH = 64
RWKV_WIDTH = RWKV_HEADS * RWKV_DH
LORA_W = 64
LORA_A = 64
LORA_V = 32
LORA_G = 128
N_EXPERTS = 8
TOP_K = 2
DEEPNORM_ALPHA = (2 * DEPTH) ** 0.25
LN_EPS = 1e-5
RWKV_GN_EPS = 64e-5

LANES = 128
HALF = LANES // 2

Z_GDN = 0
Z_MLSTM = 2048
Z_RWKV = 3584
Z_GATE = 5120
Z_WIDTH = 8192
MISC_WIDTH = 512
MISC_SMALLS = 384
SM_GDN_A, SM_GDN_B, SM_ML_I, SM_ML_F = 0, 4, 8, 12

STEP_CHUNKS = 4
MOE_ROWS = 1024
VMEM_LIMIT = 56 * 1024 * 1024


def _cparams(*sem):
    return pltpu.CompilerParams(dimension_semantics=sem, vmem_limit_bytes=VMEM_LIMIT)


def _dot(a, b):
    return jnp.dot(a.astype(bf16), b.astype(bf16), preferred_element_type=f32)


def _dot_nt(a, b):
    return lax.dot_general(a.astype(bf16), b.astype(bf16), (((1,), (1,)), ((), ())), preferred_element_type=f32)


def _dot_tn(a, b):
    return lax.dot_general(a.astype(bf16), b.astype(bf16), (((0,), (0,)), ((), ())), preferred_element_type=f32)


def _sigmoid(x):
    return 1.0 / (1.0 + jnp.exp(-x))


def _silu(x):
    return x * _sigmoid(x)


def _softplus(x):
    return jnp.maximum(x, 0.0) + jnp.log1p(jnp.exp(-jnp.abs(x)))


def _iota2(shape, axis):
    return lax.broadcasted_iota(i32, shape, axis)


def _cumsum_rows(x):
    n = x.shape[0]
    rows = _iota2(x.shape, 0)
    sh = 1
    while sh < n:
        x = x + jnp.where(rows >= sh, pltpu.roll(x, sh, 0), 0.0)
        sh *= 2
    return x


def _unit_lower_inverses(ps, nilpotent):
    n = ps[0].shape[0]
    eye = (_iota2((n, n), 0) == _iota2((n, n), 1)).astype(f32)
    ts = [eye + p for p in ps]
    for _ in range(int(math.log2(nilpotent)) - 1):
        ps = [_dot(p, p) for p in ps]
        ts = [t + _dot(t, p) for t, p in zip(ts, ps)]
    return ts


def _rows(*xs):
    return jnp.concatenate(xs, axis=0)


def _lanes(*xs):
    return jnp.concatenate(xs, axis=1)


def _half_masks(shape):
    lane = _iota2(shape, len(shape) - 1)
    return lane < HALF, lane >= HALF


def _stack2(x):
    lo, _ = _half_masks(x.shape)
    return _rows(jnp.where(lo, x, 0.0), jnp.where(lo, 0.0, x))


def _unstack2(x):
    c = x.shape[0] // 2
    return x[:c] + x[c:]


def _pair_masks(c):
    r = _iota2((2 * c, 2 * c), 0)
    q = _iota2((2 * c, 2 * c), 1)
    same = (r < c) == (q < c)
    rr = jnp.where(r < c, r, r - c)
    qq = jnp.where(q < c, q, q - c)
    return jnp.logical_and(same, rr >= qq), jnp.logical_and(same, rr > qq)


def _half_col(c, v0, v1):
    return jnp.where(_iota2((2 * c, 1), 0) < c, v0, v1)


def _ln_rows(r, g, b):
    mu = jnp.mean(r, -1, keepdims=True)
    d = r - mu
    var = jnp.mean(d * d, -1, keepdims=True)
    return d * lax.rsqrt(var + LN_EPS) * g + b


def _head64_sums(x):
    lo, _ = _half_masks((x.shape[0], LANES))
    outs = []
    for gi in range(x.shape[1] // LANES):
        xs = x[:, gi * LANES : (gi + 1) * LANES]
        s_lo = jnp.sum(jnp.where(lo, xs, 0.0), -1, keepdims=True)
        s_hi = jnp.sum(jnp.where(lo, 0.0, xs), -1, keepdims=True)
        outs.append(jnp.where(lo, s_lo, s_hi))
    return jnp.concatenate(outs, axis=1)


def _ada_body(c_ref, w_ref, b_ref, o_ref):
    o_ref[...] = _dot(_silu(c_ref[...]), w_ref[...]) + b_ref[...]


def _ada_mod(c_pad, w, b):
    n = w.shape[1]
    tn = n // 4
    return pl.pallas_call(
        _ada_body,
        grid=(n // tn,),
        in_specs=[
            pl.BlockSpec(c_pad.shape, lambda j: (0, 0)),
            pl.BlockSpec((w.shape[0], tn), lambda j: (0, j)),
            pl.BlockSpec((1, tn), lambda j: (0, j)),
        ],
        out_specs=pl.BlockSpec((c_pad.shape[0], tn), lambda j: (0, j)),
        out_shape=jax.ShapeDtypeStruct((c_pad.shape[0], n), f32),
        compiler_params=_cparams("arbitrary"),
        name="ada_mod",
    )(c_pad, w, b)


def _entry_body(x_ref, g_ref, b_ref, sc_ref, sh_ref, xn_ref, h_ref):
    xn = _ln_rows(x_ref[...], g_ref[...], b_ref[...])
    xn_ref[...] = xn
    h_ref[...] = (xn * (1.0 + sc_ref[...]) + sh_ref[...]).astype(h_ref.dtype)


def _mod_spec(rows_per_batch, tm, slot):
    return pl.BlockSpec((None, 1, D_MODEL), lambda i: ((i * tm // rows_per_batch) * 6 + slot, 0, 0))


def _row_spec(tm, width=D_MODEL):
    return pl.BlockSpec((tm, width), lambda i: (i, 0))


def _vec_spec(width=D_MODEL):
    return pl.BlockSpec((1, width), lambda i: (0, 0))


def _entry_ln(x, g, b, mod, t):
    n = x.shape[0]
    tm = 512
    return pl.pallas_call(
        _entry_body,
        grid=(n // tm,),
        in_specs=[_row_spec(tm), _vec_spec(), _vec_spec(), _mod_spec(t, tm, 1), _mod_spec(t, tm, 0)],
        out_specs=[_row_spec(tm), _row_spec(tm)],
        out_shape=[jax.ShapeDtypeStruct((n, D_MODEL), f32), jax.ShapeDtypeStruct((n, D_MODEL), bf16)],
        compiler_params=_cparams("arbitrary"),
        name="entry_ln",
    )(x, g, b, mod, mod)


def _mm_body(x_ref, w_ref, o_ref):
    o_ref[...] = jnp.dot(x_ref[...], w_ref[...], preferred_element_type=f32).astype(o_ref.dtype)


def _matmul(x, w, tm, tn, out_dtype, name):
    m, k = x.shape
    n = w.shape[1]
    return pl.pallas_call(
        _mm_body,
        grid=(m // tm, n // tn),
        in_specs=[pl.BlockSpec((tm, k), lambda i, j: (i, 0)), pl.BlockSpec((k, tn), lambda i, j: (0, j))],
        out_specs=pl.BlockSpec((tm, tn), lambda i, j: (i, j)),
        out_shape=jax.ShapeDtypeStruct((m, n), out_dtype),
        compiler_params=_cparams("arbitrary", "arbitrary"),
        name=name,
    )(x, w)


def _seq_spec(nb, c, width, col_block):
    return pl.BlockSpec((nb, c, width), lambda n: (0, n, col_block))


def _bcast_spec(shape):
    return pl.BlockSpec(shape, lambda n: (0,) * len(shape))


def _lockstep(*chains):
    pending = [(g, None) for g in chains]
    while pending:
        requests = []
        for g, value in pending:
            try:
                requests.append((g, g.send(value)))
            except StopIteration:
                pass
        pending = [(g, [thunk() for thunk in thunks]) for g, thunks in requests]


def _unit_lower_inverses_staged(ps, nilpotent):
    n = ps[0].shape[0]
    eye = (_iota2((n, n), 0) == _iota2((n, n), 1)).astype(f32)
    ts = [eye + p for p in ps]
    for _ in range(int(math.log2(nilpotent)) - 1):
        ps = yield [functools.partial(_dot, p, p) for p in ps]
        prods = yield [functools.partial(_dot, t, p) for t, p in zip(ts, ps)]
        ts = [t + d for t, d in zip(ts, prods)]
    return ts


def _gdn_body(qkv_ref, zg_ref, sm_ref, conv_ref, par_ref, ng_ref, o_ref, s_ref, buf_ref):
    nb, step_rows, w3 = qkv_ref.shape
    c = CHUNK

    @pl.when(pl.program_id(0) == 0)
    def _():
        s_ref[...] = jnp.zeros(s_ref.shape, f32)
        buf_ref[:, 0:8, :] = jnp.zeros((nb, 8, w3), f32)

    incl, strict = _pair_masks(c)
    pairs = range(GDN_HEADS // 2)
    heads = [(p, e) for p in pairs for e in range(2)]

    def col(a, lane):
        return a[:, lane : lane + 1]

    def chain(bi, s):
        rs = slice(s * c, (s + 1) * c)
        raw = qkv_ref[bi, rs, :].astype(f32)
        buf_ref[bi, 8 : 8 + c, :] = raw
        conv = conv_ref[3:4, :] * raw
        for j in range(3):
            conv = conv + conv_ref[j : j + 1, :] * buf_ref[bi, 5 + j : 5 + j + c, :]
        buf_ref[bi, 0:8, :] = buf_ref[bi, c : c + 8, :]
        x = _silu(conv)

        sm = sm_ref[bi, rs, :]
        gcum = _cumsum_rows(-jnp.exp(par_ref[0:1, :]) * _softplus(sm + par_ref[1:2, :]))
        gcum_t = gcum.T
        beta_all = _sigmoid(sm)
        zg = zg_ref[bi, rs, :].astype(f32)

        def head(i, h):
            return x[:, i * 512 + h * 128 : i * 512 + (h + 1) * 128]

        k_st, kb_st, q_st, v_st, g_st, eg_st, beta_st, decay, g_last = [], [], [], [], [], [], [], [], []
        for p in pairs:
            h0, h1 = 2 * p, 2 * p + 1
            qs, ks = [], []
            for h in (h0, h1):
                q = head(0, h)
                k = head(1, h)
                qs.append(q * (lax.rsqrt(jnp.sum(q * q, -1, keepdims=True) + 1e-6) * GDN_DK**-0.5))
                ks.append(k * lax.rsqrt(jnp.sum(k * k, -1, keepdims=True) + 1e-6))
            g = _rows(col(gcum, SM_GDN_A + h0), col(gcum, SM_GDN_A + h1))
            g_row = _lanes(gcum_t[SM_GDN_A + h0 : SM_GDN_A + h0 + 1, :], gcum_t[SM_GDN_A + h1 : SM_GDN_A + h1 + 1, :])
            beta = _rows(col(beta_all, SM_GDN_B + h0), col(beta_all, SM_GDN_B + h1))
            k2 = _rows(*ks)
            k_st.append(k2)
            kb_st.append(k2 * beta)
            q_st.append(_rows(*qs))
            v_st.append(_rows(head(2, h0), head(2, h1)))
            g_st.append(g)
            eg_st.append(jnp.exp(g))
            beta_st.append(beta)
            decay.append(jnp.where(incl, jnp.exp(g - g_row), 0.0))
            g_last.append((gcum[c - 1 : c, SM_GDN_A + h0 : SM_GDN_A + h0 + 1], gcum[c - 1 : c, SM_GDN_A + h1 : SM_GDN_A + h1 + 1]))

        m1 = yield [functools.partial(_dot_nt, _rows(kb_st[p], q_st[p]), k_st[p]) for p in pairs]
        t_inv = yield from _unit_lower_inverses_staged(
            [-jnp.where(strict, m1[p][: 2 * c] * decay[p], 0.0) for p in pairs], c)
        qk = [m1[p][2 * c :] * decay[p] for p in pairs]
        uw = yield [functools.partial(_dot, t_inv[p], _lanes(v_st[p] * beta_st[p], kb_st[p] * eg_st[p])) for p in pairs]
        qd = [q_st[p] * eg_st[p] for p in pairs]
        k_dec = [k_st[p][e * c : (e + 1) * c] * jnp.exp(g_last[p][e] - g_st[p][e * c : (e + 1) * c]) for p, e in heads]
        for _ in range(2 * s):
            yield []
        s_old = [s_ref[bi, 2 * p + e] for p, e in heads]
        ws_qs = yield [functools.partial(_dot, _rows(uw[p][e * c : (e + 1) * c, 128:], qd[p][e * c : (e + 1) * c]), st)
                       for (p, e), st in zip(heads, s_old)]
        v_new = [uw[p][e * c : (e + 1) * c, :128] - m[:c] for (p, e), m in zip(heads, ws_qs)]
        prods = yield ([functools.partial(_dot, qk[p], _rows(v_new[2 * p], v_new[2 * p + 1])) for p in pairs]
                       + [functools.partial(_dot_tn, kd, vn) for kd, vn in zip(k_dec, v_new)])
        o_st, s_add = prods[: len(pairs)], prods[len(pairs):]
        for i, (p, e) in enumerate(heads):
            h = 2 * p + e
            s_ref[bi, h] = s_old[i] * jnp.exp(g_last[p][e]) + s_add[i]
            o = ws_qs[i][c:] + o_st[p][e * c : (e + 1) * c]
            o = o * lax.rsqrt(jnp.mean(o * o, -1, keepdims=True) + 1e-6) * ng_ref[...]
            o_ref[bi, rs, h * 128 : (h + 1) * 128] = o * _silu(zg[:, h * 128 : (h + 1) * 128])

    _lockstep(*(chain(bi, s) for s in range(step_rows // c) for bi in range(nb)))


def _gdn(z, zm, conv_w, par, norm_g, b, t):
    c, rows = CHUNK, STEP_CHUNKS * CHUNK
    z3 =z.reshape(b, t, z.shape[1])
    zm3 = zm.reshape(b, t, zm.shape[1])
    out = pl.pallas_call(
        _gdn_body,
        grid=(t // rows,),
        in_specs=[
            _seq_spec(b, rows, 1536, Z_GDN // 1536),
            _seq_spec(b, rows, 512, (Z_GDN + 1536) // 512),
            _seq_spec(b, rows, 128, MISC_SMALLS // 128),
            _bcast_spec((4, 1536)),
            _bcast_spec((8, 128)),
            _bcast_spec((1, 128)),
        ],
        out_specs=_seq_spec(b, rows, 512, 0),
        out_shape=jax.ShapeDtypeStruct((b, t, 512), f32),
        scratch_shapes=[pltpu.VMEM((b, GDN_HEADS, 128, 128), f32), pltpu.VMEM((b, c + 8, 1536), f32)],
        compiler_params=_cparams("arbitrary"),
        name="gdn",
    )(z3, z3, zm3, conv_w, par, norm_g)
    return out.reshape(b * t, 512)


def _mlstm_body(qk_ref, v_ref, og_ref, sm_ref, par_ref, ng_ref, o_ref, cs_ref, m_ref):
    nb, step_rows, _ = qk_ref.shape
    c = CHUNK
    assert c == HALF

    @pl.when(pl.program_id(0) == 0)
    def _():
        cs_ref[...] = jnp.zeros(cs_ref.shape, f32)
        m_ref[...] = jnp.zeros(m_ref.shape, f32)

    def cap(x):
        return MLSTM_SOFTCAP * jnp.tanh(x / MLSTM_SOFTCAP)

    incl, _ = _pair_masks(c)
    ones = jnp.ones((2 * c, LANES), f32)
    pairs = range(MLSTM_HEADS // 2)

    def full(x):
        return jnp.broadcast_to(x, (2 * c, LANES))

    def col(a, lane):
        return a[:, lane : lane + 1]

    def row(a, lane):
        return a[lane : lane + 1, :]

    def chain(bi, s):
        rs = slice(s * c, (s + 1) * c)
        capped = cap(sm_ref[bi, rs, :] + par_ref[0:1, :])
        log_i = capped
        bcum = _cumsum_rows(-_softplus(-capped))
        log_i_t = log_i.T
        bcum_t = bcum.T
        qk = qk_ref[bi, rs, :].astype(f32)
        v_all = v_ref[bi, rs, :].astype(f32)
        og = og_ref[bi, rs, :].astype(f32)
        q_st, k_st, v_st, bc, li, b_last, dmat, m_intra = [], [], [], [], [], [], [], []
        for p in pairs:
            h0, h1 = 2 * p, 2 * p + 1
            q_st.append(_stack2(qk[:, p * 128 : (p + 1) * 128]))
            k_st.append(_stack2(qk[:, 256 + p * 128 : 256 + (p + 1) * 128] * MLSTM_DQK**-0.5))
            v_st.append(_rows(v_all[:, h0 * 128 : (h0 + 1) * 128], v_all[:, h1 * 128 : (h1 + 1) * 128]))
            bc.append(full(_rows(col(bcum, SM_ML_F + h0), col(bcum, SM_ML_F + h1))))
            li.append(full(_rows(col(log_i, SM_ML_I + h0), col(log_i, SM_ML_I + h1))))
            b_row = _lanes(row(bcum_t, SM_ML_F + h0), row(bcum_t, SM_ML_F + h1))
            li_row = _lanes(row(log_i_t, SM_ML_I + h0), row(log_i_t, SM_ML_I + h1))
            b_last.append(full(_half_col(c, bcum[c - 1 : c, SM_ML_F + h0 : SM_ML_F + h0 + 1],
                                         bcum[c - 1 : c, SM_ML_F + h1 : SM_ML_F + h1 + 1])))
            log_d = jnp.where(incl, bc[p] - b_row + li_row, -jnp.inf)
            m_intra.append(full(jnp.max(log_d, -1, keepdims=True)))
            dmat.append(jnp.exp(log_d - m_intra[p]))

        qk_t = yield [functools.partial(_dot_nt, q_st[p], k_st[p]) for p in pairs]
        pm = [dmat[p] * qk_t[p] for p in pairs]
        v_ext = [_lanes(v_st[p], ones) for p in pairs]
        intra = yield [functools.partial(_dot, pm[p], v_ext[p]) for p in pairs]
        for _ in range(s):
            yield []
        cs_old = [cs_ref[bi, p] for p in pairs]
        m_old = [_half_col(c, m_ref[bi, 2 * p : 2 * p + 1, :], m_ref[bi, 2 * p + 1 : 2 * p + 2, :]) for p in pairs]
        kw_st, s_old, m_new = [], [], []
        for p in pairs:
            log_e = b_last[p] - bc[p] + li[p]
            m_end = _half_col(c, jnp.max(log_e[:c], 0, keepdims=True), jnp.max(log_e[c:], 0, keepdims=True))
            m_new.append(jnp.maximum(b_last[p] + m_old[p], m_end))
            s_old.append(jnp.exp(b_last[p] + m_old[p] - m_new[p]))
            kw_st.append(k_st[p] * (jnp.exp(log_e - m_end) * jnp.exp(m_end - m_new[p])))
        prods = yield ([functools.partial(_dot, q_st[p], cs_old[p]) for p in pairs]
                       + [functools.partial(_dot_tn, kw_st[p], v_ext[p]) for p in pairs])
        q_cs, cs_add = prods[: len(pairs)], prods[len(pairs):]
        for p in pairs:
            m_t = jnp.maximum(bc[p] + m_old[p], m_intra[p])
            s_inter = jnp.exp(bc[p] + m_old[p] - m_t)
            s_intra = jnp.exp(m_intra[p] - m_t)
            num = s_inter * q_cs[p][:, :LANES] + s_intra * intra[p][:, :LANES]
            den = s_inter * q_cs[p][:, LANES:] + s_intra * intra[p][:, LANES:]
            hc = num / jnp.maximum(jnp.abs(den), jnp.exp(-m_t))
            cs_ref[bi, p] = _lanes(s_old[p], s_old[p]) * cs_old[p] + cs_add[p]
            for e in range(2):
                h = 2 * p + e
                m_ref[bi, h : h + 1, :] = m_new[p][e * c : e * c + 1]
                hs = hc[e * c : (e + 1) * c]
                hs = hs * lax.rsqrt(jnp.mean(hs * hs, -1, keepdims=True) + 1e-6) * ng_ref[:, h * 128 : (h + 1) * 128]
                o_ref[bi, rs, h * 128 : (h + 1) * 128] = hs * _sigmoid(og[:, h * 128 : (h + 1) * 128])

    _lockstep(*(chain(bi, s) for s in range(step_rows // c) for bi in range(nb)))


def _mlstm(z, zm, par, norm_g, b, t):
    c, rows = CHUNK, STEP_CHUNKS * CHUNK
    z3 =z.reshape(b, t, z.shape[1])
    zm3 = zm.reshape(b, t, zm.shape[1])
    out = pl.pallas_call(
        _mlstm_body,
        grid=(t // rows,),
        in_specs=[
            _seq_spec(b, rows, 512, Z_MLSTM // 512),
            _seq_spec(b, rows, 512, (Z_MLSTM + 512) // 512),
            _seq_spec(b, rows, 512, (Z_MLSTM + 1024) // 512),
            _seq_spec(b, rows, 128, MISC_SMALLS // 128),
            _bcast_spec((8, 128)),
            _bcast_spec((1, 512)),
        ],
        out_specs=_seq_spec(b, rows, 512, 0),
        out_shape=jax.ShapeDtypeStruct((b, t, 512), f32),
        scratch_shapes=[
            pltpu.VMEM((b, MLSTM_HEADS // 2, 128, 256), f32),
            pltpu.VMEM((b, 8, 128), f32),
        ],
        compiler_params=_cparams("arbitrary"),
        name="mlstm",
    )(z3, z3, z3, zm3, par, norm_g)
    return out.reshape(b * t, 512)


def _rwkv_body(*refs, has_vres):
    if has_vres:
        (r_ref, k_ref, v_ref, misc_ref, vf_ref, mu_ref, mum_ref, vecs_ref, w2_ref, a2_ref, g2_ref, v2_ref,
         y_ref, st_ref, tail_ref, tailm_ref) = refs
    else:
        (r_ref, k_ref, v_ref, misc_ref, mu_ref, mum_ref, vecs_ref, w2_ref, a2_ref, g2_ref,
         y_ref, vf_out_ref, st_ref, tail_ref, tailm_ref) = refs
    nb, step_rows, _ = r_ref.shape
    c = CHUNK

    @pl.when(pl.program_id(0) == 0)
    def _():
        st_ref[...] = jnp.zeros(st_ref.shape, f32)
        tail_ref[:, 0:8, :] = jnp.zeros((nb, 8, tail_ref.shape[2]), f32)
        tailm_ref[:, 0:8, :] = jnp.zeros((nb, 8, tailm_ref.shape[2]), f32)

    incl, strict = _pair_masks(c)
    same_head = (_iota2((LANES, LANES), 0) < HALF) == (_iota2((LANES, LANES), 1) < HALF)
    pairs = range(RWKV_HEADS // 2)
    sl = [slice(p * 128, (p + 1) * 128) for p in pairs]
    c2 = 2 * c
    w0, a0, k_k, k_a, r_k, lnx_g, lnx_b, v0 = (vecs_ref[i : i + 1, :] for i in range(8))

    def chain(bi, s):
        rs = slice(s * c, (s + 1) * c)
        raw = jnp.concatenate([r_ref[bi, rs, :], k_ref[bi, rs, :], v_ref[bi, rs, :]], axis=1).astype(f32)
        tail_ref[bi, 8 : 8 + c, :] = raw
        prev = tail_ref[bi, 7 : 7 + c, :]
        tail_ref[bi, 0:8, :] = tail_ref[bi, c : c + 8, :]
        rkv = raw + mu_ref[...] * (prev - raw)
        rawm = misc_ref[bi, rs, :]
        tailm_ref[bi, 8 : 8 + c, :] = rawm
        prevm = tailm_ref[bi, 7 : 7 + c, :]
        tailm_ref[bi, 0:8, :] = tailm_ref[bi, c : c + 8, :]
        misc = rawm + mum_ref[...] * (prevm - rawm)

        r = rkv[:, 0:512]
        k = rkv[:, 512:1024]
        v = rkv[:, 1024:1536]
        wa = misc[:, 0:128]
        lora = [functools.partial(_dot, jnp.tanh(wa), w2_ref[...]), functools.partial(_dot, wa, a2_ref[...]),
                functools.partial(_dot, _sigmoid(misc[:, 128:256]), g2_ref[...])]
        if has_vres:
            lora.append(functools.partial(_dot, misc[:, 256:384], v2_ref[...]))
        lora = yield lora
        log_w = -jnp.exp(-_softplus(-(w0 + lora[0])) - 0.5)
        a = _sigmoid(a0 + lora[1])
        g = lora[2]
        if has_vres:
            v = v + (vf_ref[bi, rs, :] - v) * _sigmoid(v0 + lora[3])
        else:
            vf_out_ref[bi, rs, :] = v
        kk = k * k_k
        kk = kk * lax.rsqrt(_head64_sums(kk * kk) + 1e-6)
        k = k * (1.0 + (a - 1.0) * k_a)
        alpha = -kk
        beta = kk * a
        lw = _cumsum_rows(log_w)
        lw_last = lw[c - 1 : c, :]
        inv = jnp.exp(-lw)
        al_bar = alpha * jnp.exp(lw - log_w)
        be_hat = beta * inv
        k_hat = k * inv
        r_bar = r * jnp.exp(lw)
        to_end = jnp.exp(lw_last - lw)
        be_end = beta * to_end
        k_end = k * to_end
        g_chunk = jnp.exp(lw_last)

        al_st = [_stack2(al_bar[:, s]) for s in sl]
        v_st = [_stack2(v[:, s]) for s in sl]
        m1 = yield [functools.partial(_dot_nt, _rows(al_st[p], _stack2(r_bar[:, s])),
                                      _rows(_stack2(be_hat[:, s]), _stack2(k_hat[:, s]))) for p, s in zip(pairs, sl)]
        t_inv = yield from _unit_lower_inverses_staged([jnp.where(strict, m[:c2, :c2], 0.0) for m in m1], c)
        l_ak = [jnp.where(strict, m[:c2, c2:], 0.0) for m in m1]
        q_rb = [jnp.where(incl, m[c2:, :c2], 0.0) for m in m1]
        q_rk = [jnp.where(incl, m[c2:, c2:], 0.0) for m in m1]
        m2 = yield [functools.partial(_dot, _rows(l_ak[p], q_rk[p]), v_st[p]) for p in pairs]
        m3 = yield [functools.partial(_dot, t_inv[p], _lanes(al_st[p], m2[p][:c2])) for p in pairs]
        for _ in range(2 * s):
            yield []
        st_old = [st_ref[bi, p] for p in pairs]
        m4 = yield [functools.partial(_dot_nt, _rows(_unstack2(m3[p][:, :128]), r_bar[:, sl[p]]), st_old[p]) for p in pairs]
        u = [m4[p][:c] + _unstack2(m3[p][:, 128:]) for p in pairs]
        m56 = yield ([functools.partial(_dot, q_rb[p], _stack2(u[p])) for p in pairs]
                     + [functools.partial(_dot_tn, _rows(u[p], v[:, sl[p]]), _rows(be_end[:, sl[p]], k_end[:, sl[p]]))
                        for p in pairs])
        m5, m6 = m56[: len(pairs)], m56[len(pairs):]
        ys = []
        for p in pairs:
            st_ref[bi, p] = g_chunk[:, sl[p]] * st_old[p] + jnp.where(same_head, m6[p], 0.0)
            ys.append(m4[p][c:] + _unstack2(m5[p]) + _unstack2(m2[p][c2:]))
        y = jnp.concatenate(ys, axis=1)
        mean = _head64_sums(y) * (1.0 / RWKV_DH)
        d = y - mean
        var = _head64_sums(d * d) * (1.0 / RWKV_DH)
        y = d * lax.rsqrt(var + RWKV_GN_EPS) * lnx_g + lnx_b
        bonus = _head64_sums(r * k * r_k) * v
        y_ref[bi, rs, :] = (y + bonus) * g

    _lockstep(*(chain(bi, s) for s in range(step_rows // c) for bi in range(nb)))


def _rwkv(z, zm, v_first, mu, mu_misc, vecs, w2p, a2p, g2, v2p, b, t):
    c, rows = CHUNK, STEP_CHUNKS * CHUNK
    has_vres = v_first is not None
    z3 = z.reshape(b, t, z.shape[1])
    blk = _seq_spec(b, rows, 512, 0)
    in_specs = [_seq_spec(b, rows, 512, (Z_RWKV + 512 * i) // 512) for i in range(3)] + [blk]
    args = [z3, z3, z3, zm.reshape(b, t, zm.shape[1])]
    if has_vres:
        in_specs.append(blk)
        args.append(v_first.reshape(b, t, 512))
    in_specs += [_bcast_spec(s) for s in ((1, 1536), (1, 512), (8, 512), (128, 512), (128, 512), (128, 512))]
    args += [mu, mu_misc, vecs, w2p, a2p, g2]
    if has_vres:
        in_specs.append(_bcast_spec((128, 512)))
        args.append(v2p)
    out_sd = jax.ShapeDtypeStruct((b, t, 512), f32)
    res = pl.pallas_call(
        functools.partial(_rwkv_body, has_vres=has_vres),
        grid=(t // rows,),
        in_specs=in_specs,
        out_specs=blk if has_vres else [blk, blk],
        out_shape=out_sd if has_vres else [out_sd, out_sd],
        scratch_shapes=[
            pltpu.VMEM((b, RWKV_HEADS // 2, 128, 128), f32),
            pltpu.VMEM((b, c + 8, 1536), f32),
            pltpu.VMEM((b, c + 8, 512), f32),
        ],
        compiler_params=_cparams("arbitrary"),
        name="rwkv_res" if has_vres else "rwkv",
    )(*args)
    if has_vres:
        return res.reshape(b * t, 512), v_first
    return res[0].reshape(b * t, 512), res[1].reshape(b * t, 512)


def _split3(x):
    hi = x.astype(bf16)
    r1 = x - hi.astype(f32)
    mid = r1.astype(bf16)
    return hi, mid, (r1 - mid.astype(f32)).astype(bf16)


def _router_pack(w):
    return _pad_cols(jnp.concatenate(_split3(w.astype(f32)), axis=1), LANES)


def _route(h, w3, b):
    h_hi, h_mid, h_lo = _split3(h)
    a = _dot(h_hi, w3)
    m = _dot(h_mid, w3)
    lo = _dot(h_lo, w3)
    down = lambda x, groups: pltpu.roll(x, LANES - groups * N_EXPERTS, 1)
    logits = a + down(a, 1) + down(a, 2) + m + down(m, 1) + lo + b
    lane = _iota2(logits.shape, 1)
    logits = jnp.where(lane < N_EXPERTS, logits, -jnp.inf)
    m1 = jnp.max(logits, -1, keepdims=True)
    i1 = jnp.min(jnp.where(logits == m1, lane, LANES), -1, keepdims=True)
    rest = jnp.where(lane == i1, -jnp.inf, logits)
    m2 = jnp.max(rest, -1, keepdims=True)
    i2 = jnp.min(jnp.where(rest == m2, lane, LANES), -1, keepdims=True)
    e = jnp.exp(m2 - m1)
    g1 = 1.0 / (1.0 + e)
    g2 = e / (1.0 + e)
    return (jnp.where(lane == 0, i1, jnp.where(lane == 1, i2, 0)),
            jnp.where(lane == 0, g1, jnp.where(lane == 1, g2, 0.0)))


def _merge_body(*refs, with_router):
    (ya_ref, yb_ref, yc_ref, ga_ref, gb_ref, gc_ref, x_ref, wb_ref, wo_ref, gt_ref, g_ref, b_ref,
     sc_ref, sh_ref) = refs[:14]
    merged = _sigmoid(ga_ref[...].astype(f32)) * _dot(ya_ref[...], wb_ref[0])
    merged = merged + _sigmoid(gb_ref[...].astype(f32)) * _dot(yb_ref[...], wb_ref[1])
    merged = merged + _sigmoid(gc_ref[...].astype(f32)) * _dot(yc_ref[...], wb_ref[2])
    y = _dot(merged, wo_ref[...])
    xn = _ln_rows(DEEPNORM_ALPHA * x_ref[...] + (1.0 + gt_ref[...]) * y, g_ref[...], b_ref[...])
    h = xn * (1.0 + sc_ref[...]) + sh_ref[...]
    if with_router:
        rw_ref, rb_ref, xn_ref, h_ref, idx_ref, gate_ref = refs[14:]
        idx_ref[...], gate_ref[...] = _route(h, rw_ref[...], rb_ref[...])
    else:
        xn_ref, h_ref = refs[14:]
    xn_ref[...] = xn
    h_ref[...] = h.astype(h_ref.dtype)


def _merge(ya, yb, yc, z, x, wb, wo, mod, ln_g, ln_b, t, h_dtype, router=None):
    n = x.shape[0]
    tm = 512
    gate = lambda i: pl.BlockSpec((tm, 1024), lambda r: (r, Z_GATE // 1024 + i))
    in_specs = [
        _row_spec(tm, 512), _row_spec(tm, 512), _row_spec(tm, 512), gate(0), gate(1), gate(2), _row_spec(tm),
        pl.BlockSpec((3, 512, D_MODEL), lambda r: (0, 0, 0)),
        pl.BlockSpec((D_MODEL, D_MODEL), lambda r: (0, 0)),
        _mod_spec(t, tm, 2), _vec_spec(), _vec_spec(), _mod_spec(t, tm, 4), _mod_spec(t, tm, 3),
    ]
    args = [ya, yb, yc, z, z, z, x, wb, wo, mod, ln_g, ln_b, mod, mod]
    out_specs = [_row_spec(tm), _row_spec(tm)]
    out_shape = [jax.ShapeDtypeStruct((n, D_MODEL), f32), jax.ShapeDtypeStruct((n, D_MODEL), h_dtype)]
    if router is not None:
        in_specs += [pl.BlockSpec((D_MODEL, LANES), lambda r: (0, 0)), _vec_spec(LANES)]
        args += list(router)
        out_specs += [_row_spec(tm, LANES), _row_spec(tm, LANES)]
        out_shape += [jax.ShapeDtypeStruct((n, LANES), i32), jax.ShapeDtypeStruct((n, LANES), f32)]
    return pl.pallas_call(
        functools.partial(_merge_body, with_router=router is not None),
        grid=(n // tm,),
        in_specs=in_specs,
        out_specs=out_specs,
        out_shape=out_shape,
        compiler_params=_cparams("arbitrary"),
        name="merge" if router is None else "merge_route",
    )(*args)


def _ffn_body(h_ref, x_ref, wg_ref, wu_ref, wd_ref, gt_ref, g_ref, b_ref, sc_ref, sh_ref, xn_ref, hn_ref, acc_ref):
    f = pl.program_id(1)

    @pl.when(f == 0)
    def _():
        acc_ref[...] = jnp.zeros(acc_ref.shape, f32)

    h = h_ref[...]
    act = _silu(_dot(h, wg_ref[...])) * _dot(h, wu_ref[...])
    acc_ref[...] += _dot(act, wd_ref[...])

    @pl.when(f == pl.num_programs(1) - 1)
    def _():
        xn = _ln_rows(DEEPNORM_ALPHA * x_ref[...] + (1.0 + gt_ref[...]) * acc_ref[...], g_ref[...], b_ref[...])
        xn_ref[...] = xn
        hn_ref[...] = (xn * (1.0 + sc_ref[...]) + sh_ref[...]).astype(hn_ref.dtype)


def _ffn(h, x, wg, wu, wd, mod, mod_next, ln_g, ln_b, t):
    n = x.shape[0]
    dff = wg.shape[1]
    tm, tf = 512, dff // 2
    row = pl.BlockSpec((tm, D_MODEL), lambda i, f: (i, 0))
    vec = pl.BlockSpec((1, D_MODEL), lambda i, f: (0, 0))
    modspec = lambda slot: pl.BlockSpec((None, 1, D_MODEL), lambda i, f: ((i * tm // t) * 6 + slot, 0, 0))
    return pl.pallas_call(
        _ffn_body,
        grid=(n // tm, dff // tf),
        in_specs=[
            row, row,
            pl.BlockSpec((D_MODEL, tf), lambda i, f: (0, f)),
            pl.BlockSpec((D_MODEL, tf), lambda i, f: (0, f)),
            pl.BlockSpec((tf, D_MODEL), lambda i, f: (f, 0)),
            modspec(5), vec, vec, modspec(1), modspec(0),
        ],
        out_specs=[row, row],
        out_shape=[jax.ShapeDtypeStruct((n, D_MODEL), f32), jax.ShapeDtypeStruct((n, D_MODEL), bf16)],
        scratch_shapes=[pltpu.VMEM((tm, D_MODEL), f32)],
        compiler_params=_cparams("arbitrary", "arbitrary"),
        name="ffn",
    )(h, x, wg, wu, wd, mod, ln_g, ln_b, mod_next, mod_next)


def _moe_body(be_ref, live_ref, tok_ref, tokn_ref, dst_ref, gate_ref, h_hbm, wg_ref, wu_ref, wd_ref, y_hbm,
              xbuf, xb16, acc_ref, obuf, gsem, ssem):
    j = pl.program_id(0)
    f = pl.program_id(1)
    nb = pl.num_programs(0)
    nf = pl.num_programs(1)
    rows = xbuf.shape[1]
    slot = j % 2
    live = live_ref[j] > 0
    next_live = live_ref[jnp.minimum(j + 1, nb - 1)] > 0

    def gather(ref, dst_slot, static):
        def start(r, carry):
            pltpu.make_async_copy(h_hbm.at[pl.ds(ref[0, 0, r], 1), :], xbuf.at[dst_slot, pl.ds(r, 1), :],
                                  gsem.at[dst_slot]).start()
            return carry

        if static:
            for r in range(rows):
                start(r, 0)
        else:
            lax.fori_loop(0, rows, start, 0, unroll=8)

    def gather_wait(src_slot):
        pltpu.make_async_copy(h_hbm.at[pl.ds(0, rows), :], xbuf.at[src_slot], gsem.at[src_slot]).wait()

    def scatter_wait(src_slot):
        pltpu.make_async_copy(obuf.at[src_slot], y_hbm.at[pl.ds(0, rows), :], ssem.at[src_slot]).wait()

    @pl.when(jnp.logical_and(f == 0, j == 0))
    def _():
        obuf[1] = jnp.zeros(obuf.shape[1:], f32)
        spare = pltpu.make_async_copy(obuf.at[1], y_hbm.at[pl.ds(y_hbm.shape[0] - rows, rows), :], ssem.at[1])
        spare.start()
        spare.wait()

    @pl.when(jnp.logical_and(f == 0, jnp.logical_and(j == 0, live)))
    def _():
        gather(tok_ref, 0, static=False)

    @pl.when(jnp.logical_and(f == 0, live))
    def _():
        gather_wait(slot)
        xb16[...] = xbuf[slot].astype(bf16)
        acc_ref[...] = jnp.zeros(acc_ref.shape, f32)

    @pl.when(jnp.logical_and(f == 0, jnp.logical_and(j + 1 < nb, next_live)))
    def _():
        gather(tokn_ref, 1 - slot, static=True)

    @pl.when(live)
    def _():
        x = xb16[...]
        act = _silu(_dot(x, wg_ref[...])) * _dot(x, wu_ref[...])
        acc_ref[...] += _dot(act, wd_ref[...])

    @pl.when(jnp.logical_and(f == nf - 1, live))
    def _():
        obuf[slot] = acc_ref[...] * gate_ref[...]

        for r in range(rows):
            pltpu.make_async_copy(obuf.at[slot, pl.ds(r, 1), :], y_hbm.at[pl.ds(dst_ref[0, 0, r], 1), :],
                                  ssem.at[slot]).start()

    @pl.when(jnp.logical_and(f == nf - 1, jnp.logical_and(j > 0, live_ref[jnp.maximum(j - 1, 0)] > 0)))
    def _():
        scatter_wait(1 - slot)

    @pl.when(jnp.logical_and(f == nf - 1, jnp.logical_and(j == nb - 1, live)))
    def _():
        scatter_wait(slot)


def _moe(h, block_e, block_live, row_tok, row_dst, row_gate, wg, wu, wd):
    n = h.shape[0]
    n_blocks = block_e.shape[0]
    rb = MOE_ROWS
    dff = wg.shape[2]
    tf = 512
    smem_blk = pl.BlockSpec((1, 1, rb), lambda j, f, be, lv: (j, 0, 0), memory_space=pltpu.SMEM)
    smem_next = pl.BlockSpec((1, 1, rb), lambda j, f, be, lv: (jnp.minimum(j + 1, n_blocks - 1), 0, 0),
                             memory_space=pltpu.SMEM)
    grid_spec = pltpu.PrefetchScalarGridSpec(
        num_scalar_prefetch=2,
        grid=(n_blocks, dff // tf),
        in_specs=[
            smem_blk,
            smem_next,
            smem_blk,
            pl.BlockSpec((rb, 1), lambda j, f, be, lv: (j, 0)),
            pl.BlockSpec(memory_space=pl.ANY),
            pl.BlockSpec((None, D_MODEL, tf), lambda j, f, be, lv: (be[j], 0, f)),
            pl.BlockSpec((None, D_MODEL, tf), lambda j, f, be, lv: (be[j], 0, f)),
            pl.BlockSpec((None, tf, D_MODEL), lambda j, f, be, lv: (be[j], f, 0)),
        ],
        out_specs=pl.BlockSpec(memory_space=pl.ANY),
        scratch_shapes=[
            pltpu.VMEM((2, rb, D_MODEL), f32),
            pltpu.VMEM((rb, D_MODEL), bf16),
            pltpu.VMEM((rb, D_MODEL), f32),
            pltpu.VMEM((2, rb, D_MODEL), f32),
            pltpu.SemaphoreType.DMA((2,)),
            pltpu.SemaphoreType.DMA((2,)),
        ],
    )
    return pl.pallas_call(
        _moe_body,
        grid_spec=grid_spec,
        out_shape=jax.ShapeDtypeStruct((TOP_K * n + rb, D_MODEL), f32),
        compiler_params=_cparams("arbitrary", "arbitrary"),
        name="moe",
    )(block_e, block_live, row_tok, row_tok, row_dst, row_gate, h, wg, wu, wd)


def _routing_tables(top_e, top_g, n):
    rb = MOE_ROWS
    n_assign = n * TOP_K
    n_blocks = -(-n_assign // rb) + N_EXPERTS
    rows = n_blocks * rb
    e_flat = top_e.reshape(-1)
    onehot = (e_flat[:, None] == jnp.arange(N_EXPERTS, dtype=i32)[None, :]).astype(i32)
    rank = jnp.sum((jnp.cumsum(onehot, axis=0) - onehot) * onehot, axis=1)
    counts = jnp.sum(onehot, axis=0)
    padded = (counts + rb - 1) // rb * rb
    pad_end = jnp.cumsum(padded)
    pad_start = pad_end - padded
    dest = jnp.sum(onehot * pad_start[None, :], axis=1) + rank
    row_a = jnp.full((rows,), n_assign, i32).at[dest].set(jnp.arange(n_assign, dtype=i32))
    valid = row_a < n_assign
    tok, slot_k = row_a // TOP_K, row_a % TOP_K
    row_tok = jnp.where(valid, tok, 0)
    row_dst = jnp.where(valid, slot_k * n + tok, n_assign + jnp.arange(rows, dtype=i32) % rb)
    row_gate = jnp.where(valid, top_g.reshape(-1)[jnp.minimum(row_a, n_assign - 1)], 0.0)
    starts = jnp.arange(n_blocks, dtype=i32) * rb
    block_e = jnp.minimum(jnp.sum((pad_end[None, :] <= starts[:, None]).astype(i32), axis=1), N_EXPERTS - 1)
    block_live = (starts < pad_end[-1]).astype(i32)
    return (block_e, block_live, row_tok.reshape(n_blocks, 1, rb), row_dst.reshape(n_blocks, 1, rb),
            row_gate.reshape(rows, 1))


def _final_body(x_ref, y0_ref, y1_ref, gt_ref, g_ref, b_ref, o_ref):
    y = y0_ref[...] + y1_ref[...]
    o_ref[...] = _ln_rows(DEEPNORM_ALPHA * x_ref[...] + (1.0 + gt_ref[...]) * y, g_ref[...], b_ref[...])


def _final_ln(x, y2, mod, ln_g, ln_b, t):
    n = x.shape[0]
    tm = 512
    nb = n // tm
    return pl.pallas_call(
        _final_body,
        grid=(nb,),
        in_specs=[_row_spec(tm), _row_spec(tm), pl.BlockSpec((tm, D_MODEL), lambda i: (i + nb, 0)),
                  _mod_spec(t, tm, 5), _vec_spec(), _vec_spec()],
        out_specs=_row_spec(tm),
        out_shape=jax.ShapeDtypeStruct((n, D_MODEL), f32),
        compiler_params=_cparams("arbitrary"),
        name="final_ln",
    )(x, y2, y2, mod, ln_g, ln_b)


def _pad_cols(w, width):
    return jnp.pad(w, ((0, 0), (0, width - w.shape[1])))


def _pad_rows(w, height):
    return jnp.pad(w, ((0, height - w.shape[0]), (0, 0)))


def _regroup_w_in(w_in, v1):
    d = w_in.shape[0]
    gdn, ml, rw = 0, 2056, 3600
    smalls = jnp.concatenate([w_in[:, gdn + 2048 : gdn + 2056], w_in[:, ml + 1536 : ml + 1544]], axis=1)
    vdown = jnp.zeros((d, LORA_V), w_in.dtype) if v1 is None else v1
    misc = jnp.concatenate([w_in[:, rw + 1536 : rw + 1792], _pad_cols(vdown, 128), _pad_cols(smalls, 128)], axis=1)
    cat = jnp.concatenate(
        [w_in[:, gdn : gdn + 2048], w_in[:, ml : ml + 1536], w_in[:, rw : rw + 1536], w_in[:, 5392:8464]], axis=1)
    assert cat.shape[1] == Z_WIDTH and misc.shape[1] == MISC_WIDTH
    return cat.astype(bf16), misc.astype(bf16)


def _row(v, width=None):
    v = v.reshape(1, -1).astype(f32)
    return v if width is None else _pad_cols(v, width)


def kernel(x, c, ln_in_g, ln_in_b, ada_w, ada_b, w_in, gdn_conv, gdn_a_log, gdn_dt_bias, gdn_norm_g, mlstm_b_i, mlstm_b_f, mlstm_norm_g, rwkv_mu, rwkv_w0, rwkv_w2, rwkv_a0, rwkv_a2, rwkv_g2, rwkv_k_k, rwkv_k_a, rwkv_r_k, rwkv_lnx_g, rwkv_lnx_b, rwkv_v1, rwkv_mu_v1, rwkv_v0, rwkv_v2, w_branch, w_o, ln1_g, ln1_b, ln2_g, ln2_b, ffn_w_gate, ffn_w_up, ffn_w_down, moe_router, moe_router_b, moe_w_gate, moe_w_up, moe_w_down):
    b, t, d = x.shape
    n = b * t
    depth = w_in.shape[0]
    xf = x.reshape(n, d)

    c_pad = _pad_rows(c.astype(f32), 8)
    mods = []
    for layer in range(depth):
        mod = _ada_mod(c_pad, ada_w[layer].astype(bf16), _row(ada_b[layer]))[:b]
        mods.append(mod.reshape(b * 6, 1, d))

    xcur, h = _entry_ln(xf, _row(ln_in_g), _row(ln_in_b), mods[0], t)
    v_first = None
    out = None
    for layer in range(depth):
        mod = mods[layer]
        has_vres = layer > 0
        w_main, w_misc = _regroup_w_in(w_in[layer], rwkv_v1[layer - 1] if has_vres else None)
        z = _matmul(h, w_main, 1024, 1024, bf16, "in_proj")
        zm = _matmul(h, w_misc, 1024, MISC_WIDTH, f32, "in_proj_misc")

        gdn_par = jnp.zeros((8, LANES), f32).at[0, 0:4].set(gdn_a_log[layer]).at[1, 0:4].set(gdn_dt_bias[layer])
        y_a = _gdn(z, zm, gdn_conv[layer].astype(f32), gdn_par, _row(gdn_norm_g[layer]), b, t)

        ml_par = jnp.zeros((8, LANES), f32).at[0, SM_ML_I : SM_ML_I + 4].set(mlstm_b_i[layer])
        ml_par = ml_par.at[0, SM_ML_F : SM_ML_F + 4].set(mlstm_b_f[layer])
        y_b = _mlstm(z, zm, ml_par, _row(mlstm_norm_g[layer]), b, t)

        mu = rwkv_mu[layer].astype(f32)
        mu_v = rwkv_mu_v1[layer - 1].astype(f32) if has_vres else jnp.zeros((LORA_V,), f32)
        mu_misc = _row(jnp.concatenate([mu[1536:1792], mu_v]), 512)
        vecs = jnp.stack([
            rwkv_w0[layer], rwkv_a0[layer], rwkv_k_k[layer], rwkv_k_a[layer], rwkv_r_k[layer].reshape(-1),
            rwkv_lnx_g[layer], rwkv_lnx_b[layer],
            rwkv_v0[layer - 1] if has_vres else jnp.zeros((RWKV_WIDTH,), f32),
        ]).astype(f32)
        w2p = _pad_rows(rwkv_w2[layer], 128).astype(bf16)
        a2p = jnp.concatenate([jnp.zeros((LORA_W, RWKV_WIDTH), f32), rwkv_a2[layer]], axis=0).astype(bf16)
        v2p = _pad_rows(rwkv_v2[layer - 1], 128).astype(bf16) if has_vres else None
        y_c, v_first = _rwkv(z, zm, v_first, _row(mu[:1536]), mu_misc, vecs, w2p, a2p, rwkv_g2[layer].astype(bf16), v2p, b, t)

        moe_layer = layer % 2 == 1
        i = layer // 2
        router = (_router_pack(moe_router[i]), _row(moe_router_b[i], LANES)) if moe_layer else None
        merged = _merge(y_a, y_b, y_c, z, xcur, w_branch[layer].astype(bf16), w_o[layer].astype(bf16), mod,
                        _row(ln1_g[layer]), _row(ln1_b[layer]), t, f32 if moe_layer else bf16, router)
        xcur, h2 = merged[:2]
        if not moe_layer:
            mod_next = mods[layer + 1] if layer + 1 < depth else mod
            xcur, h = _ffn(h2, xcur, ffn_w_gate[i].astype(bf16), ffn_w_up[i].astype(bf16), ffn_w_down[i].astype(bf16),
                           mod, mod_next, _row(ln2_g[layer]), _row(ln2_b[layer]), t)
            out = xcur
        else:
            idx, gate = merged[2:]
            tables = _routing_tables(idx[:, :TOP_K], gate[:, :TOP_K], n)
            y2 = _moe(h2, *tables, moe_w_gate[i], moe_w_up[i], moe_w_down[i])
            out = _final_ln(xcur, y2, mod, _row(ln2_g[layer]), _row(ln2_b[layer]), t)
            xcur = out
    return out.reshape(b, t, d)
```

```python
import functools
import math

import jax
import jax.numpy as jnp
from jax import lax
from jax.experimental import pallas as pl
from jax.experimental.pallas import tpu as pltpu

f32 = jnp.float32
bf16 = jnp.bfloat16
i32 = jnp.int32

D_MODEL = 1024
DEPTH = 2
CHUNK = 64
GDN_HEADS = 4
GDN_DK = 128
MLSTM_HEADS = 4
MLSTM_DQK = 64
MLSTM_DV = 128
MLSTM_SOFTCAP = 15.0
RWKV_HEADS = 8
RWKV_DH = 64
RWKV_WIDTH = RWKV_HEADS * RWKV_DH
LORA_W = 64
LORA_A = 64
LORA_V = 32
LORA_G = 128
N_EXPERTS = 8
TOP_K = 2
DEEPNORM_ALPHA = (2 * DEPTH) ** 0.25
LN_EPS = 1e-5
RWKV_GN_EPS = 64e-5

LANES = 128
HALF = LANES // 2

Z_GDN = 0
Z_MLSTM = 2048
Z_RWKV = 3584
Z_GATE = 5120
Z_WIDTH = 8192
MISC_WIDTH = 512
MISC_SMALLS = 384
SM_GDN_A, SM_GDN_B, SM_ML_I, SM_ML_F = 0, 4, 8, 12

STEP_CHUNKS = 4
MOE_ROWS = 1024
VMEM_LIMIT = 56 * 1024 * 1024


def _cparams(*sem):
    return pltpu.CompilerParams(dimension_semantics=sem, vmem_limit_bytes=VMEM_LIMIT)


def _dot(a, b):
    return jnp.dot(a.astype(bf16), b.astype(bf16), preferred_element_type=f32)


def _dot_nt(a, b):
    return lax.dot_general(a.astype(bf16), b.astype(bf16), (((1,), (1,)), ((), ())), preferred_element_type=f32)


def _dot_tn(a, b):
    return lax.dot_general(a.astype(bf16), b.astype(bf16), (((0,), (0,)), ((), ())), preferred_element_type=f32)


def _sigmoid(x):
    return 1.0 / (1.0 + jnp.exp(-x))


def _silu(x):
    return x * _sigmoid(x)


def _softplus(x):
    return jnp.maximum(x, 0.0) + jnp.log1p(jnp.exp(-jnp.abs(x)))


def _iota2(shape, axis):
    return lax.broadcasted_iota(i32, shape, axis)


def _cumsum_rows(x):
    n = x.shape[0]
    rows = _iota2(x.shape, 0)
    sh = 1
    while sh < n:
        x = x + jnp.where(rows >= sh, pltpu.roll(x, sh, 0), 0.0)
        sh *= 2
    return x


def _unit_lower_inverses(ps, nilpotent):
    n = ps[0].shape[0]
    eye = (_iota2((n, n), 0) == _iota2((n, n), 1)).astype(f32)
    ts = [eye + p for p in ps]
    for _ in range(int(math.log2(nilpotent)) - 1):
        ps = [_dot(p, p) for p in ps]
        ts = [t + _dot(t, p) for t, p in zip(ts, ps)]
    return ts


def _rows(*xs):
    return jnp.concatenate(xs, axis=0)


def _lanes(*xs):
    return jnp.concatenate(xs, axis=1)


def _half_masks(shape):
    lane = _iota2(shape, len(shape) - 1)
    return lane < HALF, lane >= HALF


def _stack2(x):
    lo, _ = _half_masks(x.shape)
    return _rows(jnp.where(lo, x, 0.0), jnp.where(lo, 0.0, x))


def _unstack2(x):
    c = x.shape[0] // 2
    return x[:c] + x[c:]


def _pair_masks(c):
    r = _iota2((2 * c, 2 * c), 0)
    q = _iota2((2 * c, 2 * c), 1)
    same = (r < c) == (q < c)
    rr = jnp.where(r < c, r, r - c)
    qq = jnp.where(q < c, q, q - c)
    return jnp.logical_and(same, rr >= qq), jnp.logical_and(same, rr > qq)


def _half_col(c, v0, v1):
    return jnp.where(_iota2((2 * c, 1), 0) < c, v0, v1)


def _ln_rows(r, g, b):
    mu = jnp.mean(r, -1, keepdims=True)
    d = r - mu
    var = jnp.mean(d * d, -1, keepdims=True)
    return d * lax.rsqrt(var + LN_EPS) * g + b


def _head64_sums(x):
    lo, _ = _half_masks((x.shape[0], LANES))
    outs = []
    for gi in range(x.shape[1] // LANES):
        xs = x[:, gi * LANES : (gi + 1) * LANES]
        s_lo = jnp.sum(jnp.where(lo, xs, 0.0), -1, keepdims=True)
        s_hi = jnp.sum(jnp.where(lo, 0.0, xs), -1, keepdims=True)
        outs.append(jnp.where(lo, s_lo, s_hi))
    return jnp.concatenate(outs, axis=1)


def _ada_body(c_ref, w_ref, b_ref, o_ref):
    o_ref[...] = _dot(_silu(c_ref[...]), w_ref[...]) + b_ref[...]


def _ada_mod(c_pad, w, b):
    n = w.shape[1]
    tn = n // 4
    return pl.pallas_call(
        _ada_body,
        grid=(n // tn,),
        in_specs=[
            pl.BlockSpec(c_pad.shape, lambda j: (0, 0)),
            pl.BlockSpec((w.shape[0], tn), lambda j: (0, j)),
            pl.BlockSpec((1, tn), lambda j: (0, j)),
        ],
        out_specs=pl.BlockSpec((c_pad.shape[0], tn), lambda j: (0, j)),
        out_shape=jax.ShapeDtypeStruct((c_pad.shape[0], n), f32),
        compiler_params=_cparams("arbitrary"),
        name="ada_mod",
    )(c_pad, w, b)


def _entry_body(x_ref, g_ref, b_ref, sc_ref, sh_ref, xn_ref, h_ref):
    xn = _ln_rows(x_ref[...], g_ref[...], b_ref[...])
    xn_ref[...] = xn
    h_ref[...] = (xn * (1.0 + sc_ref[...]) + sh_ref[...]).astype(h_ref.dtype)


def _mod_spec(rows_per_batch, tm, slot):
    return pl.BlockSpec((None, 1, D_MODEL), lambda i: ((i * tm // rows_per_batch) * 6 + slot, 0, 0))


def _row_spec(tm, width=D_MODEL):
    return pl.BlockSpec((tm, width), lambda i: (i, 0))


def _vec_spec(width=D_MODEL):
    return pl.BlockSpec((1, width), lambda i: (0, 0))


def _entry_ln(x, g, b, mod, t):
    n = x.shape[0]
    tm = 512
    return pl.pallas_call(
        _entry_body,
        grid=(n // tm,),
        in_specs=[_row_spec(tm), _vec_spec(), _vec_spec(), _mod_spec(t, tm, 1), _mod_spec(t, tm, 0)],
        out_specs=[_row_spec(tm), _row_spec(tm)],
        out_shape=[jax.ShapeDtypeStruct((n, D_MODEL), f32), jax.ShapeDtypeStruct((n, D_MODEL), bf16)],
        compiler_params=_cparams("arbitrary"),
        name="entry_ln",
    )(x, g, b, mod, mod)


def _mm_body(x_ref, w_ref, o_ref):
    o_ref[...] = jnp.dot(x_ref[...], w_ref[...], preferred_element_type=f32).astype(o_ref.dtype)


def _matmul(x, w, tm, tn, out_dtype, name):
    m, k = x.shape
    n = w.shape[1]
    return pl.pallas_call(
        _mm_body,
        grid=(m // tm, n // tn),
        in_specs=[pl.BlockSpec((tm, k), lambda i, j: (i, 0)), pl.BlockSpec((k, tn), lambda i, j: (0, j))],
        out_specs=pl.BlockSpec((tm, tn), lambda i, j: (i, j)),
        out_shape=jax.ShapeDtypeStruct((m, n), out_dtype),
        compiler_params=_cparams("arbitrary", "arbitrary"),
        name=name,
    )(x, w)


def _seq_spec(nb, c, width, col_block):
    return pl.BlockSpec((nb, c, width), lambda n: (0, n, col_block))


def _bcast_spec(shape):
    return pl.BlockSpec(shape, lambda n: (0,) * len(shape))


def _lockstep(*chains):
    pending = [(g, None) for g in chains]
    while pending:
        requests = []
        for g, value in pending:
            try:
                requests.append((g, g.send(value)))
            except StopIteration:
                pass
        pending = [(g, [thunk() for thunk in thunks]) for g, thunks in requests]


def _unit_lower_inverses_staged(ps, nilpotent):
    n = ps[0].shape[0]
    eye = (_iota2((n, n), 0) == _iota2((n, n), 1)).astype(f32)
    ts = [eye + p for p in ps]
    for _ in range(int(math.log2(nilpotent)) - 1):
        ps = yield [functools.partial(_dot, p, p) for p in ps]
        prods = yield [functools.partial(_dot, t, p) for t, p in zip(ts, ps)]
        ts = [t + d for t, d in zip(ts, prods)]
    return ts


def _gdn_body(qkv_ref, zg_ref, sm_ref, conv_ref, par_ref, ng_ref, o_ref, s_ref, buf_ref):
    nb, step_rows, w3 = qkv_ref.shape
    c = CHUNK

    @pl.when(pl.program_id(0) == 0)
    def _():
        s_ref[...] = jnp.zeros(s_ref.shape, f32)
        buf_ref[:, 0:8, :] = jnp.zeros((nb, 8, w3), f32)

    incl, strict = _pair_masks(c)
    pairs = range(GDN_HEADS // 2)
    heads = [(p, e) for p in pairs for e in range(2)]

    def col(a, lane):
        return a[:, lane : lane + 1]

    def chain(bi, s):
        rs = slice(s * c, (s + 1) * c)
        raw = qkv_ref[bi, rs, :].astype(f32)
        buf_ref[bi, 8 : 8 + c, :] = raw
        conv = conv_ref[3:4, :] * raw
        for j in range(3):
            conv = conv + conv_ref[j : j + 1, :] * buf_ref[bi, 5 + j : 5 + j + c, :]
        buf_ref[bi, 0:8, :] = buf_ref[bi, c : c + 8, :]
        x = _silu(conv)

        sm = sm_ref[bi, rs, :]
        gcum = _cumsum_rows(-jnp.exp(par_ref[0:1, :]) * _softplus(sm + par_ref[1:2, :]))
        gcum_t = gcum.T
        beta_all = _sigmoid(sm)
        zg = zg_ref[bi, rs, :].astype(f32)

        def head(i, h):
            return x[:, i * 512 + h * 128 : i * 512 + (h + 1) * 128]

        k_st, kb_st, q_st, v_st, g_st, eg_st, beta_st, decay, g_last = [], [], [], [], [], [], [], [], []
        for p in pairs:
            h0, h1 = 2 * p, 2 * p + 1
            qs, ks = [], []
            for h in (h0, h1):
                q = head(0, h)
                k = head(1, h)
                qs.append(q * (lax.rsqrt(jnp.sum(q * q, -1, keepdims=True) + 1e-6) * GDN_DK**-0.5))
                ks.append(k * lax.rsqrt(jnp.sum(k * k, -1, keepdims=True) + 1e-6))
            g = _rows(col(gcum, SM_GDN_A + h0), col(gcum, SM_GDN_A + h1))
            g_row = _lanes(gcum_t[SM_GDN_A + h0 : SM_GDN_A + h0 + 1, :], gcum_t[SM_GDN_A + h1 : SM_GDN_A + h1 + 1, :])
            beta = _rows(col(beta_all, SM_GDN_B + h0), col(beta_all, SM_GDN_B + h1))
            k2 = _rows(*ks)
            k_st.append(k2)
            kb_st.append(k2 * beta)
            q_st.append(_rows(*qs))
            v_st.append(_rows(head(2, h0), head(2, h1)))
            g_st.append(g)
            eg_st.append(jnp.exp(g))
            beta_st.append(beta)
            decay.append(jnp.where(incl, jnp.exp(g - g_row), 0.0))
            g_last.append((gcum[c - 1 : c, SM_GDN_A + h0 : SM_GDN_A + h0 + 1], gcum[c - 1 : c, SM_GDN_A + h1 : SM_GDN_A + h1 + 1]))

        m1 = yield [functools.partial(_dot_nt, _rows(kb_st[p], q_st[p]), k_st[p]) for p in pairs]
        t_inv = yield from _unit_lower_inverses_staged(
            [-jnp.where(strict, m1[p][: 2 * c] * decay[p], 0.0) for p in pairs], c)
        qk = [m1[p][2 * c :] * decay[p] for p in pairs]
        uw = yield [functools.partial(_dot, t_inv[p], _lanes(v_st[p] * beta_st[p], kb_st[p] * eg_st[p])) for p in pairs]
        qd = [q_st[p] * eg_st[p] for p in pairs]
        k_dec = [k_st[p][e * c : (e + 1) * c] * jnp.exp(g_last[p][e] - g_st[p][e * c : (e + 1) * c]) for p, e in heads]
        for _ in range(2 * s):
            yield []
        s_old = [s_ref[bi, 2 * p + e] for p, e in heads]
        ws_qs = yield [functools.partial(_dot, _rows(uw[p][e * c : (e + 1) * c, 128:], qd[p][e * c : (e + 1) * c]), st)
                       for (p, e), st in zip(heads, s_old)]
        v_new = [uw[p][e * c : (e + 1) * c, :128] - m[:c] for (p, e), m in zip(heads, ws_qs)]
        prods = yield ([functools.partial(_dot, qk[p], _rows(v_new[2 * p], v_new[2 * p + 1])) for p in pairs]
                       + [functools.partial(_dot_tn, kd, vn) for kd, vn in zip(k_dec, v_new)])
        o_st, s_add = prods[: len(pairs)], prods[len(pairs):]
        for i, (p, e) in enumerate(heads):
            h = 2 * p + e
            s_ref[bi, h] = s_old[i] * jnp.exp(g_last[p][e]) + s_add[i]
            o = ws_qs[i][c:] + o_st[p][e * c : (e + 1) * c]
            o = o * lax.rsqrt(jnp.mean(o * o, -1, keepdims=True) + 1e-6) * ng_ref[...]
            o_ref[bi, rs, h * 128 : (h + 1) * 128] = o * _silu(zg[:, h * 128 : (h + 1) * 128])

    _lockstep(*(chain(bi, s) for s in range(step_rows // c) for bi in range(nb)))


def _gdn(z, zm, conv_w, par, norm_g, b, t):
    c, rows = CHUNK, STEP_CHUNKS * CHUNK
    z3 =z.reshape(b, t, z.shape[1])
    zm3 = zm.reshape(b, t, zm.shape[1])
    out = pl.pallas_call(
        _gdn_body,
        grid=(t // rows,),
        in_specs=[
            _seq_spec(b, rows, 1536, Z_GDN // 1536),
            _seq_spec(b, rows, 512, (Z_GDN + 1536) // 512),
            _seq_spec(b, rows, 128, MISC_SMALLS // 128),
            _bcast_spec((4, 1536)),
            _bcast_spec((8, 128)),
            _bcast_spec((1, 128)),
        ],
        out_specs=_seq_spec(b, rows, 512, 0),
        out_shape=jax.ShapeDtypeStruct((b, t, 512), f32),
        scratch_shapes=[pltpu.VMEM((b, GDN_HEADS, 128, 128), f32), pltpu.VMEM((b, c + 8, 1536), f32)],
        compiler_params=_cparams("arbitrary"),
        name="gdn",
    )(z3, z3, zm3, conv_w, par, norm_g)
    return out.reshape(b * t, 512)


def _mlstm_body(qk_ref, v_ref, og_ref, sm_ref, par_ref, ng_ref, o_ref, cs_ref, m_ref):
    nb, step_rows, _ = qk_ref.shape
    c = CHUNK
    assert c == HALF

    @pl.when(pl.program_id(0) == 0)
    def _():
        cs_ref[...] = jnp.zeros(cs_ref.shape, f32)
        m_ref[...] = jnp.zeros(m_ref.shape, f32)

    def cap(x):
        return MLSTM_SOFTCAP * jnp.tanh(x / MLSTM_SOFTCAP)

    incl, _ = _pair_masks(c)
    ones = jnp.ones((2 * c, LANES), f32)
    pairs = range(MLSTM_HEADS // 2)

    def full(x):
        return jnp.broadcast_to(x, (2 * c, LANES))

    def col(a, lane):
        return a[:, lane : lane + 1]

    def row(a, lane):
        return a[lane : lane + 1, :]

    def chain(bi, s):
        rs = slice(s * c, (s + 1) * c)
        capped = cap(sm_ref[bi, rs, :] + par_ref[0:1, :])
        log_i = capped
        bcum = _cumsum_rows(-_softplus(-capped))
        log_i_t = log_i.T
        bcum_t = bcum.T
        qk = qk_ref[bi, rs, :].astype(f32)
        v_all = v_ref[bi, rs, :].astype(f32)
        og = og_ref[bi, rs, :].astype(f32)
        q_st, k_st, v_st, bc, li, b_last, dmat, m_intra = [], [], [], [], [], [], [], []
        for p in pairs:
            h0, h1 = 2 * p, 2 * p + 1
            q_st.append(_stack2(qk[:, p * 128 : (p + 1) * 128]))
            k_st.append(_stack2(qk[:, 256 + p * 128 : 256 + (p + 1) * 128] * MLSTM_DQK**-0.5))
            v_st.append(_rows(v_all[:, h0 * 128 : (h0 + 1) * 128], v_all[:, h1 * 128 : (h1 + 1) * 128]))
            bc.append(full(_rows(col(bcum, SM_ML_F + h0), col(bcum, SM_ML_F + h1))))
            li.append(full(_rows(col(log_i, SM_ML_I + h0), col(log_i, SM_ML_I + h1))))
            b_row = _lanes(row(bcum_t, SM_ML_F + h0), row(bcum_t, SM_ML_F + h1))
            li_row = _lanes(row(log_i_t, SM_ML_I + h0), row(log_i_t, SM_ML_I + h1))
            b_last.append(full(_half_col(c, bcum[c - 1 : c, SM_ML_F + h0 : SM_ML_F + h0 + 1],
                                         bcum[c - 1 : c, SM_ML_F + h1 : SM_ML_F + h1 + 1])))
            log_d = jnp.where(incl, bc[p] - b_row + li_row, -jnp.inf)
            m_intra.append(full(jnp.max(log_d, -1, keepdims=True)))
            dmat.append(jnp.exp(log_d - m_intra[p]))

        qk_t = yield [functools.partial(_dot_nt, q_st[p], k_st[p]) for p in pairs]
        pm = [dmat[p] * qk_t[p] for p in pairs]
        v_ext = [_lanes(v_st[p], ones) for p in pairs]
        intra = yield [functools.partial(_dot, pm[p], v_ext[p]) for p in pairs]
        for _ in range(s):
            yield []
        cs_old = [cs_ref[bi, p] for p in pairs]
        m_old = [_half_col(c, m_ref[bi, 2 * p : 2 * p + 1, :], m_ref[bi, 2 * p + 1 : 2 * p + 2, :]) for p in pairs]
        kw_st, s_old, m_new = [], [], []
        for p in pairs:
            log_e = b_last[p] - bc[p] + li[p]
            m_end = _half_col(c, jnp.max(log_e[:c], 0, keepdims=True), jnp.max(log_e[c:], 0, keepdims=True))
            m_new.append(jnp.maximum(b_last[p] + m_old[p], m_end))
            s_old.append(jnp.exp(b_last[p] + m_old[p] - m_new[p]))
            kw_st.append(k_st[p] * (jnp.exp(log_e - m_end) * jnp.exp(m_end - m_new[p])))
        prods = yield ([functools.partial(_dot, q_st[p], cs_old[p]) for p in pairs]
                       + [functools.partial(_dot_tn, kw_st[p], v_ext[p]) for p in pairs])
        q_cs, cs_add = prods[: len(pairs)], prods[len(pairs):]
        for p in pairs:
            m_t = jnp.maximum(bc[p] + m_old[p], m_intra[p])
            s_inter = jnp.exp(bc[p] + m_old[p] - m_t)
            s_intra = jnp.exp(m_intra[p] - m_t)
            num = s_inter * q_cs[p][:, :LANES] + s_intra * intra[p][:, :LANES]
            den = s_inter * q_cs[p][:, LANES:] + s_intra * intra[p][:, LANES:]
            hc = num / jnp.maximum(jnp.abs(den), jnp.exp(-m_t))
            cs_ref[bi, p] = _lanes(s_old[p], s_old[p]) * cs_old[p] + cs_add[p]
            for e in range(2):
                h = 2 * p + e
                m_ref[bi, h : h + 1, :] = m_new[p][e * c : e * c + 1]
                hs = hc[e * c : (e + 1) * c]
                hs = hs * lax.rsqrt(jnp.mean(hs * hs, -1, keepdims=True) + 1e-6) * ng_ref[:, h * 128 : (h + 1) * 128]
                o_ref[bi, rs, h * 128 : (h + 1) * 128] = hs * _sigmoid(og[:, h * 128 : (h + 1) * 128])

    _lockstep(*(chain(bi, s) for s in range(step_rows // c) for bi in range(nb)))


def _mlstm(z, zm, par, norm_g, b, t):
    c, rows = CHUNK, STEP_CHUNKS * CHUNK
    z3 =z.reshape(b, t, z.shape[1])
    zm3 = zm.reshape(b, t, zm.shape[1])
    out = pl.pallas_call(
        _mlstm_body,
        grid=(t // rows,),
        in_specs=[
            _seq_spec(b, rows, 512, Z_MLSTM // 512),
            _seq_spec(b, rows, 512, (Z_MLSTM + 512) // 512),
            _seq_spec(b, rows, 512, (Z_MLSTM + 1024) // 512),
            _seq_spec(b, rows, 128, MISC_SMALLS // 128),
            _bcast_spec((8, 128)),
            _bcast_spec((1, 512)),
        ],
        out_specs=_seq_spec(b, rows, 512, 0),
        out_shape=jax.ShapeDtypeStruct((b, t, 512), f32),
        scratch_shapes=[
            pltpu.VMEM((b, MLSTM_HEADS // 2, 128, 256), f32),
            pltpu.VMEM((b, 8, 128), f32),
        ],
        compiler_params=_cparams("arbitrary"),
        name="mlstm",
    )(z3, z3, z3, zm3, par, norm_g)
    return out.reshape(b * t, 512)


def _rwkv_body(*refs, has_vres):
    if has_vres:
        (r_ref, k_ref, v_ref, misc_ref, vf_ref, mu_ref, mum_ref, vecs_ref, w2_ref, a2_ref, g2_ref, v2_ref,
         y_ref, st_ref, tail_ref, tailm_ref) = refs
    else:
        (r_ref, k_ref, v_ref, misc_ref, mu_ref, mum_ref, vecs_ref, w2_ref, a2_ref, g2_ref,
         y_ref, vf_out_ref, st_ref, tail_ref, tailm_ref) = refs
    nb, step_rows, _ = r_ref.shape
    c = CHUNK

    @pl.when(pl.program_id(0) == 0)
    def _():
        st_ref[...] = jnp.zeros(st_ref.shape, f32)
        tail_ref[:, 0:8, :] = jnp.zeros((nb, 8, tail_ref.shape[2]), f32)
        tailm_ref[:, 0:8, :] = jnp.zeros((nb, 8, tailm_ref.shape[2]), f32)

    incl, strict = _pair_masks(c)
    same_head = (_iota2((LANES, LANES), 0) < HALF) == (_iota2((LANES, LANES), 1) < HALF)
    pairs = range(RWKV_HEADS // 2)
    sl = [slice(p * 128, (p + 1) * 128) for p in pairs]
    c2 = 2 * c
    w0, a0, k_k, k_a, r_k, lnx_g, lnx_b, v0 = (vecs_ref[i : i + 1, :] for i in range(8))

    def chain(bi, s):
        rs = slice(s * c, (s + 1) * c)
        raw = jnp.concatenate([r_ref[bi, rs, :], k_ref[bi, rs, :], v_ref[bi, rs, :]], axis=1).astype(f32)
        tail_ref[bi, 8 : 8 + c, :] = raw
        prev = tail_ref[bi, 7 : 7 + c, :]
        tail_ref[bi, 0:8, :] = tail_ref[bi, c : c + 8, :]
        rkv = raw + mu_ref[...] * (prev - raw)
        rawm = misc_ref[bi, rs, :]
        tailm_ref[bi, 8 : 8 + c, :] = rawm
        prevm = tailm_ref[bi, 7 : 7 + c, :]
        tailm_ref[bi, 0:8, :] = tailm_ref[bi, c : c + 8, :]
        misc = rawm + mum_ref[...] * (prevm - rawm)

        r = rkv[:, 0:512]
        k = rkv[:, 512:1024]
        v = rkv[:, 1024:1536]
        wa = misc[:, 0:128]
        lora = [functools.partial(_dot, jnp.tanh(wa), w2_ref[...]), functools.partial(_dot, wa, a2_ref[...]),
                functools.partial(_dot, _sigmoid(misc[:, 128:256]), g2_ref[...])]
        if has_vres:
            lora.append(functools.partial(_dot, misc[:, 256:384], v2_ref[...]))
        lora = yield lora
        log_w = -jnp.exp(-_softplus(-(w0 + lora[0])) - 0.5)
        a = _sigmoid(a0 + lora[1])
        g = lora[2]
        if has_vres:
            v = v + (vf_ref[bi, rs, :] - v) * _sigmoid(v0 + lora[3])
        else:
            vf_out_ref[bi, rs, :] = v
        kk = k * k_k
        kk = kk * lax.rsqrt(_head64_sums(kk * kk) + 1e-6)
        k = k * (1.0 + (a - 1.0) * k_a)
        alpha = -kk
        beta = kk * a
        lw = _cumsum_rows(log_w)
        lw_last = lw[c - 1 : c, :]
        inv = jnp.exp(-lw)
        al_bar = alpha * jnp.exp(lw - log_w)
        be_hat = beta * inv
        k_hat = k * inv
        r_bar = r * jnp.exp(lw)
        to_end = jnp.exp(lw_last - lw)
        be_end = beta * to_end
        k_end = k * to_end
        g_chunk = jnp.exp(lw_last)

        al_st = [_stack2(al_bar[:, s]) for s in sl]
        v_st = [_stack2(v[:, s]) for s in sl]
        m1 = yield [functools.partial(_dot_nt, _rows(al_st[p], _stack2(r_bar[:, s])),
                                      _rows(be_hat[:, s], be_hat[:, s], k_hat[:, s], k_hat[:, s]))
                    for p, s in zip(pairs, sl)]
        t_inv = yield from _unit_lower_inverses_staged([jnp.where(strict, m[:c2, :c2], 0.0) for m in m1], c)
        l_ak = [jnp.where(strict, m[:c2, c2:], 0.0) for m in m1]
        q_rb = [jnp.where(incl, m[c2:, :c2], 0.0) for m in m1]
        q_rk = [jnp.where(incl, m[c2:, c2:], 0.0) for m in m1]
        m2 = yield [functools.partial(_dot, _rows(l_ak[p], q_rk[p]), v_st[p]) for p in pairs]
        m3 = yield [functools.partial(_dot, t_inv[p], _lanes(al_st[p], m2[p][:c2])) for p in pairs]
        for _ in range(2 * s):
            yield []
        st_old = [st_ref[bi, p] for p in pairs]
        m4 = yield [functools.partial(_dot_nt, _rows(_unstack2(m3[p][:, :128]), r_bar[:, sl[p]]), st_old[p]) for p in pairs]
        u = [m4[p][:c] + _unstack2(m3[p][:, 128:]) for p in pairs]
        m56 = yield ([functools.partial(_dot, q_rb[p], _stack2(u[p])) for p in pairs]
                     + [functools.partial(_dot_tn, _rows(u[p], v[:, sl[p]]), _rows(be_end[:, sl[p]], k_end[:, sl[p]]))
                        for p in pairs])
        m5, m6 = m56[: len(pairs)], m56[len(pairs):]
        ys = []
        for p in pairs:
            st_ref[bi, p] = g_chunk[:, sl[p]] * st_old[p] + jnp.where(same_head, m6[p], 0.0)
            ys.append(m4[p][c:] + _unstack2(m5[p]) + _unstack2(m2[p][c2:]))
        y = jnp.concatenate(ys, axis=1)
        mean = _head64_sums(y) * (1.0 / RWKV_DH)
        d = y - mean
        var = _head64_sums(d * d) * (1.0 / RWKV_DH)
        y = d * lax.rsqrt(var + RWKV_GN_EPS) * lnx_g + lnx_b
        bonus = _head64_sums(r * k * r_k) * v
        y_ref[bi, rs, :] = (y + bonus) * g

    _lockstep(*(chain(bi, s) for s in range(step_rows // c) for bi in range(nb)))


def _rwkv(z, zm, v_first, mu, mu_misc, vecs, w2p, a2p, g2, v2p, b, t):
    c, rows = CHUNK, STEP_CHUNKS * CHUNK
    has_vres = v_first is not None
    z3 = z.reshape(b, t, z.shape[1])
    blk = _seq_spec(b, rows, 512, 0)
    in_specs = [_seq_spec(b, rows, 512, (Z_RWKV + 512 * i) // 512) for i in range(3)] + [blk]
    args = [z3, z3, z3, zm.reshape(b, t, zm.shape[1])]
    if has_vres:
        in_specs.append(blk)
        args.append(v_first.reshape(b, t, 512))
    in_specs += [_bcast_spec(s) for s in ((1, 1536), (1, 512), (8, 512), (128, 512), (128, 512), (128, 512))]
    args += [mu, mu_misc, vecs, w2p, a2p, g2]
    if has_vres:
        in_specs.append(_bcast_spec((128, 512)))
        args.append(v2p)
    out_sd = jax.ShapeDtypeStruct((b, t, 512), f32)
    res = pl.pallas_call(
        functools.partial(_rwkv_body, has_vres=has_vres),
        grid=(t // rows,),
        in_specs=in_specs,
        out_specs=blk if has_vres else [blk, blk],
        out_shape=out_sd if has_vres else [out_sd, out_sd],
        scratch_shapes=[
            pltpu.VMEM((b, RWKV_HEADS // 2, 128, 128), f32),
            pltpu.VMEM((b, c + 8, 1536), f32),
            pltpu.VMEM((b, c + 8, 512), f32),
        ],
        compiler_params=_cparams("arbitrary"),
        name="rwkv_res" if has_vres else "rwkv",
    )(*args)
    if has_vres:
        return res.reshape(b * t, 512), v_first
    return res[0].reshape(b * t, 512), res[1].reshape(b * t, 512)


def _split3(x):
    hi = x.astype(bf16)
    r1 = x - hi.astype(f32)
    mid = r1.astype(bf16)
    return hi, mid, (r1 - mid.astype(f32)).astype(bf16)


def _router_pack(w):
    return _pad_cols(jnp.concatenate(_split3(w.astype(f32)), axis=1), LANES)


def _route(h, w3, b):
    h_hi, h_mid, h_lo = _split3(h)
    a = _dot(h_hi, w3)
    m = _dot(h_mid, w3)
    lo = _dot(h_lo, w3)
    down = lambda x, groups: pltpu.roll(x, LANES - groups * N_EXPERTS, 1)
    logits = a + down(a, 1) + down(a, 2) + m + down(m, 1) + lo + b
    lane = _iota2(logits.shape, 1)
    logits = jnp.where(lane < N_EXPERTS, logits, -jnp.inf)
    m1 = jnp.max(logits, -1, keepdims=True)
    i1 = jnp.min(jnp.where(logits == m1, lane, LANES), -1, keepdims=True)
    rest = jnp.where(lane == i1, -jnp.inf, logits)
    m2 = jnp.max(rest, -1, keepdims=True)
    i2 = jnp.min(jnp.where(rest == m2, lane, LANES), -1, keepdims=True)
    e = jnp.exp(m2 - m1)
    g1 = 1.0 / (1.0 + e)
    g2 = e / (1.0 + e)
    return (jnp.where(lane == 0, i1, jnp.where(lane == 1, i2, 0)),
            jnp.where(lane == 0, g1, jnp.where(lane == 1, g2, 0.0)))


def _merge_body(*refs, with_router):
    (ya_ref, yb_ref, yc_ref, ga_ref, gb_ref, gc_ref, x_ref, wb_ref, wo_ref, gt_ref, g_ref, b_ref,
     sc_ref, sh_ref) = refs[:14]
    merged = _sigmoid(ga_ref[...].astype(f32)) * _dot(ya_ref[...], wb_ref[0])
    merged = merged + _sigmoid(gb_ref[...].astype(f32)) * _dot(yb_ref[...], wb_ref[1])
    merged = merged + _sigmoid(gc_ref[...].astype(f32)) * _dot(yc_ref[...], wb_ref[2])
    y = _dot(merged, wo_ref[...])
    xn = _ln_rows(DEEPNORM_ALPHA * x_ref[...] + (1.0 + gt_ref[...]) * y, g_ref[...], b_ref[...])
    h = xn * (1.0 + sc_ref[...]) + sh_ref[...]
    if with_router:
        rw_ref, rb_ref, xn_ref, h_ref, idx_ref, gate_ref = refs[14:]
        idx_ref[...], gate_ref[...] = _route(h, rw_ref[...], rb_ref[...])
    else:
        xn_ref, h_ref = refs[14:]
    xn_ref[...] = xn
    h_ref[...] = h.astype(h_ref.dtype)


def _merge(ya, yb, yc, z, x, wb, wo, mod, ln_g, ln_b, t, h_dtype, router=None):
    n = x.shape[0]
    tm = 512
    gate = lambda i: pl.BlockSpec((tm, 1024), lambda r: (r, Z_GATE // 1024 + i))
    in_specs = [
        _row_spec(tm, 512), _row_spec(tm, 512), _row_spec(tm, 512), gate(0), gate(1), gate(2), _row_spec(tm),
        pl.BlockSpec((3, 512, D_MODEL), lambda r: (0, 0, 0)),
        pl.BlockSpec((D_MODEL, D_MODEL), lambda r: (0, 0)),
        _mod_spec(t, tm, 2), _vec_spec(), _vec_spec(), _mod_spec(t, tm, 4), _mod_spec(t, tm, 3),
    ]
    args = [ya, yb, yc, z, z, z, x, wb, wo, mod, ln_g, ln_b, mod, mod]
    out_specs = [_row_spec(tm), _row_spec(tm)]
    out_shape = [jax.ShapeDtypeStruct((n, D_MODEL), f32), jax.ShapeDtypeStruct((n, D_MODEL), h_dtype)]
    if router is not None:
        in_specs += [pl.BlockSpec((D_MODEL, LANES), lambda r: (0, 0)), _vec_spec(LANES)]
        args += list(router)
        out_specs += [_row_spec(tm, LANES), _row_spec(tm, LANES)]
        out_shape += [jax.ShapeDtypeStruct((n, LANES), i32), jax.ShapeDtypeStruct((n, LANES), f32)]
    return pl.pallas_call(
        functools.partial(_merge_body, with_router=router is not None),
        grid=(n // tm,),
        in_specs=in_specs,
        out_specs=out_specs,
        out_shape=out_shape,
        compiler_params=_cparams("arbitrary"),
        name="merge" if router is None else "merge_route",
    )(*args)


def _ffn_body(h_ref, x_ref, wg_ref, wu_ref, wd_ref, gt_ref, g_ref, b_ref, sc_ref, sh_ref, xn_ref, hn_ref, acc_ref):
    f = pl.program_id(1)

    @pl.when(f == 0)
    def _():
        acc_ref[...] = jnp.zeros(acc_ref.shape, f32)

    h = h_ref[...]
    act = _silu(_dot(h, wg_ref[...])) * _dot(h, wu_ref[...])
    acc_ref[...] += _dot(act, wd_ref[...])

    @pl.when(f == pl.num_programs(1) - 1)
    def _():
        xn = _ln_rows(DEEPNORM_ALPHA * x_ref[...] + (1.0 + gt_ref[...]) * acc_ref[...], g_ref[...], b_ref[...])
        xn_ref[...] = xn
        hn_ref[...] = (xn * (1.0 + sc_ref[...]) + sh_ref[...]).astype(hn_ref.dtype)


def _ffn(h, x, wg, wu, wd, mod, mod_next, ln_g, ln_b, t):
    n = x.shape[0]
    dff = wg.shape[1]
    tm, tf = 512, dff // 2
    row = pl.BlockSpec((tm, D_MODEL), lambda i, f: (i, 0))
    vec = pl.BlockSpec((1, D_MODEL), lambda i, f: (0, 0))
    modspec = lambda slot: pl.BlockSpec((None, 1, D_MODEL), lambda i, f: ((i * tm // t) * 6 + slot, 0, 0))
    return pl.pallas_call(
        _ffn_body,
        grid=(n // tm, dff // tf),
        in_specs=[
            row, row,
            pl.BlockSpec((D_MODEL, tf), lambda i, f: (0, f)),
            pl.BlockSpec((D_MODEL, tf), lambda i, f: (0, f)),
            pl.BlockSpec((tf, D_MODEL), lambda i, f: (f, 0)),
            modspec(5), vec, vec, modspec(1), modspec(0),
        ],
        out_specs=[row, row],
        out_shape=[jax.ShapeDtypeStruct((n, D_MODEL), f32), jax.ShapeDtypeStruct((n, D_MODEL), bf16)],
        scratch_shapes=[pltpu.VMEM((tm, D_MODEL), f32)],
        compiler_params=_cparams("arbitrary", "arbitrary"),
        name="ffn",
    )(h, x, wg, wu, wd, mod, ln_g, ln_b, mod_next, mod_next)


def _moe_body(be_ref, live_ref, tok_ref, tokn_ref, dst_ref, gate_ref, h_hbm, wg_ref, wu_ref, wd_ref, y_hbm,
              xbuf, xb16, acc_ref, obuf, gsem, ssem):
    j = pl.program_id(0)
    f = pl.program_id(1)
    nb = pl.num_programs(0)
    nf = pl.num_programs(1)
    rows = xbuf.shape[0]
    live = live_ref[j] > 0
    next_live = live_ref[jnp.minimum(j + 1, nb - 1)] > 0
    prev_live = jnp.logical_and(j > 0, live_ref[jnp.maximum(j - 1, 0)] > 0)

    def gather(ref, static):
        def start(r, carry):
            pltpu.make_async_copy(h_hbm.at[pl.ds(ref[0, 0, r], 1), :], xbuf.at[pl.ds(r, 1), :], gsem).start()
            return carry

        if static:
            for r in range(rows):
                start(r, 0)
        else:
            lax.fori_loop(0, rows, start, 0, unroll=8)

    def gather_wait():
        pltpu.make_async_copy(h_hbm.at[pl.ds(0, rows), :], xbuf, gsem).wait()

    def scatter_wait():
        pltpu.make_async_copy(obuf, y_hbm.at[pl.ds(0, rows), :], ssem).wait()

    @pl.when(jnp.logical_and(f == 0, j == 0))
    def _():
        obuf[...] = jnp.zeros(obuf.shape, f32)
        spare = pltpu.make_async_copy(obuf, y_hbm.at[pl.ds(y_hbm.shape[0] - rows, rows), :], ssem)
        spare.start()
        spare.wait()

    @pl.when(jnp.logical_and(f == 0, jnp.logical_and(j == 0, live)))
    def _():
        gather(tok_ref, static=False)

    @pl.when(jnp.logical_and(f == 0, live))
    def _():
        gather_wait()
        xb16[...] = xbuf[...].astype(bf16)
        acc_ref[...] = jnp.zeros(acc_ref.shape, f32)

    @pl.when(jnp.logical_and(f == 0, jnp.logical_and(live, jnp.logical_and(j + 1 < nb, next_live))))
    def _():
        gather(tokn_ref, static=True)

    half = rows // 2

    def swiglu(rs):
        x = xb16[rs, :]
        act = _silu(_dot(x, wg_ref[...])) * _dot(x, wu_ref[...])
        acc_ref[rs, :] += _dot(act, wd_ref[...])

    @pl.when(live_ref[j] > half)
    def _():
        swiglu(slice(0, rows))

    @pl.when(jnp.logical_and(live, live_ref[j] <= half))
    def _():
        swiglu(slice(0, half))

    @pl.when(jnp.logical_and(f == nf - 1, prev_live))
    def _():
        scatter_wait()

    @pl.when(jnp.logical_and(f == nf - 1, live))
    def _():
        obuf[...] = acc_ref[...] * gate_ref[...]
        for r in range(rows):
            pltpu.make_async_copy(obuf.at[pl.ds(r, 1), :], y_hbm.at[pl.ds(dst_ref[0, 0, r], 1), :], ssem).start()

    @pl.when(jnp.logical_and(f == nf - 1, jnp.logical_and(j == nb - 1, live)))
    def _():
        scatter_wait()


def _moe(h, block_e, block_live, row_tok, row_dst, row_gate, wg, wu, wd):
    n = h.shape[0]
    n_blocks = block_e.shape[0]
    rb = MOE_ROWS
    dff = wg.shape[2]
    tf = 512
    smem_blk = pl.BlockSpec((1, 1, rb), lambda j, f, be, lv: (j, 0, 0), memory_space=pltpu.SMEM)
    smem_next = pl.BlockSpec((1, 1, rb), lambda j, f, be, lv: (jnp.minimum(j + 1, n_blocks - 1), 0, 0),
                             memory_space=pltpu.SMEM)
    grid_spec = pltpu.PrefetchScalarGridSpec(
        num_scalar_prefetch=2,
        grid=(n_blocks, dff // tf),
        in_specs=[
            smem_blk,
            smem_next,
            smem_blk,
            pl.BlockSpec((rb, 1), lambda j, f, be, lv: (j, 0)),
            pl.BlockSpec(memory_space=pl.ANY),
            pl.BlockSpec((None, D_MODEL, tf), lambda j, f, be, lv: (be[j], 0, f)),
            pl.BlockSpec((None, D_MODEL, tf), lambda j, f, be, lv: (be[j], 0, f)),
            pl.BlockSpec((None, tf, D_MODEL), lambda j, f, be, lv: (be[j], f, 0)),
        ],
        out_specs=pl.BlockSpec(memory_space=pl.ANY),
        scratch_shapes=[
            pltpu.VMEM((rb, D_MODEL), f32),
            pltpu.VMEM((rb, D_MODEL), bf16),
            pltpu.VMEM((rb, D_MODEL), f32),
            pltpu.VMEM((rb, D_MODEL), f32),
            pltpu.SemaphoreType.DMA,
            pltpu.SemaphoreType.DMA,
        ],
    )
    return pl.pallas_call(
        _moe_body,
        grid_spec=grid_spec,
        out_shape=jax.ShapeDtypeStruct((TOP_K * n + rb, D_MODEL), f32),
        compiler_params=_cparams("arbitrary", "arbitrary"),
        name="moe",
    )(block_e, block_live, row_tok, row_tok, row_dst, row_gate, h, wg, wu, wd)


def _routing_tables(top_e, top_g, n):
    rb = MOE_ROWS
    n_assign = n * TOP_K
    n_blocks = -(-n_assign // rb) + N_EXPERTS
    rows = n_blocks * rb
    e_flat = top_e.reshape(-1)
    onehot = (e_flat[:, None] == jnp.arange(N_EXPERTS, dtype=i32)[None, :]).astype(i32)
    rank = jnp.sum((jnp.cumsum(onehot, axis=0) - onehot) * onehot, axis=1)
    counts = jnp.sum(onehot, axis=0)
    padded = (counts + rb - 1) // rb * rb
    pad_end = jnp.cumsum(padded)
    pad_start = pad_end - padded
    dest = jnp.sum(onehot * pad_start[None, :], axis=1) + rank
    row_a = jnp.full((rows,), n_assign, i32).at[dest].set(jnp.arange(n_assign, dtype=i32), unique_indices=True)
    valid = row_a < n_assign
    tok, slot_k = row_a // TOP_K, row_a % TOP_K
    row_tok = jnp.where(valid, tok, 0)
    row_dst = jnp.where(valid, slot_k * n + tok, n_assign + jnp.arange(rows, dtype=i32) % rb)
    row_gate = jnp.where(valid, top_g.reshape(-1)[jnp.minimum(row_a, n_assign - 1)], 0.0)
    starts = jnp.arange(n_blocks, dtype=i32) * rb
    block_e = jnp.minimum(jnp.sum((pad_end[None, :] <= starts[:, None]).astype(i32), axis=1), N_EXPERTS - 1)
    block_live = jnp.clip(jnp.sum((jnp.arange(N_EXPERTS)[None, :] == block_e[:, None]) * (pad_start + counts)[None, :],
                                  axis=1) - starts, 0, rb).astype(i32)
    block_live = jnp.where(starts < pad_end[-1], block_live, 0)
    return (block_e, block_live, row_tok.reshape(n_blocks, 1, rb), row_dst.reshape(n_blocks, 1, rb),
            row_gate.reshape(rows, 1))


def _final_body(x_ref, y0_ref, y1_ref, gt_ref, g_ref, b_ref, o_ref):
    y = y0_ref[...] + y1_ref[...]
    o_ref[...] = _ln_rows(DEEPNORM_ALPHA * x_ref[...] + (1.0 + gt_ref[...]) * y, g_ref[...], b_ref[...])


def _final_ln(x, y2, mod, ln_g, ln_b, t):
    n = x.shape[0]
    tm = 512
    nb = n // tm
    return pl.pallas_call(
        _final_body,
        grid=(nb,),
        in_specs=[_row_spec(tm), _row_spec(tm), pl.BlockSpec((tm, D_MODEL), lambda i: (i + nb, 0)),
                  _mod_spec(t, tm, 5), _vec_spec(), _vec_spec()],
        out_specs=_row_spec(tm),
        out_shape=jax.ShapeDtypeStruct((n, D_MODEL), f32),
        compiler_params=_cparams("arbitrary"),
        name="final_ln",
    )(x, y2, y2, mod, ln_g, ln_b)


def _pad_cols(w, width):
    return jnp.pad(w, ((0, 0), (0, width - w.shape[1])))


def _pad_rows(w, height):
    return jnp.pad(w, ((0, height - w.shape[0]), (0, 0)))


def _regroup_w_in(w_in, v1):
    d = w_in.shape[0]
    gdn, ml, rw = 0, 2056, 3600
    smalls = jnp.concatenate([w_in[:, gdn + 2048 : gdn + 2056], w_in[:, ml + 1536 : ml + 1544]], axis=1)
    vdown = jnp.zeros((d, LORA_V), w_in.dtype) if v1 is None else v1
    misc = jnp.concatenate([w_in[:, rw + 1536 : rw + 1792], _pad_cols(vdown, 128), _pad_cols(smalls, 128)], axis=1)
    cat = jnp.concatenate(
        [w_in[:, gdn : gdn + 2048], w_in[:, ml : ml + 1536], w_in[:, rw : rw + 1536], w_in[:, 5392:8464]], axis=1)
    assert cat.shape[1] == Z_WIDTH and misc.shape[1] == MISC_WIDTH
    return cat.astype(bf16), misc.astype(bf16)


def _row(v, width=None):
    v = v.reshape(1, -1).astype(f32)
    return v if width is None else _pad_cols(v, width)


def kernel(x, c, ln_in_g, ln_in_b, ada_w, ada_b, w_in, gdn_conv, gdn_a_log, gdn_dt_bias, gdn_norm_g, mlstm_b_i, mlstm_b_f, mlstm_norm_g, rwkv_mu, rwkv_w0, rwkv_w2, rwkv_a0, rwkv_a2, rwkv_g2, rwkv_k_k, rwkv_k_a, rwkv_r_k, rwkv_lnx_g, rwkv_lnx_b, rwkv_v1, rwkv_mu_v1, rwkv_v0, rwkv_v2, w_branch, w_o, ln1_g, ln1_b, ln2_g, ln2_b, ffn_w_gate, ffn_w_up, ffn_w_down, moe_router, moe_router_b, moe_w_gate, moe_w_up, moe_w_down):
    b, t, d = x.shape
    n = b * t
    depth = w_in.shape[0]
    xf = x.reshape(n, d)

    c_pad = _pad_rows(c.astype(f32), 8)
    mods = []
    for layer in range(depth):
        mod = _ada_mod(c_pad, ada_w[layer].astype(bf16), _row(ada_b[layer]))[:b]
        mods.append(mod.reshape(b * 6, 1, d))

    xcur, h = _entry_ln(xf, _row(ln_in_g), _row(ln_in_b), mods[0], t)
    v_first = None
    out = None
    for layer in range(depth):
        mod = mods[layer]
        has_vres = layer > 0
        w_main, w_misc = _regroup_w_in(w_in[layer], rwkv_v1[layer - 1] if has_vres else None)
        z = _matmul(h, w_main, 2048, 1024, bf16, "in_proj")
        zm = _matmul(h, w_misc, 1024, MISC_WIDTH, f32, "in_proj_misc")

        gdn_par = jnp.zeros((8, LANES), f32).at[0, 0:4].set(gdn_a_log[layer]).at[1, 0:4].set(gdn_dt_bias[layer])
        y_a = _gdn(z, zm, gdn_conv[layer].astype(f32), gdn_par, _row(gdn_norm_g[layer]), b, t)

        ml_par = jnp.zeros((8, LANES), f32).at[0, SM_ML_I : SM_ML_I + 4].set(mlstm_b_i[layer])
        ml_par = ml_par.at[0, SM_ML_F : SM_ML_F + 4].set(mlstm_b_f[layer])
        y_b = _mlstm(z, zm, ml_par, _row(mlstm_norm_g[layer]), b, t)

        mu = rwkv_mu[layer].astype(f32)
        mu_v = rwkv_mu_v1[layer - 1].astype(f32) if has_vres else jnp.zeros((LORA_V,), f32)
        mu_misc = _row(jnp.concatenate([mu[1536:1792], mu_v]), 512)
        vecs = jnp.stack([
            rwkv_w0[layer], rwkv_a0[layer], rwkv_k_k[layer], rwkv_k_a[layer], rwkv_r_k[layer].reshape(-1),
            rwkv_lnx_g[layer], rwkv_lnx_b[layer],
            rwkv_v0[layer - 1] if has_vres else jnp.zeros((RWKV_WIDTH,), f32),
        ]).astype(f32)
        w2p = _pad_rows(rwkv_w2[layer], 128).astype(bf16)
        a2p = jnp.concatenate([jnp.zeros((LORA_W, RWKV_WIDTH), f32), rwkv_a2[layer]], axis=0).astype(bf16)
        v2p = _pad_rows(rwkv_v2[layer - 1], 128).astype(bf16) if has_vres else None
        y_c, v_first = _rwkv(z, zm, v_first, _row(mu[:1536]), mu_misc, vecs, w2p, a2p, rwkv_g2[layer].astype(bf16), v2p, b, t)

        moe_layer = layer % 2 == 1
        i = layer // 2
        router = (_router_pack(moe_router[i]), _row(moe_router_b[i], LANES)) if moe_layer else None
        merged = _merge(y_a, y_b, y_c, z, xcur, w_branch[layer].astype(bf16), w_o[layer].astype(bf16), mod,
                        _row(ln1_g[layer]), _row(ln1_b[layer]), t, f32 if moe_layer else bf16, router)
        xcur, h2 = merged[:2]
        if not moe_layer:
            mod_next = mods[layer + 1] if layer + 1 < depth else mod
            xcur, h = _ffn(h2, xcur, ffn_w_gate[i].astype(bf16), ffn_w_up[i].astype(bf16), ffn_w_down[i].astype(bf16),
                           mod, mod_next, _row(ln2_g[layer]), _row(ln2_b[layer]), t)
            out = xcur
        else:
            idx, gate = merged[2:]
            tables = _routing_tables(idx[:, :TOP_K], gate[:, :TOP_K], n)
            y2 = _moe(h2, *tables, moe_w_gate[i], moe_w_up[i], moe_w_down[i])
            out = _final_ln(xcur, y2, mod, _row(ln2_g[layer]), _row(ln2_b[layer]), t)
            xcur = out
    return out.reshape(b, t, d)
```

```python
import functools
import math

import jax
import jax.numpy as jnp
from jax import lax
from jax.experimental import pallas as pl
from jax.experimental.pallas import tpu as pltpu

f32 = jnp.float32
bf16 = jnp.bfloat16
i32 = jnp.int32

D_MODEL = 1024
DEPTH = 2
CHUNK = 64
GDN_HEADS = 4
GDN_DK = 128
MLSTM_HEADS = 4
MLSTM_DQK = 64
MLSTM_DV = 128
MLSTM_SOFTCAP = 15.0
RWKV_HEADS = 8
RWKV_DH = 64
RWKV_WIDTH = RWKV_HEADS * RWKV_DH
LORA_W = 64
LORA_A = 64
LORA_V = 32
LORA_G = 128
N_EXPERTS = 8
TOP_K = 2
DEEPNORM_ALPHA = (2 * DEPTH) ** 0.25
LN_EPS = 1e-5
RWKV_GN_EPS = 64e-5

LANES = 128
HALF = LANES // 2

Z_GDN = 0
Z_MLSTM = 2048
Z_RWKV = 3584
Z_GATE = 5120
Z_WIDTH = 8192
MISC_WIDTH = 512
MISC_SMALLS = 384
SM_GDN_A, SM_GDN_B, SM_ML_I, SM_ML_F = 0, 4, 8, 12

STEP_CHUNKS = 4
MOE_ROWS = 1024
VMEM_LIMIT = 56 * 1024 * 1024


def _cparams(*sem):
    return pltpu.CompilerParams(dimension_semantics=sem, vmem_limit_bytes=VMEM_LIMIT)


def _dot(a, b):
    return jnp.dot(a.astype(bf16), b.astype(bf16), preferred_element_type=f32)


def _dot_nt(a, b):
    return lax.dot_general(a.astype(bf16), b.astype(bf16), (((1,), (1,)), ((), ())), preferred_element_type=f32)


def _dot_tn(a, b):
    return lax.dot_general(a.astype(bf16), b.astype(bf16), (((0,), (0,)), ((), ())), preferred_element_type=f32)


def _sigmoid(x):
    return 1.0 / (1.0 + jnp.exp(-x))


def _silu(x):
    return x * _sigmoid(x)


def _softplus(x):
    return jnp.maximum(x, 0.0) + jnp.log1p(jnp.exp(-jnp.abs(x)))


def _iota2(shape, axis):
    return lax.broadcasted_iota(i32, shape, axis)


def _cumsum_rows(x):
    n = x.shape[0]
    rows = _iota2(x.shape, 0)
    sh = 1
    while sh < n:
        x = x + jnp.where(rows >= sh, pltpu.roll(x, sh, 0), 0.0)
        sh *= 2
    return x


def _unit_lower_inverses(ps, nilpotent):
    n = ps[0].shape[0]
    eye = (_iota2((n, n), 0) == _iota2((n, n), 1)).astype(f32)
    ts = [eye + p for p in ps]
    for _ in range(int(math.log2(nilpotent)) - 1):
        ps = [_dot(p, p) for p in ps]
        ts = [t + _dot(t, p) for t, p in zip(ts, ps)]
    return ts


def _rows(*xs):
    return jnp.concatenate(xs, axis=0)


def _lanes(*xs):
    return jnp.concatenate(xs, axis=1)


def _half_masks(shape):
    lane = _iota2(shape, len(shape) - 1)
    return lane < HALF, lane >= HALF


def _stack2(x):
    lo, _ = _half_masks(x.shape)
    return _rows(jnp.where(lo, x, 0.0), jnp.where(lo, 0.0, x))


def _unstack2(x):
    c = x.shape[0] // 2
    return x[:c] + x[c:]


def _pair_masks(c):
    r = _iota2((2 * c, 2 * c), 0)
    q = _iota2((2 * c, 2 * c), 1)
    same = (r < c) == (q < c)
    rr = jnp.where(r < c, r, r - c)
    qq = jnp.where(q < c, q, q - c)
    return jnp.logical_and(same, rr >= qq), jnp.logical_and(same, rr > qq)


def _half_col(c, v0, v1):
    return jnp.where(_iota2((2 * c, 1), 0) < c, v0, v1)


def _ln_rows(r, g, b):
    mu = jnp.mean(r, -1, keepdims=True)
    d = r - mu
    var = jnp.mean(d * d, -1, keepdims=True)
    return d * lax.rsqrt(var + LN_EPS) * g + b


def _head64_sums(x):
    lo, _ = _half_masks((x.shape[0], LANES))
    outs = []
    for gi in range(x.shape[1] // LANES):
        xs = x[:, gi * LANES : (gi + 1) * LANES]
        s_lo = jnp.sum(jnp.where(lo, xs, 0.0), -1, keepdims=True)
        s_hi = jnp.sum(jnp.where(lo, 0.0, xs), -1, keepdims=True)
        outs.append(jnp.where(lo, s_lo, s_hi))
    return jnp.concatenate(outs, axis=1)


def _ada_body(c_ref, w_ref, b_ref, o_ref):
    o_ref[...] = _dot(_silu(c_ref[...]), w_ref[...]) + b_ref[...]


def _ada_mod(c_pad, w, b):
    n = w.shape[1]
    tn = n // 4
    return pl.pallas_call(
        _ada_body,
        grid=(n // tn,),
        in_specs=[
            pl.BlockSpec(c_pad.shape, lambda j: (0, 0)),
            pl.BlockSpec((w.shape[0], tn), lambda j: (0, j)),
            pl.BlockSpec((1, tn), lambda j: (0, j)),
        ],
        out_specs=pl.BlockSpec((c_pad.shape[0], tn), lambda j: (0, j)),
        out_shape=jax.ShapeDtypeStruct((c_pad.shape[0], n), f32),
        compiler_params=_cparams("arbitrary"),
        name="ada_mod",
    )(c_pad, w, b)


def _entry_body(x_ref, g_ref, b_ref, sc_ref, sh_ref, xn_ref, h_ref):
    xn = _ln_rows(x_ref[...], g_ref[...], b_ref[...])
    xn_ref[...] = xn
    h_ref[...] = (xn * (1.0 + sc_ref[...]) + sh_ref[...]).astype(h_ref.dtype)


def _mod_spec(rows_per_batch, tm, slot):
    return pl.BlockSpec((None, 1, D_MODEL), lambda i: ((i * tm // rows_per_batch) * 6 + slot, 0, 0))


def _row_spec(tm, width=D_MODEL):
    return pl.BlockSpec((tm, width), lambda i: (i, 0))


def _vec_spec(width=D_MODEL):
    return pl.BlockSpec((1, width), lambda i: (0, 0))


def _entry_ln(x, g, b, mod, t):
    n = x.shape[0]
    tm = 512
    return pl.pallas_call(
        _entry_body,
        grid=(n // tm,),
        in_specs=[_row_spec(tm), _vec_spec(), _vec_spec(), _mod_spec(t, tm, 1), _mod_spec(t, tm, 0)],
        out_specs=[_row_spec(tm), _row_spec(tm)],
        out_shape=[jax.ShapeDtypeStruct((n, D_MODEL), f32), jax.ShapeDtypeStruct((n, D_MODEL), bf16)],
        compiler_params=_cparams("arbitrary"),
        name="entry_ln",
    )(x, g, b, mod, mod)


def _in_proj_body(x_ref, w_ref, wm_ref, z_ref, zm_ref):
    x = x_ref[...]
    z_ref[...] = jnp.dot(x, w_ref[...], preferred_element_type=f32).astype(z_ref.dtype)

    @pl.when(pl.program_id(1) == 0)
    def _():
        zm_ref[...] = jnp.dot(x, wm_ref[...], preferred_element_type=f32)


def _in_proj(x, w, w_misc, tm, tn):
    m, k = x.shape
    n = w.shape[1]
    nm = w_misc.shape[1]
    return pl.pallas_call(
        _in_proj_body,
        grid=(m // tm, n // tn),
        in_specs=[pl.BlockSpec((tm, k), lambda i, j: (i, 0)), pl.BlockSpec((k, tn), lambda i, j: (0, j)),
                  pl.BlockSpec((k, nm), lambda i, j: (0, 0))],
        out_specs=[pl.BlockSpec((tm, tn), lambda i, j: (i, j)), pl.BlockSpec((tm, nm), lambda i, j: (i, 0))],
        out_shape=[jax.ShapeDtypeStruct((m, n), bf16), jax.ShapeDtypeStruct((m, nm), f32)],
        compiler_params=_cparams("arbitrary", "arbitrary"),
        name="in_proj",
    )(x, w, w_misc)


def _seq_spec(nb, c, width, col_block):
    return pl.BlockSpec((nb, c, width), lambda n: (0, n, col_block))


def _bcast_spec(shape):
    return pl.BlockSpec(shape, lambda n: (0,) * len(shape))


def _lockstep(*chains):
    pending = [(g, None) for g in chains]
    while pending:
        requests = []
        for g, value in pending:
            try:
                requests.append((g, g.send(value)))
            except StopIteration:
                pass
        pending = [(g, [thunk() for thunk in thunks]) for g, thunks in requests]


def _unit_lower_inverses_staged(ps, nilpotent):
    n = ps[0].shape[0]
    eye = (_iota2((n, n), 0) == _iota2((n, n), 1)).astype(f32)
    ts = [eye + p for p in ps]
    for _ in range(int(math.log2(nilpotent)) - 1):
        ps = yield [functools.partial(_dot, p, p) for p in ps]
        prods = yield [functools.partial(_dot, t, p) for t, p in zip(ts, ps)]
        ts = [t + d for t, d in zip(ts, prods)]
    return ts


def _gdn_body(qkv_ref, zg_ref, sm_ref, conv_ref, par_ref, ng_ref, o_ref, s_ref, buf_ref):
    nb, step_rows, w3 = qkv_ref.shape
    c = CHUNK

    @pl.when(pl.program_id(0) == 0)
    def _():
        s_ref[...] = jnp.zeros(s_ref.shape, f32)
        buf_ref[:, 0:8, :] = jnp.zeros((nb, 8, w3), f32)

    incl, strict = _pair_masks(c)
    pairs = range(GDN_HEADS // 2)
    heads = [(p, e) for p in pairs for e in range(2)]

    def col(a, lane):
        return a[:, lane : lane + 1]

    def chain(bi, s):
        rs = slice(s * c, (s + 1) * c)
        raw = qkv_ref[bi, rs, :].astype(f32)
        buf_ref[bi, 8 : 8 + c, :] = raw
        conv = conv_ref[3:4, :] * raw
        for j in range(3):
            conv = conv + conv_ref[j : j + 1, :] * buf_ref[bi, 5 + j : 5 + j + c, :]
        buf_ref[bi, 0:8, :] = buf_ref[bi, c : c + 8, :]
        x = _silu(conv)

        sm = sm_ref[bi, rs, :]
        gcum = _cumsum_rows(-jnp.exp(par_ref[0:1, :]) * _softplus(sm + par_ref[1:2, :]))
        gcum_t = gcum.T
        beta_all = _sigmoid(sm)
        zg = zg_ref[bi, rs, :].astype(f32)

        def head(i, h):
            return x[:, i * 512 + h * 128 : i * 512 + (h + 1) * 128]

        k_st, kb_st, q_st, v_st, g_st, eg_st, beta_st, decay, g_last = [], [], [], [], [], [], [], [], []
        for p in pairs:
            h0, h1 = 2 * p, 2 * p + 1
            qs, ks = [], []
            for h in (h0, h1):
                q = head(0, h)
                k = head(1, h)
                qs.append(q * (lax.rsqrt(jnp.sum(q * q, -1, keepdims=True) + 1e-6) * GDN_DK**-0.5))
                ks.append(k * lax.rsqrt(jnp.sum(k * k, -1, keepdims=True) + 1e-6))
            g = _rows(col(gcum, SM_GDN_A + h0), col(gcum, SM_GDN_A + h1))
            g_row = _lanes(gcum_t[SM_GDN_A + h0 : SM_GDN_A + h0 + 1, :], gcum_t[SM_GDN_A + h1 : SM_GDN_A + h1 + 1, :])
            beta = _rows(col(beta_all, SM_GDN_B + h0), col(beta_all, SM_GDN_B + h1))
            k2 = _rows(*ks)
            k_st.append(k2)
            kb_st.append(k2 * beta)
            q_st.append(_rows(*qs))
            v_st.append(_rows(head(2, h0), head(2, h1)))
            g_st.append(g)
            eg_st.append(jnp.exp(g))
            beta_st.append(beta)
            decay.append(jnp.where(incl, jnp.exp(g - g_row), 0.0))
            g_last.append((gcum[c - 1 : c, SM_GDN_A + h0 : SM_GDN_A + h0 + 1], gcum[c - 1 : c, SM_GDN_A + h1 : SM_GDN_A + h1 + 1]))

        m1 = yield [functools.partial(_dot_nt, _rows(kb_st[p], q_st[p]), k_st[p]) for p in pairs]
        t_inv = yield from _unit_lower_inverses_staged(
            [-jnp.where(strict, m1[p][: 2 * c] * decay[p], 0.0) for p in pairs], c)
        qk = [m1[p][2 * c :] * decay[p] for p in pairs]
        uw = yield [functools.partial(_dot, t_inv[p], _lanes(v_st[p] * beta_st[p], kb_st[p] * eg_st[p])) for p in pairs]
        qd = [q_st[p] * eg_st[p] for p in pairs]
        k_dec = [k_st[p][e * c : (e + 1) * c] * jnp.exp(g_last[p][e] - g_st[p][e * c : (e + 1) * c]) for p, e in heads]
        for _ in range(2 * s):
            yield []
        s_old = [s_ref[bi, 2 * p + e] for p, e in heads]
        ws_qs = yield [functools.partial(_dot, _rows(uw[p][e * c : (e + 1) * c, 128:], qd[p][e * c : (e + 1) * c]), st)
                       for (p, e), st in zip(heads, s_old)]
        v_new = [uw[p][e * c : (e + 1) * c, :128] - m[:c] for (p, e), m in zip(heads, ws_qs)]
        prods = yield ([functools.partial(_dot, qk[p], _rows(v_new[2 * p], v_new[2 * p + 1])) for p in pairs]
                       + [functools.partial(_dot_tn, kd, vn) for kd, vn in zip(k_dec, v_new)])
        o_st, s_add = prods[: len(pairs)], prods[len(pairs):]
        for i, (p, e) in enumerate(heads):
            h = 2 * p + e
            s_ref[bi, h] = s_old[i] * jnp.exp(g_last[p][e]) + s_add[i]
            o = ws_qs[i][c:] + o_st[p][e * c : (e + 1) * c]
            o = o * lax.rsqrt(jnp.mean(o * o, -1, keepdims=True) + 1e-6) * ng_ref[...]
            o_ref[bi, rs, h * 128 : (h + 1) * 128] = (o * _silu(zg[:, h * 128 : (h + 1) * 128])).astype(o_ref.dtype)

    _lockstep(*(chain(bi, s) for s in range(step_rows // c) for bi in range(nb)))


def _gdn(z, zm, conv_w, par, norm_g, b, t):
    c, rows = CHUNK, STEP_CHUNKS * CHUNK
    z3 =z.reshape(b, t, z.shape[1])
    zm3 = zm.reshape(b, t, zm.shape[1])
    out = pl.pallas_call(
        _gdn_body,
        grid=(t // rows,),
        in_specs=[
            _seq_spec(b, rows, 1536, Z_GDN // 1536),
            _seq_spec(b, rows, 512, (Z_GDN + 1536) // 512),
            _seq_spec(b, rows, 128, MISC_SMALLS // 128),
            _bcast_spec((4, 1536)),
            _bcast_spec((8, 128)),
            _bcast_spec((1, 128)),
        ],
        out_specs=_seq_spec(b, rows, 512, 0),
        out_shape=jax.ShapeDtypeStruct((b, t, 512), bf16),
        scratch_shapes=[pltpu.VMEM((b, GDN_HEADS, 128, 128), f32), pltpu.VMEM((b, c + 8, 1536), f32)],
        compiler_params=_cparams("arbitrary"),
        name="gdn",
    )(z3, z3, zm3, conv_w, par, norm_g)
    return out.reshape(b * t, 512)


def _mlstm_body(qk_ref, v_ref, og_ref, sm_ref, par_ref, ng_ref, o_ref, cs_ref, m_ref):
    nb, step_rows, _ = qk_ref.shape
    c = CHUNK
    assert c == HALF

    @pl.when(pl.program_id(0) == 0)
    def _():
        cs_ref[...] = jnp.zeros(cs_ref.shape, f32)
        m_ref[...] = jnp.zeros(m_ref.shape, f32)

    def cap(x):
        return MLSTM_SOFTCAP * jnp.tanh(x / MLSTM_SOFTCAP)

    incl, _ = _pair_masks(c)
    ones = jnp.ones((2 * c, LANES), f32)
    pairs = range(MLSTM_HEADS // 2)

    def full(x):
        return jnp.broadcast_to(x, (2 * c, LANES))

    def col(a, lane):
        return a[:, lane : lane + 1]

    def row(a, lane):
        return a[lane : lane + 1, :]

    def chain(bi, s):
        rs = slice(s * c, (s + 1) * c)
        capped = cap(sm_ref[bi, rs, :] + par_ref[0:1, :])
        log_i = capped
        bcum = _cumsum_rows(-_softplus(-capped))
        log_i_t = log_i.T
        bcum_t = bcum.T
        qk = qk_ref[bi, rs, :].astype(f32)
        v_all = v_ref[bi, rs, :].astype(f32)
        og = og_ref[bi, rs, :].astype(f32)
        q_st, k_st, v_st, bc, li, b_last, dmat, m_intra = [], [], [], [], [], [], [], []
        for p in pairs:
            h0, h1 = 2 * p, 2 * p + 1
            q_st.append(_stack2(qk[:, p * 128 : (p + 1) * 128]))
            k_st.append(_stack2(qk[:, 256 + p * 128 : 256 + (p + 1) * 128] * MLSTM_DQK**-0.5))
            v_st.append(_rows(v_all[:, h0 * 128 : (h0 + 1) * 128], v_all[:, h1 * 128 : (h1 + 1) * 128]))
            bc.append(full(_rows(col(bcum, SM_ML_F + h0), col(bcum, SM_ML_F + h1))))
            li.append(full(_rows(col(log_i, SM_ML_I + h0), col(log_i, SM_ML_I + h1))))
            b_row = _lanes(row(bcum_t, SM_ML_F + h0), row(bcum_t, SM_ML_F + h1))
            li_row = _lanes(row(log_i_t, SM_ML_I + h0), row(log_i_t, SM_ML_I + h1))
            b_last.append(full(_half_col(c, bcum[c - 1 : c, SM_ML_F + h0 : SM_ML_F + h0 + 1],
                                         bcum[c - 1 : c, SM_ML_F + h1 : SM_ML_F + h1 + 1])))
            log_d = jnp.where(incl, bc[p] - b_row + li_row, -jnp.inf)
            m_intra.append(full(jnp.max(log_d, -1, keepdims=True)))
            dmat.append(jnp.exp(log_d - m_intra[p]))

        qk_t = yield [functools.partial(_dot_nt, q_st[p], k_st[p]) for p in pairs]
        pm = [dmat[p] * qk_t[p] for p in pairs]
        v_ext = [_lanes(v_st[p], ones) for p in pairs]
        intra = yield [functools.partial(_dot, pm[p], v_ext[p]) for p in pairs]
        for _ in range(s):
            yield []
        cs_old = [cs_ref[bi, p] for p in pairs]
        m_old = [_half_col(c, m_ref[bi, 2 * p : 2 * p + 1, :], m_ref[bi, 2 * p + 1 : 2 * p + 2, :]) for p in pairs]
        kw_st, s_old, m_new = [], [], []
        for p in pairs:
            log_e = b_last[p] - bc[p] + li[p]
            m_end = _half_col(c, jnp.max(log_e[:c], 0, keepdims=True), jnp.max(log_e[c:], 0, keepdims=True))
            m_new.append(jnp.maximum(b_last[p] + m_old[p], m_end))
            s_old.append(jnp.exp(b_last[p] + m_old[p] - m_new[p]))
            kw_st.append(k_st[p] * (jnp.exp(log_e - m_end) * jnp.exp(m_end - m_new[p])))
        prods = yield ([functools.partial(_dot, q_st[p], cs_old[p]) for p in pairs]
                       + [functools.partial(_dot_tn, kw_st[p], v_ext[p]) for p in pairs])
        q_cs, cs_add = prods[: len(pairs)], prods[len(pairs):]
        for p in pairs:
            m_t = jnp.maximum(bc[p] + m_old[p], m_intra[p])
            s_inter = jnp.exp(bc[p] + m_old[p] - m_t)
            s_intra = jnp.exp(m_intra[p] - m_t)
            num = s_inter * q_cs[p][:, :LANES] + s_intra * intra[p][:, :LANES]
            den = s_inter * q_cs[p][:, LANES:] + s_intra * intra[p][:, LANES:]
            hc = num / jnp.maximum(jnp.abs(den), jnp.exp(-m_t))
            cs_ref[bi, p] = _lanes(s_old[p], s_old[p]) * cs_old[p] + cs_add[p]
            for e in range(2):
                h = 2 * p + e
                m_ref[bi, h : h + 1, :] = m_new[p][e * c : e * c + 1]
                hs = hc[e * c : (e + 1) * c]
                hs = hs * lax.rsqrt(jnp.mean(hs * hs, -1, keepdims=True) + 1e-6) * ng_ref[:, h * 128 : (h + 1) * 128]
                o_ref[bi, rs, h * 128 : (h + 1) * 128] = (hs * _sigmoid(og[:, h * 128 : (h + 1) * 128])).astype(o_ref.dtype)

    _lockstep(*(chain(bi, s) for s in range(step_rows // c) for bi in range(nb)))


def _mlstm(z, zm, par, norm_g, b, t):
    c, rows = CHUNK, STEP_CHUNKS * CHUNK
    z3 = z.reshape(b, t, z.shape[1])
    zm3 = zm.reshape(b, t, zm.shape[1])
    out = pl.pallas_call(
        _mlstm_body,
        grid=(t // rows,),
        in_specs=[
            _seq_spec(b, rows, 512, Z_MLSTM // 512),
            _seq_spec(b, rows, 512, (Z_MLSTM + 512) // 512),
            _seq_spec(b, rows, 512, (Z_MLSTM + 1024) // 512),
            _seq_spec(b, rows, 128, MISC_SMALLS // 128),
            _bcast_spec((8, 128)),
            _bcast_spec((1, 512)),
        ],
        out_specs=_seq_spec(b, rows, 512, 0),
        out_shape=jax.ShapeDtypeStruct((b, t, 512), bf16),
        scratch_shapes=[
            pltpu.VMEM((b, MLSTM_HEADS // 2, 128, 256), f32),
            pltpu.VMEM((b, 8, 128), f32),
        ],
        compiler_params=_cparams("arbitrary"),
        name="mlstm",
    )(z3, z3, z3, zm3, par, norm_g)
    return out.reshape(b * t, 512)


def _rwkv_body(*refs, has_vres):
    if has_vres:
        (r_ref, k_ref, v_ref, misc_ref, vf_ref, mu_ref, mum_ref, vecs_ref, w2_ref, a2_ref, g2_ref, v2_ref,
         y_ref, st_ref, tail_ref, tailm_ref) = refs
    else:
        (r_ref, k_ref, v_ref, misc_ref, mu_ref, mum_ref, vecs_ref, w2_ref, a2_ref, g2_ref,
         y_ref, vf_out_ref, st_ref, tail_ref, tailm_ref) = refs
    nb, step_rows, _ = r_ref.shape
    c = CHUNK

    @pl.when(pl.program_id(0) == 0)
    def _():
        st_ref[...] = jnp.zeros(st_ref.shape, f32)
        tail_ref[:, 0:8, :] = jnp.zeros((nb, 8, tail_ref.shape[2]), f32)
        tailm_ref[:, 0:8, :] = jnp.zeros((nb, 8, tailm_ref.shape[2]), f32)

    incl, strict = _pair_masks(c)
    same_head = (_iota2((LANES, LANES), 0) < HALF) == (_iota2((LANES, LANES), 1) < HALF)
    pairs = range(RWKV_HEADS // 2)
    sl = [slice(p * 128, (p + 1) * 128) for p in pairs]
    c2 = 2 * c
    w0, a0, k_k, k_a, r_k, lnx_g, lnx_b, v0 = (vecs_ref[i : i + 1, :] for i in range(8))

    def chain(bi, s):
        rs = slice(s * c, (s + 1) * c)
        raw = jnp.concatenate([r_ref[bi, rs, :], k_ref[bi, rs, :], v_ref[bi, rs, :]], axis=1).astype(f32)
        tail_ref[bi, 8 : 8 + c, :] = raw
        prev = tail_ref[bi, 7 : 7 + c, :]
        tail_ref[bi, 0:8, :] = tail_ref[bi, c : c + 8, :]
        rkv = raw + mu_ref[...] * (prev - raw)
        rawm = misc_ref[bi, rs, :]
        tailm_ref[bi, 8 : 8 + c, :] = rawm
        prevm = tailm_ref[bi, 7 : 7 + c, :]
        tailm_ref[bi, 0:8, :] = tailm_ref[bi, c : c + 8, :]
        misc = rawm + mum_ref[...] * (prevm - rawm)

        r = rkv[:, 0:512]
        k = rkv[:, 512:1024]
        v = rkv[:, 1024:1536]
        wa = misc[:, 0:128]
        lora = [functools.partial(_dot, jnp.tanh(wa), w2_ref[...]), functools.partial(_dot, wa, a2_ref[...]),
                functools.partial(_dot, _sigmoid(misc[:, 128:256]), g2_ref[...])]
        if has_vres:
            lora.append(functools.partial(_dot, misc[:, 256:384], v2_ref[...]))
        lora = yield lora
        log_w = -jnp.exp(-_softplus(-(w0 + lora[0])) - 0.5)
        a = _sigmoid(a0 + lora[1])
        g = lora[2]
        if has_vres:
            v = v + (vf_ref[bi, rs, :] - v) * _sigmoid(v0 + lora[3])
        else:
            vf_out_ref[bi, rs, :] = v
        kk = k * k_k
        kk = kk * lax.rsqrt(_head64_sums(kk * kk) + 1e-6)
        k = k * (1.0 + (a - 1.0) * k_a)
        alpha = -kk
        beta = kk * a
        lw = _cumsum_rows(log_w)
        lw_last = lw[c - 1 : c, :]
        inv = jnp.exp(-lw)
        al_bar = alpha * jnp.exp(lw - log_w)
        be_hat = beta * inv
        k_hat = k * inv
        r_bar = r * jnp.exp(lw)
        to_end = jnp.exp(lw_last - lw)
        be_end = beta * to_end
        k_end = k * to_end
        g_chunk = jnp.exp(lw_last)

        al_st = [_stack2(al_bar[:, s]) for s in sl]
        v_st = [_stack2(v[:, s]) for s in sl]
        m1 = yield [functools.partial(_dot_nt, _rows(al_st[p], _stack2(r_bar[:, s])),
                                      _rows(be_hat[:, s], be_hat[:, s], k_hat[:, s], k_hat[:, s]))
                    for p, s in zip(pairs, sl)]
        t_inv = yield from _unit_lower_inverses_staged([jnp.where(strict, m[:c2, :c2], 0.0) for m in m1], c)
        l_ak = [jnp.where(strict, m[:c2, c2:], 0.0) for m in m1]
        q_rb = [jnp.where(incl, m[c2:, :c2], 0.0) for m in m1]
        q_rk = [jnp.where(incl, m[c2:, c2:], 0.0) for m in m1]
        m2 = yield [functools.partial(_dot, _rows(l_ak[p], q_rk[p]), v_st[p]) for p in pairs]
        m3 = yield [functools.partial(_dot, t_inv[p], _lanes(al_st[p], m2[p][:c2])) for p in pairs]
        for _ in range(2 * s):
            yield []
        st_old = [st_ref[bi, p] for p in pairs]
        m4 = yield [functools.partial(_dot_nt, _rows(_unstack2(m3[p][:, :128]), r_bar[:, sl[p]]), st_old[p]) for p in pairs]
        u = [m4[p][:c] + _unstack2(m3[p][:, 128:]) for p in pairs]
        m56 = yield ([functools.partial(_dot, q_rb[p], _stack2(u[p])) for p in pairs]
                     + [functools.partial(_dot_tn, _rows(u[p], v[:, sl[p]]), _rows(be_end[:, sl[p]], k_end[:, sl[p]]))
                        for p in pairs])
        m5, m6 = m56[: len(pairs)], m56[len(pairs):]
        ys = []
        for p in pairs:
            st_ref[bi, p] = g_chunk[:, sl[p]] * st_old[p] + jnp.where(same_head, m6[p], 0.0)
            ys.append(m4[p][c:] + _unstack2(m5[p]) + _unstack2(m2[p][c2:]))
        y = jnp.concatenate(ys, axis=1)
        mean = _head64_sums(y) * (1.0 / RWKV_DH)
        d = y - mean
        var = _head64_sums(d * d) * (1.0 / RWKV_DH)
        y = d * lax.rsqrt(var + RWKV_GN_EPS) * lnx_g + lnx_b
        bonus = _head64_sums(r * k * r_k) * v
        y_ref[bi, rs, :] = ((y + bonus) * g).astype(y_ref.dtype)

    _lockstep(*(chain(bi, s) for s in range(step_rows // c) for bi in range(nb)))


def _rwkv(z, zm, v_first, mu, mu_misc, vecs, w2p, a2p, g2, v2p, b, t):
    c, rows = CHUNK, STEP_CHUNKS * CHUNK
    has_vres = v_first is not None
    z3 = z.reshape(b, t, z.shape[1])
    blk = _seq_spec(b, rows, 512, 0)
    in_specs = [_seq_spec(b, rows, 512, (Z_RWKV + 512 * i) // 512) for i in range(3)] + [blk]
    args = [z3, z3, z3, zm.reshape(b, t, zm.shape[1])]
    if has_vres:
        in_specs.append(blk)
        args.append(v_first.reshape(b, t, 512))
    in_specs += [_bcast_spec(s) for s in ((1, 1536), (1, 512), (8, 512), (128, 512), (128, 512), (128, 512))]
    args += [mu, mu_misc, vecs, w2p, a2p, g2]
    if has_vres:
        in_specs.append(_bcast_spec((128, 512)))
        args.append(v2p)
    y_sd = jax.ShapeDtypeStruct((b, t, 512), bf16)
    res = pl.pallas_call(
        functools.partial(_rwkv_body, has_vres=has_vres),
        grid=(t // rows,),
        in_specs=in_specs,
        out_specs=blk if has_vres else [blk, blk],
        out_shape=y_sd if has_vres else [y_sd, jax.ShapeDtypeStruct((b, t, 512), f32)],
        scratch_shapes=[
            pltpu.VMEM((b, RWKV_HEADS // 2, 128, 128), f32),
            pltpu.VMEM((b, c + 8, 1536), f32),
            pltpu.VMEM((b, c + 8, 512), f32),
        ],
        compiler_params=_cparams("arbitrary"),
        name="rwkv_res" if has_vres else "rwkv",
    )(*args)
    if has_vres:
        return res.reshape(b * t, 512), v_first
    return res[0].reshape(b * t, 512), res[1].reshape(b * t, 512)


def _split3(x):
    hi = x.astype(bf16)
    r1 = x - hi.astype(f32)
    mid = r1.astype(bf16)
    return hi, mid, (r1 - mid.astype(f32)).astype(bf16)


def _router_pack(w):
    return _pad_cols(jnp.concatenate(_split3(w.astype(f32)), axis=1), LANES)


def _route(h, w3, b):
    h_hi, h_mid, h_lo = _split3(h)
    a = _dot(h_hi, w3)
    m = _dot(h_mid, w3)
    lo = _dot(h_lo, w3)
    down = lambda x, groups: pltpu.roll(x, LANES - groups * N_EXPERTS, 1)
    logits = a + down(a, 1) + down(a, 2) + m + down(m, 1) + lo + b
    lane = _iota2(logits.shape, 1)
    logits = jnp.where(lane < N_EXPERTS, logits, -jnp.inf)
    m1 = jnp.max(logits, -1, keepdims=True)
    i1 = jnp.min(jnp.where(logits == m1, lane, LANES), -1, keepdims=True)
    rest = jnp.where(lane == i1, -jnp.inf, logits)
    m2 = jnp.max(rest, -1, keepdims=True)
    i2 = jnp.min(jnp.where(rest == m2, lane, LANES), -1, keepdims=True)
    e = jnp.exp(m2 - m1)
    g1 = 1.0 / (1.0 + e)
    g2 = e / (1.0 + e)
    return (jnp.where(lane == 0, i1, jnp.where(lane == 1, i2, 0)),
            jnp.where(lane == 0, g1, jnp.where(lane == 1, g2, 0.0)))


def _merge_body(*refs, with_router):
    (ya_ref, yb_ref, yc_ref, ga_ref, gb_ref, gc_ref, x_ref, wb_ref, wo_ref, gt_ref, g_ref, b_ref,
     sc_ref, sh_ref) = refs[:14]
    merged = _sigmoid(ga_ref[...].astype(f32)) * _dot(ya_ref[...], wb_ref[0])
    merged = merged + _sigmoid(gb_ref[...].astype(f32)) * _dot(yb_ref[...], wb_ref[1])
    merged = merged + _sigmoid(gc_ref[...].astype(f32)) * _dot(yc_ref[...], wb_ref[2])
    y = _dot(merged, wo_ref[...])
    xn = _ln_rows(DEEPNORM_ALPHA * x_ref[...] + (1.0 + gt_ref[...]) * y, g_ref[...], b_ref[...])
    h = xn * (1.0 + sc_ref[...]) + sh_ref[...]
    if with_router:
        rw_ref, rb_ref, xn_ref, h_ref, idx_ref, gate_ref = refs[14:]
        idx_ref[...], gate_ref[...] = _route(h, rw_ref[...], rb_ref[...])
    else:
        xn_ref, h_ref = refs[14:]
    xn_ref[...] = xn
    h_ref[...] = h.astype(h_ref.dtype)


def _merge(ya, yb, yc, z, x, wb, wo, mod, ln_g, ln_b, t, h_dtype, router=None):
    n = x.shape[0]
    tm = 512
    gate = lambda i: pl.BlockSpec((tm, 1024), lambda r: (r, Z_GATE // 1024 + i))
    in_specs = [
        _row_spec(tm, 512), _row_spec(tm, 512), _row_spec(tm, 512), gate(0), gate(1), gate(2), _row_spec(tm),
        pl.BlockSpec((3, 512, D_MODEL), lambda r: (0, 0, 0)),
        pl.BlockSpec((D_MODEL, D_MODEL), lambda r: (0, 0)),
        _mod_spec(t, tm, 2), _vec_spec(), _vec_spec(), _mod_spec(t, tm, 4), _mod_spec(t, tm, 3),
    ]
    args = [ya, yb, yc, z, z, z, x, wb, wo, mod, ln_g, ln_b, mod, mod]
    out_specs = [_row_spec(tm), _row_spec(tm)]
    out_shape = [jax.ShapeDtypeStruct((n, D_MODEL), f32), jax.ShapeDtypeStruct((n, D_MODEL), h_dtype)]
    if router is not None:
        in_specs += [pl.BlockSpec((D_MODEL, LANES), lambda r: (0, 0)), _vec_spec(LANES)]
        args += list(router)
        out_specs += [_row_spec(tm, LANES), _row_spec(tm, LANES)]
        out_shape += [jax.ShapeDtypeStruct((n, LANES), i32), jax.ShapeDtypeStruct((n, LANES), f32)]
    return pl.pallas_call(
        functools.partial(_merge_body, with_router=router is not None),
        grid=(n // tm,),
        in_specs=in_specs,
        out_specs=out_specs,
        out_shape=out_shape,
        compiler_params=_cparams("arbitrary"),
        name="merge" if router is None else "merge_route",
    )(*args)


def _ffn_body(h_ref, x_ref, wg_ref, wu_ref, wd_ref, gt_ref, g_ref, b_ref, sc_ref, sh_ref, xn_ref, hn_ref, acc_ref):
    f = pl.program_id(1)

    @pl.when(f == 0)
    def _():
        acc_ref[...] = jnp.zeros(acc_ref.shape, f32)

    h = h_ref[...]
    act = _silu(_dot(h, wg_ref[...])) * _dot(h, wu_ref[...])
    acc_ref[...] += _dot(act, wd_ref[...])

    @pl.when(f == pl.num_programs(1) - 1)
    def _():
        xn = _ln_rows(DEEPNORM_ALPHA * x_ref[...] + (1.0 + gt_ref[...]) * acc_ref[...], g_ref[...], b_ref[...])
        xn_ref[...] = xn
        hn_ref[...] = (xn * (1.0 + sc_ref[...]) + sh_ref[...]).astype(hn_ref.dtype)


def _ffn(h, x, wg, wu, wd, mod, mod_next, ln_g, ln_b, t):
    n = x.shape[0]
    dff = wg.shape[1]
    tm, tf = 512, dff // 2
    row = pl.BlockSpec((tm, D_MODEL), lambda i, f: (i, 0))
    vec = pl.BlockSpec((1, D_MODEL), lambda i, f: (0, 0))
    modspec = lambda slot: pl.BlockSpec((None, 1, D_MODEL), lambda i, f: ((i * tm // t) * 6 + slot, 0, 0))
    return pl.pallas_call(
        _ffn_body,
        grid=(n // tm, dff // tf),
        in_specs=[
            row, row,
            pl.BlockSpec((D_MODEL, tf), lambda i, f: (0, f)),
            pl.BlockSpec((D_MODEL, tf), lambda i, f: (0, f)),
            pl.BlockSpec((tf, D_MODEL), lambda i, f: (f, 0)),
            modspec(5), vec, vec, modspec(1), modspec(0),
        ],
        out_specs=[row, row],
        out_shape=[jax.ShapeDtypeStruct((n, D_MODEL), f32), jax.ShapeDtypeStruct((n, D_MODEL), bf16)],
        scratch_shapes=[pltpu.VMEM((tm, D_MODEL), f32)],
        compiler_params=_cparams("arbitrary", "arbitrary"),
        name="ffn",
    )(h, x, wg, wu, wd, mod, ln_g, ln_b, mod_next, mod_next)


def _moe_body(be_ref, live_ref, tok_ref, tokn_ref, dst_ref, gate_ref, h_hbm, wg_ref, wu_ref, wd_ref, y_hbm,
              xbuf, xb16, acc_ref, obuf, gsem, ssem):
    j = pl.program_id(0)
    f = pl.program_id(1)
    nb = pl.num_programs(0)
    nf = pl.num_programs(1)
    rows = xbuf.shape[0]
    live = live_ref[j] > 0
    next_live = live_ref[jnp.minimum(j + 1, nb - 1)] > 0
    prev_live = jnp.logical_and(j > 0, live_ref[jnp.maximum(j - 1, 0)] > 0)

    def gather(ref, static):
        def start(r, carry):
            pltpu.make_async_copy(h_hbm.at[pl.ds(ref[0, 0, r], 1), :], xbuf.at[pl.ds(r, 1), :], gsem).start()
            return carry

        if static:
            for r in range(rows):
                start(r, 0)
        else:
            lax.fori_loop(0, rows, start, 0, unroll=8)

    def gather_wait():
        pltpu.make_async_copy(h_hbm.at[pl.ds(0, rows), :], xbuf, gsem).wait()

    def scatter_wait():
        pltpu.make_async_copy(obuf, y_hbm.at[pl.ds(0, rows), :], ssem).wait()

    @pl.when(jnp.logical_and(f == 0, j == 0))
    def _():
        obuf[...] = jnp.zeros(obuf.shape, f32)
        spare = pltpu.make_async_copy(obuf, y_hbm.at[pl.ds(y_hbm.shape[0] - rows, rows), :], ssem)
        spare.start()
        spare.wait()

    @pl.when(jnp.logical_and(f == 0, jnp.logical_and(j == 0, live)))
    def _():
        gather(tok_ref, static=False)

    @pl.when(jnp.logical_and(f == 0, live))
    def _():
        gather_wait()
        xb16[...] = xbuf[...].astype(bf16)
        acc_ref[...] = jnp.zeros(acc_ref.shape, f32)

    @pl.when(jnp.logical_and(f == 0, jnp.logical_and(live, jnp.logical_and(j + 1 < nb, next_live))))
    def _():
        gather(tokn_ref, static=True)

    half = rows // 2

    def swiglu(rs):
        x = xb16[rs, :]
        act = _silu(_dot(x, wg_ref[...])) * _dot(x, wu_ref[...])
        acc_ref[rs, :] += _dot(act, wd_ref[...])

    @pl.when(live_ref[j] > half)
    def _():
        swiglu(slice(0, rows))

    @pl.when(jnp.logical_and(live, live_ref[j] <= half))
    def _():
        swiglu(slice(0, half))

    @pl.when(jnp.logical_and(f == nf - 1, prev_live))
    def _():
        scatter_wait()

    @pl.when(jnp.logical_and(f == nf - 1, live))
    def _():
        obuf[...] = acc_ref[...] * gate_ref[...]
        for r in range(rows):
            pltpu.make_async_copy(obuf.at[pl.ds(r, 1), :], y_hbm.at[pl.ds(dst_ref[0, 0, r], 1), :], ssem).start()

    @pl.when(jnp.logical_and(f == nf - 1, jnp.logical_and(j == nb - 1, live)))
    def _():
        scatter_wait()


def _moe(h, block_e, block_live, row_tok, row_dst, row_gate, wg, wu, wd):
    n = h.shape[0]
    n_blocks = block_e.shape[0]
    rb = MOE_ROWS
    dff = wg.shape[2]
    tf = 512
    smem_blk =pl.BlockSpec((1, 1, rb), lambda j, f, be, lv: (j, 0, 0), memory_space=pltpu.SMEM)
    smem_next = pl.BlockSpec((1, 1, rb), lambda j, f, be, lv: (jnp.minimum(j + 1, n_blocks - 1), 0, 0),
                             memory_space=pltpu.SMEM)
    grid_spec = pltpu.PrefetchScalarGridSpec(
        num_scalar_prefetch=2,
        grid=(n_blocks, dff // tf),
        in_specs=[
            smem_blk,
            smem_next,
            smem_blk,
            pl.BlockSpec((rb, 1), lambda j, f, be, lv: (j, 0)),
            pl.BlockSpec(memory_space=pl.ANY),
            pl.BlockSpec((None, D_MODEL, tf), lambda j, f, be, lv: (be[j], 0, f)),
            pl.BlockSpec((None, D_MODEL, tf), lambda j, f, be, lv: (be[j], 0, f)),
            pl.BlockSpec((None, tf, D_MODEL), lambda j, f, be, lv: (be[j], f, 0)),
        ],
        out_specs=pl.BlockSpec(memory_space=pl.ANY),
        scratch_shapes=[
            pltpu.VMEM((rb, D_MODEL), f32),
            pltpu.VMEM((rb, D_MODEL), bf16),
            pltpu.VMEM((rb, D_MODEL), f32),
            pltpu.VMEM((rb, D_MODEL), f32),
            pltpu.SemaphoreType.DMA,
            pltpu.SemaphoreType.DMA,
        ],
    )
    return pl.pallas_call(
        _moe_body,
        grid_spec=grid_spec,
        out_shape=jax.ShapeDtypeStruct((TOP_K * n + rb, D_MODEL), f32),
        compiler_params=_cparams("arbitrary", "arbitrary"),
        name="moe",
    )(block_e, block_live, row_tok, row_tok, row_dst, row_gate, h, wg, wu, wd)


def _routing_tables(top_e, top_g, n):
    rb = MOE_ROWS
    n_assign = n * TOP_K
    n_blocks = -(-n_assign // rb) + N_EXPERTS
    rows = n_blocks * rb
    e_flat = top_e.reshape(-1)
    onehot = (e_flat[:, None] == jnp.arange(N_EXPERTS, dtype=i32)[None, :]).astype(i32)
    rank = jnp.sum((jnp.cumsum(onehot, axis=0) - onehot) * onehot, axis=1)
    counts = jnp.sum(onehot, axis=0)
    padded = (counts + rb - 1) // rb * rb
    pad_end = jnp.cumsum(padded)
    pad_start = pad_end - padded
    dest = jnp.sum(onehot * pad_start[None, :], axis=1) + rank
    row_a = jnp.full((rows,), n_assign, i32).at[dest].set(jnp.arange(n_assign, dtype=i32), unique_indices=True)
    valid = row_a < n_assign
    tok, slot_k = row_a // TOP_K, row_a % TOP_K
    row_tok = jnp.where(valid, tok, 0)
    row_dst = jnp.where(valid, slot_k * n + tok, n_assign + jnp.arange(rows, dtype=i32) % rb)
    row_gate = jnp.where(valid, top_g.reshape(-1)[jnp.minimum(row_a, n_assign - 1)], 0.0)
    starts = jnp.arange(n_blocks, dtype=i32) * rb
    block_e = jnp.minimum(jnp.sum((pad_end[None, :] <= starts[:, None]).astype(i32), axis=1), N_EXPERTS - 1)
    block_live = jnp.clip(jnp.sum((jnp.arange(N_EXPERTS)[None, :] == block_e[:, None]) * (pad_start + counts)[None, :],
                                  axis=1) - starts, 0, rb).astype(i32)
    block_live = jnp.where(starts < pad_end[-1], block_live, 0)
    return (block_e, block_live, row_tok.reshape(n_blocks, 1, rb), row_dst.reshape(n_blocks, 1, rb),
            row_gate.reshape(rows, 1))


def _final_body(x_ref, y0_ref, y1_ref, gt_ref, g_ref, b_ref, o_ref):
    y = y0_ref[...] + y1_ref[...]
    o_ref[...] = _ln_rows(DEEPNORM_ALPHA * x_ref[...] + (1.0 + gt_ref[...]) * y, g_ref[...], b_ref[...])


def _final_ln(x, y2, mod, ln_g, ln_b, t):
    n = x.shape[0]
    tm = 512
    nb = n // tm
    return pl.pallas_call(
        _final_body,
        grid=(nb,),
        in_specs=[_row_spec(tm), _row_spec(tm), pl.BlockSpec((tm, D_MODEL), lambda i: (i + nb, 0)),
                  _mod_spec(t, tm, 5), _vec_spec(), _vec_spec()],
        out_specs=_row_spec(tm),
        out_shape=jax.ShapeDtypeStruct((n, D_MODEL), f32),
        compiler_params=_cparams("arbitrary"),
        name="final_ln",
    )(x, y2, y2, mod, ln_g, ln_b)


def _pad_cols(w, width):
    return jnp.pad(w, ((0, 0), (0, width - w.shape[1])))


def _pad_rows(w, height):
    return jnp.pad(w, ((0, height - w.shape[0]), (0, 0)))


def _regroup_w_in(w_in, v1):
    d = w_in.shape[0]
    gdn, ml, rw = 0, 2056, 3600
    smalls = jnp.concatenate([w_in[:, gdn + 2048 : gdn + 2056], w_in[:, ml + 1536 : ml + 1544]], axis=1)
    vdown = jnp.zeros((d, LORA_V), w_in.dtype) if v1 is None else v1
    misc = jnp.concatenate([w_in[:, rw + 1536 : rw + 1792], _pad_cols(vdown, 128), _pad_cols(smalls, 128)], axis=1)
    cat = jnp.concatenate(
        [w_in[:, gdn : gdn + 2048], w_in[:, ml : ml + 1536], w_in[:, rw : rw + 1536], w_in[:, 5392:8464]], axis=1)
    assert cat.shape[1] == Z_WIDTH and misc.shape[1] == MISC_WIDTH
    return cat.astype(bf16), misc.astype(bf16)


def _row(v, width=None):
    v = v.reshape(1, -1).astype(f32)
    return v if width is None else _pad_cols(v, width)


def kernel(x, c, ln_in_g, ln_in_b, ada_w, ada_b, w_in, gdn_conv, gdn_a_log, gdn_dt_bias, gdn_norm_g, mlstm_b_i, mlstm_b_f, mlstm_norm_g, rwkv_mu, rwkv_w0, rwkv_w2, rwkv_a0, rwkv_a2, rwkv_g2, rwkv_k_k, rwkv_k_a, rwkv_r_k, rwkv_lnx_g, rwkv_lnx_b, rwkv_v1, rwkv_mu_v1, rwkv_v0, rwkv_v2, w_branch, w_o, ln1_g, ln1_b, ln2_g, ln2_b, ffn_w_gate, ffn_w_up, ffn_w_down, moe_router, moe_router_b, moe_w_gate, moe_w_up, moe_w_down):
    b, t, d = x.shape
    n = b * t
    depth = w_in.shape[0]
    xf = x.reshape(n, d)

    c_pad = _pad_rows(c.astype(f32), 8)
    mods = []
    for layer in range(depth):
        mod = _ada_mod(c_pad, ada_w[layer].astype(bf16), _row(ada_b[layer]))[:b]
        mods.append(mod.reshape(b * 6, 1, d))

    xcur, h = _entry_ln(xf, _row(ln_in_g), _row(ln_in_b), mods[0], t)
    v_first = None
    out = None
    for layer in range(depth):
        mod = mods[layer]
        has_vres = layer > 0
        w_main, w_misc = _regroup_w_in(w_in[layer], rwkv_v1[layer - 1] if has_vres else None)
        z, zm = _in_proj(h, w_main, w_misc, 2048, 1024)

        gdn_par = jnp.zeros((8, LANES), f32).at[0, 0:4].set(gdn_a_log[layer]).at[1, 0:4].set(gdn_dt_bias[layer])
        y_a = _gdn(z, zm, gdn_conv[layer].astype(f32), gdn_par, _row(gdn_norm_g[layer]), b, t)

        ml_par = jnp.zeros((8, LANES), f32).at[0, SM_ML_I : SM_ML_I + 4].set(mlstm_b_i[layer])
        ml_par = ml_par.at[0, SM_ML_F : SM_ML_F + 4].set(mlstm_b_f[layer])
        y_b = _mlstm(z, zm, ml_par, _row(mlstm_norm_g[layer]), b, t)

        mu = rwkv_mu[layer].astype(f32)
        mu_v = rwkv_mu_v1[layer - 1].astype(f32) if has_vres else jnp.zeros((LORA_V,), f32)
        mu_misc = _row(jnp.concatenate([mu[1536:1792], mu_v]), 512)
        vecs = jnp.stack([
            rwkv_w0[layer], rwkv_a0[layer], rwkv_k_k[layer], rwkv_k_a[layer], rwkv_r_k[layer].reshape(-1),
            rwkv_lnx_g[layer], rwkv_lnx_b[layer],
            rwkv_v0[layer - 1] if has_vres else jnp.zeros((RWKV_WIDTH,), f32),
        ]).astype(f32)
        w2p = _pad_rows(rwkv_w2[layer], 128).astype(bf16)
        a2p = jnp.concatenate([jnp.zeros((LORA_W, RWKV_WIDTH), f32), rwkv_a2[layer]], axis=0).astype(bf16)
        v2p = _pad_rows(rwkv_v2[layer - 1], 128).astype(bf16) if has_vres else None
        y_c, v_first = _rwkv(z, zm, v_first, _row(mu[:1536]), mu_misc, vecs, w2p, a2p, rwkv_g2[layer].astype(bf16), v2p, b, t)

        moe_layer = layer % 2 == 1
        i = layer // 2
        router = (_router_pack(moe_router[i]), _row(moe_router_b[i], LANES)) if moe_layer else None
        merged = _merge(y_a, y_b, y_c, z, xcur, w_branch[layer].astype(bf16), w_o[layer].astype(bf16), mod,
                        _row(ln1_g[layer]), _row(ln1_b[layer]), t, f32 if moe_layer else bf16, router)
        xcur, h2 = merged[:2]
        if not moe_layer:
            mod_next = mods[layer + 1] if layer + 1 < depth else mod
            xcur, h = _ffn(h2, xcur, ffn_w_gate[i].astype(bf16), ffn_w_up[i].astype(bf16), ffn_w_down[i].astype(bf16),
                           mod, mod_next, _row(ln2_g[layer]), _row(ln2_b[layer]), t)
            out = xcur
        else:
            idx, gate = merged[2:]
            tables = _routing_tables(idx[:, :TOP_K], gate[:, :TOP_K], n)
            y2 = _moe(h2, *tables, moe_w_gate[i], moe_w_up[i], moe_w_down[i])
            out = _final_ln(xcur, y2, mod, _row(ln2_g[layer]), _row(ln2_b[layer]), t)
            xcur = out
    return out.reshape(b, t, d)
```

```python
import functools
import math

import jax
import jax.numpy as jnp
from jax import lax
from jax.experimental import pallas as pl
from jax.experimental.pallas import tpu as pltpu

f32 = jnp.float32
bf16 = jnp.bfloat16
i32 = jnp.int32

D_MODEL = 1024
DEPTH = 2
CHUNK = 64
GDN_HEADS = 4
GDN_DK = 128
MLSTM_HEADS = 4
MLSTM_DQK = 64
MLSTM_DV = 128
MLSTM_SOFTCAP = 15.0
RWKV_HEADS = 8
RWKV_DH = 64
RWKV_WIDTH = RWKV_HEADS * RWKV_DH
LORA_W = 64
LORA_A = 64
LORA_V = 32
LORA_G = 128
N_EXPERTS = 8
TOP_K = 2
DEEPNORM_ALPHA = (2 * DEPTH) ** 0.25
LN_EPS = 1e-5
RWKV_GN_EPS = 64e-5

LANES = 128
HALF = LANES // 2

Z_GDN = 0
Z_MLSTM = 2048
Z_RWKV = 3584
Z_GATE = 5120
Z_WIDTH = 8192
MISC_WIDTH = 512
MISC_SMALLS = 384
SM_GDN_A, SM_GDN_B, SM_ML_I, SM_ML_F = 0, 4, 8, 12

STEP_CHUNKS = 4
MOE_ROWS = 1024
VMEM_LIMIT = 56 * 1024 * 1024


def _cparams(*sem):
    return pltpu.CompilerParams(dimension_semantics=sem, vmem_limit_bytes=VMEM_LIMIT)


def _dot(a, b):
    return jnp.dot(a.astype(bf16), b.astype(bf16), preferred_element_type=f32)


def _dot_nt(a, b):
    return lax.dot_general(a.astype(bf16), b.astype(bf16), (((1,), (1,)), ((), ())), preferred_element_type=f32)


def _dot_tn(a, b):
    return lax.dot_general(a.astype(bf16), b.astype(bf16), (((0,), (0,)), ((), ())), preferred_element_type=f32)


def _sigmoid(x):
    return 1.0 / (1.0 + jnp.exp(-x))


def _silu(x):
    return x * _sigmoid(x)


def _softplus(x):
    return jnp.maximum(x, 0.0) + jnp.log1p(jnp.exp(-jnp.abs(x)))


def _iota2(shape, axis):
    return lax.broadcasted_iota(i32, shape, axis)


def _cumsum_rows(x):
    n = x.shape[0]
    rows = _iota2(x.shape, 0)
    sh = 1
    while sh < n:
        x = x + jnp.where(rows >= sh, pltpu.roll(x, sh, 0), 0.0)
        sh *= 2
    return x


def _rows(*xs):
    return jnp.concatenate(xs, axis=0)


def _lanes(*xs):
    return jnp.concatenate(xs, axis=1)


def _half_masks(shape):
    lane = _iota2(shape, len(shape) - 1)
    return lane < HALF, lane >= HALF


def _stack2(x):
    lo, _ = _half_masks(x.shape)
    return _rows(jnp.where(lo, x, 0.0), jnp.where(lo, 0.0, x))


def _unstack2(x):
    c = x.shape[0] // 2
    return x[:c] + x[c:]


def _pair_masks(c):
    r = _iota2((2 * c, 2 * c), 0)
    q = _iota2((2 * c, 2 * c), 1)
    same = (r < c) == (q < c)
    rr = jnp.where(r < c, r, r - c)
    qq = jnp.where(q < c, q, q - c)
    return jnp.logical_and(same, rr >= qq), jnp.logical_and(same, rr > qq)


def _half_col(c, v0, v1):
    return jnp.where(_iota2((2 * c, 1), 0) < c, v0, v1)


def _ln_rows(r, g, b):
    mu = jnp.mean(r, -1, keepdims=True)
    d = r - mu
    var = jnp.mean(d * d, -1, keepdims=True)
    return d * lax.rsqrt(var + LN_EPS) * g + b


def _head64_sums(x):
    lo, _ = _half_masks((x.shape[0], LANES))
    outs = []
    for gi in range(x.shape[1] // LANES):
        xs = x[:, gi * LANES : (gi + 1) * LANES]
        s_lo = jnp.sum(jnp.where(lo, xs, 0.0), -1, keepdims=True)
        s_hi = jnp.sum(jnp.where(lo, 0.0, xs), -1, keepdims=True)
        outs.append(jnp.where(lo, s_lo, s_hi))
    return jnp.concatenate(outs, axis=1)


def _ada_body(c_ref, w_ref, b_ref, o_ref):
    o_ref[...] = _dot(_silu(c_ref[...]), w_ref[...]) + b_ref[...]


def _ada_mod(c_pad, w, b):
    n = w.shape[1]
    tn = n // 4
    return pl.pallas_call(
        _ada_body,
        grid=(n // tn,),
        in_specs=[
            pl.BlockSpec(c_pad.shape, lambda j: (0, 0)),
            pl.BlockSpec((w.shape[0], tn), lambda j: (0, j)),
            pl.BlockSpec((1, tn), lambda j: (0, j)),
        ],
        out_specs=pl.BlockSpec((c_pad.shape[0], tn), lambda j: (0, j)),
        out_shape=jax.ShapeDtypeStruct((c_pad.shape[0], n), f32),
        compiler_params=_cparams("arbitrary"),
        name="ada_mod",
    )(c_pad, w, b)


def _entry_body(x_ref, g_ref, b_ref, sc_ref, sh_ref, xn_ref, h_ref):
    xn = _ln_rows(x_ref[...], g_ref[...], b_ref[...])
    xn_ref[...] = xn
    h_ref[...] = (xn * (1.0 + sc_ref[...]) + sh_ref[...]).astype(h_ref.dtype)


def _mod_spec(rows_per_batch, tm, slot):
    return pl.BlockSpec((None, 1, D_MODEL), lambda i: ((i * tm // rows_per_batch) * 6 + slot, 0, 0))


def _row_spec(tm, width=D_MODEL):
    return pl.BlockSpec((tm, width), lambda i: (i, 0))


def _vec_spec(width=D_MODEL):
    return pl.BlockSpec((1, width), lambda i: (0, 0))


def _entry_ln(x, g, b, mod, t):
    n = x.shape[0]
    tm = 512
    return pl.pallas_call(
        _entry_body,
        grid=(n // tm,),
        in_specs=[_row_spec(tm), _vec_spec(), _vec_spec(), _mod_spec(t, tm, 1), _mod_spec(t, tm, 0)],
        out_specs=[_row_spec(tm), _row_spec(tm)],
        out_shape=[jax.ShapeDtypeStruct((n, D_MODEL), f32), jax.ShapeDtypeStruct((n, D_MODEL), bf16)],
        compiler_params=_cparams("arbitrary"),
        name="entry_ln",
    )(x, g, b, mod, mod)


def _in_proj_body(x_ref, w_ref, wm_ref, z_ref, zm_ref):
    x = x_ref[...]
    z_ref[...] = jnp.dot(x, w_ref[...], preferred_element_type=f32).astype(z_ref.dtype)

    @pl.when(pl.program_id(1) == 0)
    def _():
        zm_ref[...] = jnp.dot(x, wm_ref[...], preferred_element_type=f32)


def _in_proj(x, w, w_misc, tm, tn):
    m, k = x.shape
    n = w.shape[1]
    nm = w_misc.shape[1]
    return pl.pallas_call(
        _in_proj_body,
        grid=(m // tm, n // tn),
        in_specs=[pl.BlockSpec((tm, k), lambda i, j: (i, 0)), pl.BlockSpec((k, tn), lambda i, j: (0, j)),
                  pl.BlockSpec((k, nm), lambda i, j: (0, 0))],
        out_specs=[pl.BlockSpec((tm, tn), lambda i, j: (i, j)), pl.BlockSpec((tm, nm), lambda i, j: (i, 0))],
        out_shape=[jax.ShapeDtypeStruct((m, n), bf16), jax.ShapeDtypeStruct((m, nm), f32)],
        compiler_params=_cparams("arbitrary", "arbitrary"),
        name="in_proj",
    )(x, w, w_misc)


def _seq_spec(nb, c, width, col_block):
    return pl.BlockSpec((nb, c, width), lambda n: (0, n, col_block))


def _bcast_spec(shape):
    return pl.BlockSpec(shape, lambda n: (0,) * len(shape))


def _lockstep(*chains):
    pending = [(g, None) for g in chains]
    while pending:
        requests = []
        for g, value in pending:
            try:
                requests.append((g, g.send(value)))
            except StopIteration:
                pass
        pending = [(g, [thunk() for thunk in thunks]) for g, thunks in requests]


def _unit_lower_inverses_staged(ps, nilpotent):
    n = ps[0].shape[0]
    eye = (_iota2((n, n), 0) == _iota2((n, n), 1)).astype(f32)
    ts = [eye + p for p in ps]
    ps = [p.astype(bf16) for p in ps]
    for _ in range(int(math.log2(nilpotent)) - 1):
        ps = [p.astype(bf16) for p in (yield [functools.partial(_dot, p, p) for p in ps])]
        prods = yield [functools.partial(_dot, t, p) for t, p in zip(ts, ps)]
        ts = [t + d for t, d in zip(ts, prods)]
    return ts


def _gdn_chains(qkv_ref, zg_ref, sm_ref, conv_ref, par_ref, ng_ref, o_ref, s_ref, buf_ref):
    nb, step_rows, w3 = qkv_ref.shape
    c = CHUNK

    @pl.when(pl.program_id(0) == 0)
    def _():
        s_ref[...] = jnp.zeros(s_ref.shape, f32)
        buf_ref[:, 0:8, :] = jnp.zeros((nb, 8, w3), f32)

    incl, strict = _pair_masks(c)
    pairs = range(GDN_HEADS // 2)
    heads = [(p, e) for p in pairs for e in range(2)]

    def col(a, lane):
        return a[:, lane : lane + 1]

    def chain(bi, s):
        rs = slice(s * c, (s + 1) * c)
        raw = qkv_ref[bi, rs, :].astype(f32)
        buf_ref[bi, 8 : 8 + c, :] = raw
        conv = conv_ref[3:4, :] * raw
        for j in range(3):
            conv = conv + conv_ref[j : j + 1, :] * buf_ref[bi, 5 + j : 5 + j + c, :]
        buf_ref[bi, 0:8, :] = buf_ref[bi, c : c + 8, :]
        x = _silu(conv)

        sm = sm_ref[bi, rs, :]
        gcum = _cumsum_rows(-jnp.exp(par_ref[0:1, :]) * _softplus(sm + par_ref[1:2, :]))
        gcum_t = gcum.T
        beta_all = _sigmoid(sm)
        zg = zg_ref[bi, rs, :].astype(f32)

        def head(i, h):
            return x[:, i * 512 + h * 128 : i * 512 + (h + 1) * 128]

        k_st, kb_st, q_st, v_st, g_st, eg_st, beta_st, decay, g_last = [], [], [], [], [], [], [], [], []
        for p in pairs:
            h0, h1 = 2 * p, 2 * p + 1
            qs, ks = [], []
            for h in (h0, h1):
                q = head(0, h)
                k = head(1, h)
                qs.append(q * (lax.rsqrt(jnp.sum(q * q, -1, keepdims=True) + 1e-6) * GDN_DK**-0.5))
                ks.append(k * lax.rsqrt(jnp.sum(k * k, -1, keepdims=True) + 1e-6))
            g = _rows(col(gcum, SM_GDN_A + h0), col(gcum, SM_GDN_A + h1))
            g_row = _lanes(gcum_t[SM_GDN_A + h0 : SM_GDN_A + h0 + 1, :], gcum_t[SM_GDN_A + h1 : SM_GDN_A + h1 + 1, :])
            beta = _rows(col(beta_all, SM_GDN_B + h0), col(beta_all, SM_GDN_B + h1))
            k2 = _rows(*ks)
            k_st.append(k2)
            kb_st.append(k2 * beta)
            q_st.append(_rows(*qs))
            v_st.append(_rows(head(2, h0), head(2, h1)))
            g_st.append(g)
            eg_st.append(jnp.exp(g))
            beta_st.append(beta)
            decay.append(jnp.where(incl, jnp.exp(g - g_row), 0.0))
            g_last.append((gcum[c - 1 : c, SM_GDN_A + h0 : SM_GDN_A + h0 + 1], gcum[c - 1 : c, SM_GDN_A + h1 : SM_GDN_A + h1 + 1]))

        m1 = yield [functools.partial(_dot_nt, _rows(kb_st[p], q_st[p]), k_st[p]) for p in pairs]
        t_inv = yield from _unit_lower_inverses_staged(
            [-jnp.where(strict, m1[p][: 2 * c] * decay[p], 0.0) for p in pairs], c)
        qk = [(m1[p][2 * c :] * decay[p]).astype(bf16) for p in pairs]
        uw = yield [functools.partial(_dot, t_inv[p], _lanes(v_st[p] * beta_st[p], kb_st[p] * eg_st[p])) for p in pairs]
        qd = [q_st[p] * eg_st[p] for p in pairs]
        k_dec = [k_st[p][e * c : (e + 1) * c] * jnp.exp(g_last[p][e] - g_st[p][e * c : (e + 1) * c]) for p, e in heads]
        for _ in range(2 * s):
            yield []
        s_old = [s_ref[bi, 2 * p + e] for p, e in heads]
        ws_qs = yield [functools.partial(_dot, _rows(uw[p][e * c : (e + 1) * c, 128:], qd[p][e * c : (e + 1) * c]), st)
                       for (p, e), st in zip(heads, s_old)]
        v_new = [uw[p][e * c : (e + 1) * c, :128] - m[:c] for (p, e), m in zip(heads, ws_qs)]
        prods = yield ([functools.partial(_dot, qk[p], _rows(v_new[2 * p], v_new[2 * p + 1])) for p in pairs]
                       + [functools.partial(_dot_tn, kd, vn) for kd, vn in zip(k_dec, v_new)])
        o_st, s_add = prods[: len(pairs)], prods[len(pairs):]
        for i, (p, e) in enumerate(heads):
            h = 2 * p + e
            s_ref[bi, h] = s_old[i] * jnp.exp(g_last[p][e]) + s_add[i]
            o = ws_qs[i][c:] + o_st[p][e * c : (e + 1) * c]
            o = o * lax.rsqrt(jnp.mean(o * o, -1, keepdims=True) + 1e-6) * ng_ref[...]
            o_ref[bi, rs, h * 128 : (h + 1) * 128] = (o * _silu(zg[:, h * 128 : (h + 1) * 128])).astype(o_ref.dtype)

    return [chain(bi, s) for s in range(step_rows // c) for bi in range(nb)]


def _mlstm_chains(qk_ref, v_ref, og_ref, sm_ref, par_ref, ng_ref, o_ref, cs_ref, m_ref):
    nb, step_rows, _ = qk_ref.shape
    c = CHUNK
    assert c == HALF

    @pl.when(pl.program_id(0) == 0)
    def _():
        cs_ref[...] = jnp.zeros(cs_ref.shape, f32)
        m_ref[...] = jnp.zeros(m_ref.shape, f32)

    def cap(x):
        return MLSTM_SOFTCAP * jnp.tanh(x / MLSTM_SOFTCAP)

    incl, _ = _pair_masks(c)
    ones = jnp.ones((2 * c, LANES), f32)
    pairs = range(MLSTM_HEADS // 2)

    def full(x):
        return jnp.broadcast_to(x, (2 * c, LANES))

    def col(a, lane):
        return a[:, lane : lane + 1]

    def row(a, lane):
        return a[lane : lane + 1, :]

    def chain(bi, s):
        rs = slice(s * c, (s + 1) * c)
        capped = cap(sm_ref[bi, rs, :] + par_ref[0:1, :])
        log_i = capped
        bcum = _cumsum_rows(-_softplus(-capped))
        log_i_t = log_i.T
        bcum_t = bcum.T
        qk = qk_ref[bi, rs, :].astype(f32)
        v_all = v_ref[bi, rs, :].astype(f32)
        og = og_ref[bi, rs, :].astype(f32)
        q_st, k_st, v_st, bc, li, b_last, dmat, m_intra = [], [], [], [], [], [], [], []
        for p in pairs:
            h0, h1 = 2 * p, 2 * p + 1
            q_st.append(_stack2(qk[:, p * 128 : (p + 1) * 128]))
            k_st.append(_stack2(qk[:, 256 + p * 128 : 256 + (p + 1) * 128] * MLSTM_DQK**-0.5))
            v_st.append(_rows(v_all[:, h0 * 128 : (h0 + 1) * 128], v_all[:, h1 * 128 : (h1 + 1) * 128]))
            bc.append(full(_rows(col(bcum, SM_ML_F + h0), col(bcum, SM_ML_F + h1))))
            li.append(full(_rows(col(log_i, SM_ML_I + h0), col(log_i, SM_ML_I + h1))))
            b_row = _lanes(row(bcum_t, SM_ML_F + h0), row(bcum_t, SM_ML_F + h1))
            li_row = _lanes(row(log_i_t, SM_ML_I + h0), row(log_i_t, SM_ML_I + h1))
            b_last.append(full(_half_col(c, bcum[c - 1 : c, SM_ML_F + h0 : SM_ML_F + h0 + 1],
                                         bcum[c - 1 : c, SM_ML_F + h1 : SM_ML_F + h1 + 1])))
            log_d = jnp.where(incl, bc[p] - b_row + li_row, -jnp.inf)
            m_intra.append(full(jnp.max(log_d, -1, keepdims=True)))
            dmat.append(jnp.exp(log_d - m_intra[p]))

        qk_t = yield [functools.partial(_dot_nt, q_st[p], k_st[p]) for p in pairs]
        pm = [(dmat[p] * qk_t[p]).astype(bf16) for p in pairs]
        v_ext = [_lanes(v_st[p], ones) for p in pairs]
        intra = yield [functools.partial(_dot, pm[p], v_ext[p]) for p in pairs]
        for _ in range(s):
            yield []
        cs_old = [cs_ref[bi, p] for p in pairs]
        m_old = [_half_col(c, m_ref[bi, 2 * p : 2 * p + 1, :], m_ref[bi, 2 * p + 1 : 2 * p + 2, :]) for p in pairs]
        kw_st, s_old, m_new = [], [], []
        for p in pairs:
            log_e = b_last[p] - bc[p] + li[p]
            m_end = _half_col(c, jnp.max(log_e[:c], 0, keepdims=True), jnp.max(log_e[c:], 0, keepdims=True))
            m_new.append(jnp.maximum(b_last[p] + m_old[p], m_end))
            s_old.append(jnp.exp(b_last[p] + m_old[p] - m_new[p]))
            kw_st.append(k_st[p] * (jnp.exp(log_e - m_end) * jnp.exp(m_end - m_new[p])))
        prods = yield ([functools.partial(_dot, q_st[p], cs_old[p]) for p in pairs]
                       + [functools.partial(_dot_tn, kw_st[p], v_ext[p]) for p in pairs])
        q_cs, cs_add = prods[: len(pairs)], prods[len(pairs):]
        for p in pairs:
            m_t = jnp.maximum(bc[p] + m_old[p], m_intra[p])
            s_inter = jnp.exp(bc[p] + m_old[p] - m_t)
            s_intra = jnp.exp(m_intra[p] - m_t)
            num = s_inter * q_cs[p][:, :LANES] + s_intra * intra[p][:, :LANES]
            den = s_inter * q_cs[p][:, LANES:] + s_intra * intra[p][:, LANES:]
            hc = num / jnp.maximum(jnp.abs(den), jnp.exp(-m_t))
            cs_ref[bi, p] = _lanes(s_old[p], s_old[p]) * cs_old[p] + cs_add[p]
            for e in range(2):
                h = 2 * p + e
                m_ref[bi, h : h + 1, :] = m_new[p][e * c : e * c + 1]
                hs = hc[e * c : (e + 1) * c]
                hs = hs * lax.rsqrt(jnp.mean(hs * hs, -1, keepdims=True) + 1e-6) * ng_ref[:, h * 128 : (h + 1) * 128]
                o_ref[bi, rs, h * 128 : (h + 1) * 128] = (hs * _sigmoid(og[:, h * 128 : (h + 1) * 128])).astype(o_ref.dtype)

    return [chain(bi, s) for s in range(step_rows // c) for bi in range(nb)]


def _rwkv_chains(r_ref, k_ref, v_ref, misc_ref, vf_ref, mu_ref, mum_ref, vecs_ref, w2_ref, a2_ref, g2_ref, v2_ref,
                 y_ref, vf_out_ref, st_ref, tail_ref, tailm_ref):
    has_vres = vf_ref is not None
    nb, step_rows, _ = r_ref.shape
    c = CHUNK

    @pl.when(pl.program_id(0) == 0)
    def _():
        st_ref[...] = jnp.zeros(st_ref.shape, f32)
        tail_ref[:, 0:8, :] = jnp.zeros((nb, 8, tail_ref.shape[2]), f32)
        tailm_ref[:, 0:8, :] = jnp.zeros((nb, 8, tailm_ref.shape[2]), f32)

    incl, strict = _pair_masks(c)
    same_head = (_iota2((LANES, LANES), 0) < HALF) == (_iota2((LANES, LANES), 1) < HALF)
    pairs = range(RWKV_HEADS // 2)
    sl = [slice(p * 128, (p + 1) * 128) for p in pairs]
    c2 = 2 * c
    w0, a0, k_k, k_a, r_k, lnx_g, lnx_b, v0 = (vecs_ref[i : i + 1, :] for i in range(8))

    def chain(bi, s):
        rs = slice(s * c, (s + 1) * c)
        raw = jnp.concatenate([r_ref[bi, rs, :], k_ref[bi, rs, :], v_ref[bi, rs, :]], axis=1).astype(f32)
        tail_ref[bi, 8 : 8 + c, :] = raw
        prev = tail_ref[bi, 7 : 7 + c, :]
        tail_ref[bi, 0:8, :] = tail_ref[bi, c : c + 8, :]
        rkv = raw + mu_ref[...] * (prev - raw)
        rawm = misc_ref[bi, rs, :]
        tailm_ref[bi, 8 : 8 + c, :] = rawm
        prevm = tailm_ref[bi, 7 : 7 + c, :]
        tailm_ref[bi, 0:8, :] = tailm_ref[bi, c : c + 8, :]
        misc = rawm + mum_ref[...] * (prevm - rawm)

        r = rkv[:, 0:512]
        k = rkv[:, 512:1024]
        v = rkv[:, 1024:1536]
        wa = misc[:, 0:128]
        lora = [functools.partial(_dot, jnp.tanh(wa), w2_ref[...]), functools.partial(_dot, wa, a2_ref[...]),
                functools.partial(_dot, _sigmoid(misc[:, 128:256]), g2_ref[...])]
        if has_vres:
            lora.append(functools.partial(_dot, misc[:, 256:384], v2_ref[...]))
        lora = yield lora
        log_w = -jnp.exp(-_softplus(-(w0 + lora[0])) - 0.5)
        a = _sigmoid(a0 + lora[1])
        g = lora[2]
        if has_vres:
            v = v + (vf_ref[bi, rs, :] - v) * _sigmoid(v0 + lora[3])
        else:
            vf_out_ref[bi, rs, :] = v
        kk = k * k_k
        kk = kk * lax.rsqrt(_head64_sums(kk * kk) + 1e-6)
        k = k * (1.0 + (a - 1.0) * k_a)
        alpha = -kk
        beta = kk * a
        lw = _cumsum_rows(log_w)
        lw_last = lw[c - 1 : c, :]
        inv = jnp.exp(-lw)
        al_bar = alpha * jnp.exp(lw - log_w)
        be_hat = beta * inv
        k_hat = k * inv
        r_bar = r * jnp.exp(lw)
        to_end = jnp.exp(lw_last - lw)
        be_end = beta * to_end
        k_end = k * to_end
        g_chunk = jnp.exp(lw_last)

        al_st = [_stack2(al_bar[:, s]).astype(bf16) for s in sl]
        v_st = [_stack2(v[:, s]).astype(bf16) for s in sl]
        m1 = yield [functools.partial(_dot_nt, _rows(al_st[p], _stack2(r_bar[:, s]).astype(bf16)),
                                      _rows(be_hat[:, s], be_hat[:, s], k_hat[:, s], k_hat[:, s]))
                    for p, s in zip(pairs, sl)]
        t_inv = yield from _unit_lower_inverses_staged([jnp.where(strict, m[:c2, :c2], 0.0) for m in m1], c)
        l_ak = [jnp.where(strict, m[:c2, c2:], 0.0).astype(bf16) for m in m1]
        q_rb = [jnp.where(incl, m[c2:, :c2], 0.0).astype(bf16) for m in m1]
        q_rk = [jnp.where(incl, m[c2:, c2:], 0.0).astype(bf16) for m in m1]
        m2 = yield [functools.partial(_dot, _rows(l_ak[p], q_rk[p]), v_st[p]) for p in pairs]
        m3 = yield [functools.partial(_dot, t_inv[p], _lanes(al_st[p], m2[p][:c2].astype(bf16))) for p in pairs]
        for _ in range(2 * s):
            yield []
        st_old = [st_ref[bi, p] for p in pairs]
        m4 = yield [functools.partial(_dot_nt, _rows(_unstack2(m3[p][:, :128]), r_bar[:, sl[p]]), st_old[p]) for p in pairs]
        u = [m4[p][:c] + _unstack2(m3[p][:, 128:]) for p in pairs]
        m56 = yield ([functools.partial(_dot, q_rb[p], _stack2(u[p])) for p in pairs]
                     + [functools.partial(_dot_tn, _rows(u[p], v[:, sl[p]]), _rows(be_end[:, sl[p]], k_end[:, sl[p]]))
                        for p in pairs])
        m5, m6 = m56[: len(pairs)], m56[len(pairs):]
        ys = []
        for p in pairs:
            st_ref[bi, p] = g_chunk[:, sl[p]] * st_old[p] + jnp.where(same_head, m6[p], 0.0)
            ys.append(m4[p][c:] + _unstack2(m5[p]) + _unstack2(m2[p][c2:]))
        y = jnp.concatenate(ys, axis=1)
        mean = _head64_sums(y) * (1.0 / RWKV_DH)
        d = y - mean
        var = _head64_sums(d * d) * (1.0 / RWKV_DH)
        y = d * lax.rsqrt(var + RWKV_GN_EPS) * lnx_g + lnx_b
        bonus = _head64_sums(r * k * r_k) * v
        y_ref[bi, rs, :] = ((y + bonus) * g).astype(y_ref.dtype)

    return [chain(bi, s) for s in range(step_rows // c) for bi in range(nb)]


def _mixers_body(*refs, has_vres):
    it = iter(refs)
    take = lambda n: [next(it) for _ in range(n)]
    qkv, zg, sm, conv, gpar, gng, qk, v, og, mpar, mng, r, k, v3, misc = take(15)
    vf = next(it) if has_vres else None
    mu, mum, vecs, w2, a2, g2 = take(6)
    v2 = next(it) if has_vres else None
    ya, yb, yc = take(3)
    vf_out = None if has_vres else next(it)
    s, buf, cs, m, st, tail, tailm = take(7)
    _lockstep(*_rwkv_chains(r, k, v3, misc, vf, mu, mum, vecs, w2, a2, g2, v2, yc, vf_out, st, tail, tailm),
              *_gdn_chains(qkv, zg, sm, conv, gpar, gng, ya, s, buf),
              *_mlstm_chains(qk, v, og, sm, mpar, mng, yb, cs, m))


def _mixers(z, zm, v_first, gdn_args, mlstm_args, rwkv_args, b, t):
    c, rows = CHUNK, STEP_CHUNKS * CHUNK
    has_vres = v_first is not None
    z3 = z.reshape(b, t, z.shape[1])
    zm3 = zm.reshape(b, t, zm.shape[1])
    blk = _seq_spec(b, rows, 512, 0)
    zblk = lambda off: _seq_spec(b, rows, 512, off // 512)
    conv_w, gdn_par, gdn_ng = gdn_args
    ml_par, ml_ng = mlstm_args
    mu, mu_misc, vecs, w2p, a2p, g2, v2p = rwkv_args
    in_specs = [
        _seq_spec(b, rows, 1536, Z_GDN // 1536), zblk(Z_GDN + 1536), _seq_spec(b, rows, 128, MISC_SMALLS // 128),
        _bcast_spec((4, 1536)), _bcast_spec((8, 128)), _bcast_spec((1, 128)),
        zblk(Z_MLSTM), zblk(Z_MLSTM + 512), zblk(Z_MLSTM + 1024), _bcast_spec((8, 128)), _bcast_spec((1, 512)),
        zblk(Z_RWKV), zblk(Z_RWKV + 512), zblk(Z_RWKV + 1024), blk,
    ]
    args = [z3, z3, zm3, conv_w, gdn_par, gdn_ng, z3, z3, z3, ml_par, ml_ng, z3, z3, z3, zm3]
    if has_vres:
        in_specs.append(blk)
        args.append(v_first.reshape(b, t, 512))
    in_specs += [_bcast_spec(sh) for sh in ((1, 1536), (1, 512), (8, 512), (128, 512), (128, 512), (128, 512))]
    args += [mu, mu_misc, vecs, w2p, a2p, g2]
    if has_vres:
        in_specs.append(_bcast_spec((128, 512)))
        args.append(v2p)
    y_sd = jax.ShapeDtypeStruct((b, t, 512), bf16)
    res = pl.pallas_call(
        functools.partial(_mixers_body, has_vres=has_vres),
        grid=(t // rows,),
        in_specs=in_specs,
        out_specs=[blk] * (3 if has_vres else 4),
        out_shape=[y_sd] * 3 + ([] if has_vres else [jax.ShapeDtypeStruct((b, t, 512), f32)]),
        scratch_shapes=[
            pltpu.VMEM((b, GDN_HEADS, 128, 128), f32), pltpu.VMEM((b, c + 8, 1536), f32),
            pltpu.VMEM((b, MLSTM_HEADS // 2, 128, 256), f32), pltpu.VMEM((b, 8, 128), f32),
            pltpu.VMEM((b, RWKV_HEADS // 2, 128, 128), f32), pltpu.VMEM((b, c + 8, 1536), f32),
            pltpu.VMEM((b, c + 8, 512), f32),
        ],
        compiler_params=_cparams("arbitrary"),
        name="mixers_res" if has_vres else "mixers",
    )(*args)
    ys = [o.reshape(b * t, 512) for o in res]
    return ys[0], ys[1], ys[2], (v_first if has_vres else ys[3])


def _split3(x):
    hi = x.astype(bf16)
    r1 = x - hi.astype(f32)
    mid = r1.astype(bf16)
    return hi, mid, (r1 - mid.astype(f32)).astype(bf16)


def _router_pack(w):
    return _pad_cols(jnp.concatenate(_split3(w.astype(f32)), axis=1), LANES)


def _route(h, w3, b):
    h_hi, h_mid, h_lo = _split3(h)
    a = _dot(h_hi, w3)
    m = _dot(h_mid, w3)
    lo = _dot(h_lo, w3)
    down = lambda x, groups: pltpu.roll(x, LANES - groups * N_EXPERTS, 1)
    logits = a + down(a, 1) + down(a, 2) + m + down(m, 1) + lo + b
    lane = _iota2(logits.shape, 1)
    logits = jnp.where(lane < N_EXPERTS, logits, -jnp.inf)
    m1 = jnp.max(logits, -1, keepdims=True)
    i1 = jnp.min(jnp.where(logits == m1, lane, LANES), -1, keepdims=True)
    rest = jnp.where(lane == i1, -jnp.inf, logits)
    m2 = jnp.max(rest, -1, keepdims=True)
    i2 = jnp.min(jnp.where(rest == m2, lane, LANES), -1, keepdims=True)
    e = jnp.exp(m2 - m1)
    g1 = 1.0 / (1.0 + e)
    g2 = e / (1.0 + e)
    return (jnp.where(lane == 0, i1, jnp.where(lane == 1, i2, 0)),
            jnp.where(lane == 0, g1, jnp.where(lane == 1, g2, 0.0)))


def _merge_body(*refs, with_router):
    (ya_ref, yb_ref, yc_ref, ga_ref, gb_ref, gc_ref, x_ref, wb_ref, wo_ref, gt_ref, g_ref, b_ref,
     sc_ref, sh_ref) = refs[:14]
    merged = _sigmoid(ga_ref[...].astype(f32)) * _dot(ya_ref[...], wb_ref[0])
    merged = merged + _sigmoid(gb_ref[...].astype(f32)) * _dot(yb_ref[...], wb_ref[1])
    merged = merged + _sigmoid(gc_ref[...].astype(f32)) * _dot(yc_ref[...], wb_ref[2])
    y = _dot(merged, wo_ref[...])
    xn = _ln_rows(DEEPNORM_ALPHA * x_ref[...] + (1.0 + gt_ref[...]) * y, g_ref[...], b_ref[...])
    h = xn * (1.0 + sc_ref[...]) + sh_ref[...]
    if with_router:
        rw_ref, rb_ref, xn_ref, h_ref, idx_ref, gate_ref = refs[14:]
        idx_ref[...], gate_ref[...] = _route(h, rw_ref[...], rb_ref[...])
    else:
        xn_ref, h_ref = refs[14:]
    xn_ref[...] = xn
    h_ref[...] = h.astype(h_ref.dtype)


def _merge(ya, yb, yc, z, x, wb, wo, mod, ln_g, ln_b, t, h_dtype, router=None):
    n = x.shape[0]
    tm = 512
    gate = lambda i: pl.BlockSpec((tm, 1024), lambda r: (r, Z_GATE // 1024 + i))
    in_specs = [
        _row_spec(tm, 512), _row_spec(tm, 512), _row_spec(tm, 512), gate(0), gate(1), gate(2), _row_spec(tm),
        pl.BlockSpec((3, 512, D_MODEL), lambda r: (0, 0, 0)),
        pl.BlockSpec((D_MODEL, D_MODEL), lambda r: (0, 0)),
        _mod_spec(t, tm, 2), _vec_spec(), _vec_spec(), _mod_spec(t, tm, 4), _mod_spec(t, tm, 3),
    ]
    args = [ya, yb, yc, z, z, z, x, wb, wo, mod, ln_g, ln_b, mod, mod]
    out_specs = [_row_spec(tm), _row_spec(tm)]
    out_shape = [jax.ShapeDtypeStruct((n, D_MODEL), f32), jax.ShapeDtypeStruct((n, D_MODEL), h_dtype)]
    if router is not None:
        in_specs += [pl.BlockSpec((D_MODEL, LANES), lambda r: (0, 0)), _vec_spec(LANES)]
        args += list(router)
        out_specs += [_row_spec(tm, LANES), _row_spec(tm, LANES)]
        out_shape += [jax.ShapeDtypeStruct((n, LANES), i32), jax.ShapeDtypeStruct((n, LANES), f32)]
    return pl.pallas_call(
        functools.partial(_merge_body, with_router=router is not None),
        grid=(n // tm,),
        in_specs=in_specs,
        out_specs=out_specs,
        out_shape=out_shape,
        compiler_params=_cparams("arbitrary"),
        name="merge" if router is None else "merge_route",
    )(*args)


def _ffn_body(h_ref, x_ref, wg_ref, wu_ref, wd_ref, gt_ref, g_ref, b_ref, sc_ref, sh_ref, xn_ref, hn_ref, acc_ref):
    f = pl.program_id(1)

    @pl.when(f == 0)
    def _():
        acc_ref[...] = jnp.zeros(acc_ref.shape, f32)

    h = h_ref[...]
    act = _silu(_dot(h, wg_ref[...])) * _dot(h, wu_ref[...])
    acc_ref[...] += _dot(act, wd_ref[...])

    @pl.when(f == pl.num_programs(1) - 1)
    def _():
        xn = _ln_rows(DEEPNORM_ALPHA * x_ref[...] + (1.0 + gt_ref[...]) * acc_ref[...], g_ref[...], b_ref[...])
        xn_ref[...] = xn
        hn_ref[...] = (xn * (1.0 + sc_ref[...]) + sh_ref[...]).astype(hn_ref.dtype)


def _ffn(h, x, wg, wu, wd, mod, mod_next, ln_g, ln_b, t):
    n = x.shape[0]
    dff = wg.shape[1]
    tm, tf = 512, dff // 2
    row = pl.BlockSpec((tm, D_MODEL), lambda i, f: (i, 0))
    vec = pl.BlockSpec((1, D_MODEL), lambda i, f: (0, 0))
    modspec = lambda slot: pl.BlockSpec((None, 1, D_MODEL), lambda i, f: ((i * tm // t) * 6 + slot, 0, 0))
    return pl.pallas_call(
        _ffn_body,
        grid=(n // tm, dff // tf),
        in_specs=[
            row, row,
            pl.BlockSpec((D_MODEL, tf), lambda i, f: (0, f)),
            pl.BlockSpec((D_MODEL, tf), lambda i, f: (0, f)),
            pl.BlockSpec((tf, D_MODEL), lambda i, f: (f, 0)),
            modspec(5), vec, vec, modspec(1), modspec(0),
        ],
        out_specs=[row, row],
        out_shape=[jax.ShapeDtypeStruct((n, D_MODEL), f32), jax.ShapeDtypeStruct((n, D_MODEL), bf16)],
        scratch_shapes=[pltpu.VMEM((tm, D_MODEL), f32)],
        compiler_params=_cparams("arbitrary", "arbitrary"),
        name="ffn",
    )(h, x, wg, wu, wd, mod, ln_g, ln_b, mod_next, mod_next)


def _moe_body(be_ref, live_ref, tok_ref, tokn_ref, dst_ref, gate_ref, h_hbm, wg_ref, wu_ref, wd_ref, y_hbm,
              xbuf, xb16, acc_ref, obuf, gsem, ssem):
    j = pl.program_id(0)
    f = pl.program_id(1)
    nb = pl.num_programs(0)
    nf = pl.num_programs(1)
    rows = xbuf.shape[0]
    live = live_ref[j] > 0
    next_live = live_ref[jnp.minimum(j + 1, nb - 1)] > 0
    prev_live = jnp.logical_and(j > 0, live_ref[jnp.maximum(j - 1, 0)] > 0)

    def gather(ref, static):
        def start(r, carry):
            pltpu.make_async_copy(h_hbm.at[pl.ds(ref[0, 0, r], 1), :], xbuf.at[pl.ds(r, 1), :], gsem).start()
            return carry

        if static:
            for r in range(rows):
                start(r, 0)
        else:
            lax.fori_loop(0, rows, start, 0, unroll=8)

    def gather_wait():
        pltpu.make_async_copy(h_hbm.at[pl.ds(0, rows), :], xbuf, gsem).wait()

    def scatter_wait():
        pltpu.make_async_copy(obuf, y_hbm.at[pl.ds(0, rows), :], ssem).wait()

    @pl.when(jnp.logical_and(f == 0, j == 0))
    def _():
        obuf[...] = jnp.zeros(obuf.shape, f32)
        spare = pltpu.make_async_copy(obuf, y_hbm.at[pl.ds(y_hbm.shape[0] - rows, rows), :], ssem)
        spare.start()
        spare.wait()

    @pl.when(jnp.logical_and(f == 0, jnp.logical_and(j == 0, live)))
    def _():
        gather(tok_ref, static=False)

    @pl.when(jnp.logical_and(f == 0, live))
    def _():
        gather_wait()
        xb16[...] = xbuf[...].astype(bf16)
        acc_ref[...] = jnp.zeros(acc_ref.shape, f32)

    @pl.when(jnp.logical_and(f == 0, jnp.logical_and(live, jnp.logical_and(j + 1 < nb, next_live))))
    def _():
        gather(tokn_ref, static=True)

    half = rows // 2

    def swiglu(rs):
        x = xb16[rs, :]
        act = _silu(_dot(x, wg_ref[...])) * _dot(x, wu_ref[...])
        acc_ref[rs, :] += _dot(act, wd_ref[...])

    @pl.when(live_ref[j] > half)
    def _():
        swiglu(slice(0, rows))

    @pl.when(jnp.logical_and(live, live_ref[j] <= half))
    def _():
        swiglu(slice(0, half))

    @pl.when(jnp.logical_and(f == nf - 1, prev_live))
    def _():
        scatter_wait()

    @pl.when(jnp.logical_and(f == nf - 1, live))
    def _():
        obuf[...] = acc_ref[...] * gate_ref[...]
        for r in range(rows):
            pltpu.make_async_copy(obuf.at[pl.ds(r, 1), :], y_hbm.at[pl.ds(dst_ref[0, 0, r], 1), :], ssem).start()

    @pl.when(jnp.logical_and(f == nf - 1, jnp.logical_and(j == nb - 1, live)))
    def _():
        scatter_wait()


def _moe(h, block_e, block_live, row_tok, row_dst, row_gate, wg, wu, wd):
    n = h.shape[0]
    n_blocks = block_e.shape[0]
    rb = MOE_ROWS
    dff = wg.shape[2]
    tf = 512
    smem_blk =pl.BlockSpec((1, 1, rb), lambda j, f, be, lv: (j, 0, 0), memory_space=pltpu.SMEM)
    smem_next = pl.BlockSpec((1, 1, rb), lambda j, f, be, lv: (jnp.minimum(j + 1, n_blocks - 1), 0, 0),
                             memory_space=pltpu.SMEM)
    grid_spec = pltpu.PrefetchScalarGridSpec(
        num_scalar_prefetch=2,
        grid=(n_blocks, dff // tf),
        in_specs=[
            smem_blk,
            smem_next,
            smem_blk,
            pl.BlockSpec((rb, 1), lambda j, f, be, lv: (j, 0)),
            pl.BlockSpec(memory_space=pl.ANY),
            pl.BlockSpec((None, D_MODEL, tf), lambda j, f, be, lv: (be[j], 0, f)),
            pl.BlockSpec((None, D_MODEL, tf), lambda j, f, be, lv: (be[j], 0, f)),
            pl.BlockSpec((None, tf, D_MODEL), lambda j, f, be, lv: (be[j], f, 0)),
        ],
        out_specs=pl.BlockSpec(memory_space=pl.ANY),
        scratch_shapes=[
            pltpu.VMEM((rb, D_MODEL), f32),
            pltpu.VMEM((rb, D_MODEL), bf16),
            pltpu.VMEM((rb, D_MODEL), f32),
            pltpu.VMEM((rb, D_MODEL), f32),
            pltpu.SemaphoreType.DMA,
            pltpu.SemaphoreType.DMA,
        ],
    )
    return pl.pallas_call(
        _moe_body,
        grid_spec=grid_spec,
        out_shape=jax.ShapeDtypeStruct((TOP_K * n + rb, D_MODEL), f32),
        compiler_params=_cparams("arbitrary", "arbitrary"),
        name="moe",
    )(block_e, block_live, row_tok, row_tok, row_dst, row_gate, h, wg, wu, wd)


def _routing_tables(top_e, top_g, n):
    rb = MOE_ROWS
    n_assign = n * TOP_K
    n_blocks = -(-n_assign // rb) + N_EXPERTS
    rows = n_blocks * rb
    e_flat = top_e.reshape(-1)
    onehot = (e_flat[:, None] == jnp.arange(N_EXPERTS, dtype=i32)[None, :]).astype(i32)
    rank = jnp.sum((jnp.cumsum(onehot, axis=0) - onehot) * onehot, axis=1)
    counts = jnp.sum(onehot, axis=0)
    padded = (counts + rb - 1) // rb * rb
    pad_end = jnp.cumsum(padded)
    pad_start = pad_end - padded
    dest = jnp.sum(onehot * pad_start[None, :], axis=1) + rank
    row_a = jnp.full((rows,), n_assign, i32).at[dest].set(jnp.arange(n_assign, dtype=i32), unique_indices=True)
    valid = row_a < n_assign
    tok, slot_k = row_a // TOP_K, row_a % TOP_K
    row_tok = jnp.where(valid, tok, 0)
    row_dst = jnp.where(valid, slot_k * n + tok, n_assign + jnp.arange(rows, dtype=i32) % rb)
    row_gate = jnp.where(valid, top_g.reshape(-1)[jnp.minimum(row_a, n_assign - 1)], 0.0)
    starts = jnp.arange(n_blocks, dtype=i32) * rb
    block_e = jnp.minimum(jnp.sum((pad_end[None, :] <= starts[:, None]).astype(i32), axis=1), N_EXPERTS - 1)
    block_live = jnp.clip(jnp.sum((jnp.arange(N_EXPERTS)[None, :] == block_e[:, None]) * (pad_start + counts)[None, :],
                                  axis=1) - starts, 0, rb).astype(i32)
    block_live = jnp.where(starts < pad_end[-1], block_live, 0)
    return (block_e, block_live, row_tok.reshape(n_blocks, 1, rb), row_dst.reshape(n_blocks, 1, rb),
            row_gate.reshape(rows, 1))


def _final_body(x_ref, y0_ref, y1_ref, gt_ref, g_ref, b_ref, o_ref):
    y = y0_ref[...] + y1_ref[...]
    o_ref[...] = _ln_rows(DEEPNORM_ALPHA * x_ref[...] + (1.0 + gt_ref[...]) * y, g_ref[...], b_ref[...])


def _final_ln(x, y2, mod, ln_g, ln_b, t):
    n = x.shape[0]
    tm = 512
    nb = n // tm
    return pl.pallas_call(
        _final_body,
        grid=(nb,),
        in_specs=[_row_spec(tm), _row_spec(tm), pl.BlockSpec((tm, D_MODEL), lambda i: (i + nb, 0)),
                  _mod_spec(t, tm, 5), _vec_spec(), _vec_spec()],
        out_specs=_row_spec(tm),
        out_shape=jax.ShapeDtypeStruct((n, D_MODEL), f32),
        compiler_params=_cparams("arbitrary"),
        name="final_ln",
    )(x, y2, y2, mod, ln_g, ln_b)


def _pad_cols(w, width):
    return jnp.pad(w, ((0, 0), (0, width - w.shape[1])))


def _pad_rows(w, height):
    return jnp.pad(w, ((0, height - w.shape[0]), (0, 0)))


def _regroup_w_in(w_in, v1):
    d = w_in.shape[0]
    gdn, ml, rw = 0, 2056, 3600
    smalls = jnp.concatenate([w_in[:, gdn + 2048 : gdn + 2056], w_in[:, ml + 1536 : ml + 1544]], axis=1)
    vdown = jnp.zeros((d, LORA_V), w_in.dtype) if v1 is None else v1
    misc = jnp.concatenate([w_in[:, rw + 1536 : rw + 1792], _pad_cols(vdown, 128), _pad_cols(smalls, 128)], axis=1)
    cat = jnp.concatenate(
        [w_in[:, gdn : gdn + 2048], w_in[:, ml : ml + 1536], w_in[:, rw : rw + 1536], w_in[:, 5392:8464]], axis=1)
    assert cat.shape[1] == Z_WIDTH and misc.shape[1] == MISC_WIDTH
    return cat.astype(bf16), misc.astype(bf16)


def _row(v, width=None):
    v = v.reshape(1, -1).astype(f32)
    return v if width is None else _pad_cols(v, width)


def kernel(x, c, ln_in_g, ln_in_b, ada_w, ada_b, w_in, gdn_conv, gdn_a_log, gdn_dt_bias, gdn_norm_g, mlstm_b_i, mlstm_b_f, mlstm_norm_g, rwkv_mu, rwkv_w0, rwkv_w2, rwkv_a0, rwkv_a2, rwkv_g2, rwkv_k_k, rwkv_k_a, rwkv_r_k, rwkv_lnx_g, rwkv_lnx_b, rwkv_v1, rwkv_mu_v1, rwkv_v0, rwkv_v2, w_branch, w_o, ln1_g, ln1_b, ln2_g, ln2_b, ffn_w_gate, ffn_w_up, ffn_w_down, moe_router, moe_router_b, moe_w_gate, moe_w_up, moe_w_down):
    b, t, d = x.shape
    n = b * t
    depth = w_in.shape[0]
    xf = x.reshape(n, d)

    c_pad = _pad_rows(c.astype(f32), 8)
    mods = []
    for layer in range(depth):
        mod = _ada_mod(c_pad, ada_w[layer].astype(bf16), _row(ada_b[layer]))[:b]
        mods.append(mod.reshape(b * 6, 1, d))

    xcur, h = _entry_ln(xf, _row(ln_in_g), _row(ln_in_b), mods[0], t)
    v_first = None
    out = None
    for layer in range(depth):
        mod = mods[layer]
        has_vres = layer > 0
        w_main, w_misc = _regroup_w_in(w_in[layer], rwkv_v1[layer - 1] if has_vres else None)
        z, zm = _in_proj(h, w_main, w_misc, 2048, 1024)

        gdn_par = jnp.zeros((8, LANES), f32).at[0, 0:4].set(gdn_a_log[layer]).at[1, 0:4].set(gdn_dt_bias[layer])
        ml_par = jnp.zeros((8, LANES), f32).at[0, SM_ML_I : SM_ML_I + 4].set(mlstm_b_i[layer])
        ml_par = ml_par.at[0, SM_ML_F : SM_ML_F + 4].set(mlstm_b_f[layer])

        mu = rwkv_mu[layer].astype(f32)
        mu_v = rwkv_mu_v1[layer - 1].astype(f32) if has_vres else jnp.zeros((LORA_V,), f32)
        mu_misc = _row(jnp.concatenate([mu[1536:1792], mu_v]), 512)
        vecs = jnp.stack([
            rwkv_w0[layer], rwkv_a0[layer], rwkv_k_k[layer], rwkv_k_a[layer], rwkv_r_k[layer].reshape(-1),
            rwkv_lnx_g[layer], rwkv_lnx_b[layer],
            rwkv_v0[layer - 1] if has_vres else jnp.zeros((RWKV_WIDTH,), f32),
        ]).astype(f32)
        w2p = _pad_rows(rwkv_w2[layer], 128).astype(bf16)
        a2p = jnp.concatenate([jnp.zeros((LORA_W, RWKV_WIDTH), f32), rwkv_a2[layer]], axis=0).astype(bf16)
        v2p = _pad_rows(rwkv_v2[layer - 1], 128).astype(bf16) if has_vres else None
        y_a, y_b, y_c, v_first = _mixers(
            z, zm, v_first,
            (gdn_conv[layer].astype(f32), gdn_par, _row(gdn_norm_g[layer])),
            (ml_par, _row(mlstm_norm_g[layer])),
            (_row(mu[:1536]), mu_misc, vecs, w2p, a2p, rwkv_g2[layer].astype(bf16), v2p), b, t)

        moe_layer = layer % 2 == 1
        i = layer // 2
        router = (_router_pack(moe_router[i]), _row(moe_router_b[i], LANES)) if moe_layer else None
        merged = _merge(y_a, y_b, y_c, z, xcur, w_branch[layer].astype(bf16), w_o[layer].astype(bf16), mod,
                        _row(ln1_g[layer]), _row(ln1_b[layer]), t, f32 if moe_layer else bf16, router)
        xcur, h2 = merged[:2]
        if not moe_layer:
            mod_next = mods[layer + 1] if layer + 1 < depth else mod
            xcur, h = _ffn(h2, xcur, ffn_w_gate[i].astype(bf16), ffn_w_up[i].astype(bf16), ffn_w_down[i].astype(bf16),
                           mod, mod_next, _row(ln2_g[layer]), _row(ln2_b[layer]), t)
            out = xcur
        else:
            idx, gate = merged[2:]
            tables = _routing_tables(idx[:, :TOP_K], gate[:, :TOP_K], n)
            y2 = _moe(h2, *tables, moe_w_gate[i], moe_w_up[i], moe_w_down[i])
            out = _final_ln(xcur, y2, mod, _row(ln2_g[layer]), _row(ln2_b[layer]), t)
            xcur = out
    return out.reshape(b, t, d)
```

```python
import functools
import math

import jax
import jax.numpy as jnp
from jax import lax
from jax.experimental import pallas as pl
from jax.experimental.pallas import tpu as pltpu

f32 = jnp.float32
bf16 = jnp.bfloat16
i32 = jnp.int32

D_MODEL = 1024
DEPTH = 2
CHUNK = 64
GDN_HEADS = 4
GDN_DK = 128
MLSTM_HEADS = 4
MLSTM_DQK = 64
MLSTM_DV = 128
MLSTM_SOFTCAP = 15.0
RWKV_HEADS = 8
RWKV_DH = 64
RWKV_WIDTH = RWKV_HEADS * RWKV_DH
LORA_W = 64
LORA_A = 64
LORA_V = 32
LORA_G = 128
N_EXPERTS = 8
TOP_K = 2
DEEPNORM_ALPHA = (2 * DEPTH) ** 0.25
LN_EPS = 1e-5
RWKV_GN_EPS = 64e-5

LANES = 128
HALF = LANES // 2

Z_GDN = 0
Z_MLSTM = 2048
Z_RWKV = 3584
Z_GATE = 5120
Z_WIDTH = 8192
MISC_WIDTH = 512
MISC_SMALLS = 384
SM_GDN_A, SM_GDN_B, SM_ML_I, SM_ML_F = 0, 4, 8, 12

STEP_CHUNKS = 4
MOE_ROWS = 1024
VMEM_LIMIT = 56 * 1024 * 1024


def _cparams(*sem):
    return pltpu.CompilerParams(dimension_semantics=sem, vmem_limit_bytes=VMEM_LIMIT)


def _dot(a, b):
    return jnp.dot(a.astype(bf16), b.astype(bf16), preferred_element_type=f32)


def _dot_nt(a, b):
    return lax.dot_general(a.astype(bf16), b.astype(bf16), (((1,), (1,)), ((), ())), preferred_element_type=f32)


def _dot_tn(a, b):
    return lax.dot_general(a.astype(bf16), b.astype(bf16), (((0,), (0,)), ((), ())), preferred_element_type=f32)


def _sigmoid(x):
    return 1.0 / (1.0 + jnp.exp(-x))


def _silu(x):
    return x * _sigmoid(x)


def _softplus(x):
    return jnp.maximum(x, 0.0) + jnp.log1p(jnp.exp(-jnp.abs(x)))


def _iota2(shape, axis):
    return lax.broadcasted_iota(i32, shape, axis)


def _cumsum_rows(x):
    n = x.shape[0]
    rows = _iota2(x.shape, 0)
    sh = 1
    while sh < n:
        x = x + jnp.where(rows >= sh, pltpu.roll(x, sh, 0), 0.0)
        sh *= 2
    return x


def _rows(*xs):
    return jnp.concatenate(xs, axis=0)


def _lanes(*xs):
    return jnp.concatenate(xs, axis=1)


def _half_masks(shape):
    lane = _iota2(shape, len(shape) - 1)
    return lane < HALF, lane >= HALF


def _stack2(x):
    lo, _ = _half_masks(x.shape)
    return _rows(jnp.where(lo, x, 0.0), jnp.where(lo, 0.0, x))


def _unstack2(x):
    c = x.shape[0] // 2
    return x[:c] + x[c:]


def _pair_masks(c):
    r = _iota2((2 * c, 2 * c), 0)
    q = _iota2((2 * c, 2 * c), 1)
    same = (r < c) == (q < c)
    rr = jnp.where(r < c, r, r - c)
    qq = jnp.where(q < c, q, q - c)
    return jnp.logical_and(same, rr >= qq), jnp.logical_and(same, rr > qq)


def _half_col(c, v0, v1):
    return jnp.where(_iota2((2 * c, 1), 0) < c, v0, v1)


def _ln_rows(r, g, b):
    mu = jnp.mean(r, -1, keepdims=True)
    d = r - mu
    var = jnp.mean(d * d, -1, keepdims=True)
    return d * lax.rsqrt(var + LN_EPS) * g + b


def _head64_sums(x):
    lo, _ = _half_masks((x.shape[0], LANES))
    outs = []
    for gi in range(x.shape[1] // LANES):
        xs = x[:, gi * LANES : (gi + 1) * LANES]
        s_lo = jnp.sum(jnp.where(lo, xs, 0.0), -1, keepdims=True)
        s_hi = jnp.sum(jnp.where(lo, 0.0, xs), -1, keepdims=True)
        outs.append(jnp.where(lo, s_lo, s_hi))
    return jnp.concatenate(outs, axis=1)


def _ada_body(c_ref, w_ref, b_ref, o_ref):
    o_ref[...] = _dot(_silu(c_ref[...]), w_ref[...]) + b_ref[...]


def _ada_mod(c_pad, w, b):
    n = w.shape[1]
    tn = n // 4
    return pl.pallas_call(
        _ada_body,
        grid=(n // tn,),
        in_specs=[
            pl.BlockSpec(c_pad.shape, lambda j: (0, 0)),
            pl.BlockSpec((w.shape[0], tn), lambda j: (0, j)),
            pl.BlockSpec((1, tn), lambda j: (0, j)),
        ],
        out_specs=pl.BlockSpec((c_pad.shape[0], tn), lambda j: (0, j)),
        out_shape=jax.ShapeDtypeStruct((c_pad.shape[0], n), f32),
        compiler_params=_cparams("arbitrary"),
        name="ada_mod",
    )(c_pad, w, b)


def _entry_body(x_ref, g_ref, b_ref, sc_ref, sh_ref, xn_ref, h_ref):
    xn = _ln_rows(x_ref[...], g_ref[...], b_ref[...])
    xn_ref[...] = xn
    h_ref[...] = (xn * (1.0 + sc_ref[...]) + sh_ref[...]).astype(h_ref.dtype)


def _mod_spec(rows_per_batch, tm, slot):
    return pl.BlockSpec((None, 1, D_MODEL), lambda i: ((i * tm // rows_per_batch) * 6 + slot, 0, 0))


def _row_spec(tm, width=D_MODEL):
    return pl.BlockSpec((tm, width), lambda i: (i, 0))


def _vec_spec(width=D_MODEL):
    return pl.BlockSpec((1, width), lambda i: (0, 0))


def _entry_ln(x, g, b, mod, t):
    n = x.shape[0]
    tm = 512
    return pl.pallas_call(
        _entry_body,
        grid=(n // tm,),
        in_specs=[_row_spec(tm), _vec_spec(), _vec_spec(), _mod_spec(t, tm, 1), _mod_spec(t, tm, 0)],
        out_specs=[_row_spec(tm), _row_spec(tm)],
        out_shape=[jax.ShapeDtypeStruct((n, D_MODEL), f32), jax.ShapeDtypeStruct((n, D_MODEL), bf16)],
        compiler_params=_cparams("arbitrary"),
        name="entry_ln",
    )(x, g, b, mod, mod)


def _in_proj_body(x_ref, w_ref, wm_ref, z_ref, zm_ref):
    x = x_ref[...]
    z_ref[...] = jnp.dot(x, w_ref[...], preferred_element_type=f32).astype(z_ref.dtype)

    @pl.when(pl.program_id(1) == 0)
    def _():
        zm_ref[...] = jnp.dot(x, wm_ref[...], preferred_element_type=f32)


def _in_proj(x, w, w_misc, tm, tn):
    m, k = x.shape
    n = w.shape[1]
    nm = w_misc.shape[1]
    return pl.pallas_call(
        _in_proj_body,
        grid=(m // tm, n // tn),
        in_specs=[pl.BlockSpec((tm, k), lambda i, j: (i, 0)), pl.BlockSpec((k, tn), lambda i, j: (0, j)),
                  pl.BlockSpec((k, nm), lambda i, j: (0, 0))],
        out_specs=[pl.BlockSpec((tm, tn), lambda i, j: (i, j)), pl.BlockSpec((tm, nm), lambda i, j: (i, 0))],
        out_shape=[jax.ShapeDtypeStruct((m, n), bf16), jax.ShapeDtypeStruct((m, nm), f32)],
        compiler_params=_cparams("arbitrary", "arbitrary"),
        name="in_proj",
    )(x, w, w_misc)


def _seq_spec(nb, c, width, col_block):
    return pl.BlockSpec((nb, c, width), lambda n: (0, n, col_block))


def _bcast_spec(shape):
    return pl.BlockSpec(shape, lambda n: (0,) * len(shape))


def _lockstep(*chains):
    pending = [(g, None) for g in chains]
    while pending:
        requests = []
        for g, value in pending:
            try:
                requests.append((g, g.send(value)))
            except StopIteration:
                pass
        pending = [(g, [thunk() for thunk in thunks]) for g, thunks in requests]


def _unit_lower_inverses_staged(ps, nilpotent):
    n = ps[0].shape[0]
    eye = (_iota2((n, n), 0) == _iota2((n, n), 1)).astype(f32)
    ts = [eye + p for p in ps]
    ps = [p.astype(bf16) for p in ps]
    for _ in range(int(math.log2(nilpotent)) - 1):
        ps = [p.astype(bf16) for p in (yield [functools.partial(_dot, p, p) for p in ps])]
        prods = yield [functools.partial(_dot, t, p) for t, p in zip(ts, ps)]
        ts = [t + d for t, d in zip(ts, prods)]
    return ts


def _gdn_chains(qkv_ref, zg_ref, sm_ref, conv_ref, par_ref, ng_ref, o_ref, s_ref, buf_ref):
    nb, step_rows, w3 = qkv_ref.shape
    c = CHUNK

    @pl.when(pl.program_id(0) == 0)
    def _():
        s_ref[...] = jnp.zeros(s_ref.shape, f32)
        buf_ref[:, 0:8, :] = jnp.zeros((nb, 8, w3), f32)

    incl, strict = _pair_masks(c)
    pairs = range(GDN_HEADS // 2)
    heads = [(p, e) for p in pairs for e in range(2)]

    def col(a, lane):
        return a[:, lane : lane + 1]

    def chain(bi, s):
        rs = slice(s * c, (s + 1) * c)
        raw = qkv_ref[bi, rs, :].astype(f32)
        buf_ref[bi, 8 : 8 + c, :] = raw
        conv = conv_ref[3:4, :] * raw
        for j in range(3):
            conv = conv + conv_ref[j : j + 1, :] * buf_ref[bi, 5 + j : 5 + j + c, :]
        buf_ref[bi, 0:8, :] = buf_ref[bi, c : c + 8, :]
        x = _silu(conv)

        sm = sm_ref[bi, rs, :]
        gcum = _cumsum_rows(-jnp.exp(par_ref[0:1, :]) * _softplus(sm + par_ref[1:2, :]))
        gcum_t = gcum.T
        beta_all = _sigmoid(sm)
        zg = zg_ref[bi, rs, :].astype(f32)

        def head(i, h):
            return x[:, i * 512 + h * 128 : i * 512 + (h + 1) * 128]

        k_st, kb_st, q_st, v_st, g_st, eg_st, beta_st, decay, g_last = [], [], [], [], [], [], [], [], []
        for p in pairs:
            h0, h1 = 2 * p, 2 * p + 1
            qs, ks = [], []
            for h in (h0, h1):
                q = head(0, h)
                k = head(1, h)
                qs.append(q * (lax.rsqrt(jnp.sum(q * q, -1, keepdims=True) + 1e-6) * GDN_DK**-0.5))
                ks.append(k * lax.rsqrt(jnp.sum(k * k, -1, keepdims=True) + 1e-6))
            g = _rows(col(gcum, SM_GDN_A + h0), col(gcum, SM_GDN_A + h1))
            g_row = _lanes(gcum_t[SM_GDN_A + h0 : SM_GDN_A + h0 + 1, :], gcum_t[SM_GDN_A + h1 : SM_GDN_A + h1 + 1, :])
            beta = _rows(col(beta_all, SM_GDN_B + h0), col(beta_all, SM_GDN_B + h1))
            k2 = _rows(*ks)
            k_st.append(k2)
            kb_st.append(k2 * beta)
            q_st.append(_rows(*qs))
            v_st.append(_rows(head(2, h0), head(2, h1)))
            g_st.append(g)
            eg_st.append(jnp.exp(g))
            beta_st.append(beta)
            decay.append(jnp.where(incl, jnp.exp(g - g_row), 0.0))
            g_last.append((gcum[c - 1 : c, SM_GDN_A + h0 : SM_GDN_A + h0 + 1], gcum[c - 1 : c, SM_GDN_A + h1 : SM_GDN_A + h1 + 1]))

        m1 = yield [functools.partial(_dot_nt, _rows(kb_st[p], q_st[p]), k_st[p]) for p in pairs]
        t_inv = yield from _unit_lower_inverses_staged(
            [-jnp.where(strict, m1[p][: 2 * c] * decay[p], 0.0) for p in pairs], c)
        qk = [(m1[p][2 * c :] * decay[p]).astype(bf16) for p in pairs]
        uw = yield [functools.partial(_dot, t_inv[p], _lanes(v_st[p] * beta_st[p], kb_st[p] * eg_st[p])) for p in pairs]
        qd = [q_st[p] * eg_st[p] for p in pairs]
        k_dec = [k_st[p][e * c : (e + 1) * c] * jnp.exp(g_last[p][e] - g_st[p][e * c : (e + 1) * c]) for p, e in heads]
        for _ in range(2 * s):
            yield []
        s_old = [s_ref[bi, 2 * p + e] for p, e in heads]
        ws_qs = yield [functools.partial(_dot, _rows(uw[p][e * c : (e + 1) * c, 128:], qd[p][e * c : (e + 1) * c]), st)
                       for (p, e), st in zip(heads, s_old)]
        v_new = [uw[p][e * c : (e + 1) * c, :128] - m[:c] for (p, e), m in zip(heads, ws_qs)]
        prods = yield ([functools.partial(_dot, qk[p], _rows(v_new[2 * p], v_new[2 * p + 1])) for p in pairs]
                       + [functools.partial(_dot_tn, kd, vn) for kd, vn in zip(k_dec, v_new)])
        o_st, s_add = prods[: len(pairs)], prods[len(pairs):]
        for i, (p, e) in enumerate(heads):
            h = 2 * p + e
            s_ref[bi, h] = s_old[i] * jnp.exp(g_last[p][e]) + s_add[i]
            o = ws_qs[i][c:] + o_st[p][e * c : (e + 1) * c]
            o = o * lax.rsqrt(jnp.mean(o * o, -1, keepdims=True) + 1e-6) * ng_ref[...]
            o_ref[bi, rs, h * 128 : (h + 1) * 128] = (o * _silu(zg[:, h * 128 : (h + 1) * 128])).astype(o_ref.dtype)

    return [chain(bi, s) for s in range(step_rows // c) for bi in range(nb)]


def _mlstm_chains(qk_ref, v_ref, og_ref, sm_ref, par_ref, ng_ref, o_ref, cs_ref, m_ref):
    nb, step_rows, _ = qk_ref.shape
    c = CHUNK
    assert c == HALF

    @pl.when(pl.program_id(0) == 0)
    def _():
        cs_ref[...] = jnp.zeros(cs_ref.shape, f32)
        m_ref[...] = jnp.zeros(m_ref.shape, f32)

    def cap(x):
        return MLSTM_SOFTCAP * jnp.tanh(x / MLSTM_SOFTCAP)

    incl, _ = _pair_masks(c)
    ones = jnp.ones((2 * c, LANES), f32)
    pairs = range(MLSTM_HEADS // 2)

    def full(x):
        return jnp.broadcast_to(x, (2 * c, LANES))

    def col(a, lane):
        return a[:, lane : lane + 1]

    def row(a, lane):
        return a[lane : lane + 1, :]

    def chain(bi, s):
        rs = slice(s * c, (s + 1) * c)
        capped = cap(sm_ref[bi, rs, :] + par_ref[0:1, :])
        log_i = capped
        bcum = _cumsum_rows(-_softplus(-capped))
        log_i_t = log_i.T
        bcum_t = bcum.T
        qk = qk_ref[bi, rs, :].astype(f32)
        v_all = v_ref[bi, rs, :].astype(f32)
        og = og_ref[bi, rs, :].astype(f32)
        q_st, k_st, v_st, bc, li, b_last, dmat, m_intra = [], [], [], [], [], [], [], []
        for p in pairs:
            h0, h1 = 2 * p, 2 * p + 1
            q_st.append(_stack2(qk[:, p * 128 : (p + 1) * 128]))
            k_st.append(_stack2(qk[:, 256 + p * 128 : 256 + (p + 1) * 128] * MLSTM_DQK**-0.5))
            v_st.append(_rows(v_all[:, h0 * 128 : (h0 + 1) * 128], v_all[:, h1 * 128 : (h1 + 1) * 128]))
            bc.append(full(_rows(col(bcum, SM_ML_F + h0), col(bcum, SM_ML_F + h1))))
            li.append(full(_rows(col(log_i, SM_ML_I + h0), col(log_i, SM_ML_I + h1))))
            b_row = _lanes(row(bcum_t, SM_ML_F + h0), row(bcum_t, SM_ML_F + h1))
            li_row = _lanes(row(log_i_t, SM_ML_I + h0), row(log_i_t, SM_ML_I + h1))
            b_last.append(full(_half_col(c, bcum[c - 1 : c, SM_ML_F + h0 : SM_ML_F + h0 + 1],
                                         bcum[c - 1 : c, SM_ML_F + h1 : SM_ML_F + h1 + 1])))
            log_d = jnp.where(incl, bc[p] - b_row + li_row, -jnp.inf)
            m_intra.append(full(jnp.max(log_d, -1, keepdims=True)))
            dmat.append(jnp.exp(log_d - m_intra[p]))

        qk_t = yield [functools.partial(_dot_nt, q_st[p], k_st[p]) for p in pairs]
        pm = [(dmat[p] * qk_t[p]).astype(bf16) for p in pairs]
        v_ext = [_lanes(v_st[p], ones) for p in pairs]
        intra = yield [functools.partial(_dot, pm[p], v_ext[p]) for p in pairs]
        for _ in range(s):
            yield []
        cs_old = [cs_ref[bi, p] for p in pairs]
        m_old = [_half_col(c, m_ref[bi, 2 * p : 2 * p + 1, :], m_ref[bi, 2 * p + 1 : 2 * p + 2, :]) for p in pairs]
        kw_st, s_old, m_new = [], [], []
        for p in pairs:
            log_e = b_last[p] - bc[p] + li[p]
            m_end = _half_col(c, jnp.max(log_e[:c], 0, keepdims=True), jnp.max(log_e[c:], 0, keepdims=True))
            m_new.append(jnp.maximum(b_last[p] + m_old[p], m_end))
            s_old.append(jnp.exp(b_last[p] + m_old[p] - m_new[p]))
            kw_st.append(k_st[p] * (jnp.exp(log_e - m_end) * jnp.exp(m_end - m_new[p])))
        prods = yield ([functools.partial(_dot, q_st[p], cs_old[p]) for p in pairs]
                       + [functools.partial(_dot_tn, kw_st[p], v_ext[p]) for p in pairs])
        q_cs, cs_add = prods[: len(pairs)], prods[len(pairs):]
        for p in pairs:
            m_t = jnp.maximum(bc[p] + m_old[p], m_intra[p])
            s_inter = jnp.exp(bc[p] + m_old[p] - m_t)
            s_intra = jnp.exp(m_intra[p] - m_t)
            num = s_inter * q_cs[p][:, :LANES] + s_intra * intra[p][:, :LANES]
            den = s_inter * q_cs[p][:, LANES:] + s_intra * intra[p][:, LANES:]
            hc = num / jnp.maximum(jnp.abs(den), jnp.exp(-m_t))
            cs_ref[bi, p] = _lanes(s_old[p], s_old[p]) * cs_old[p] + cs_add[p]
            for e in range(2):
                h = 2 * p + e
                m_ref[bi, h : h + 1, :] = m_new[p][e * c : e * c + 1]
                hs = hc[e * c : (e + 1) * c]
                hs = hs * lax.rsqrt(jnp.mean(hs * hs, -1, keepdims=True) + 1e-6) * ng_ref[:, h * 128 : (h + 1) * 128]
                o_ref[bi, rs, h * 128 : (h + 1) * 128] = (hs * _sigmoid(og[:, h * 128 : (h + 1) * 128])).astype(o_ref.dtype)

    return [chain(bi, s) for s in range(step_rows // c) for bi in range(nb)]


def _rwkv_chains(r_ref, k_ref, v_ref, misc_ref, vf_ref, mu_ref, mum_ref, vecs_ref, w2_ref, a2_ref, g2_ref, v2_ref,
                 y_ref, vf_out_ref, st_ref, tail_ref, tailm_ref):
    has_vres = vf_ref is not None
    nb, step_rows, _ = r_ref.shape
    c = CHUNK

    @pl.when(pl.program_id(0) == 0)
    def _():
        st_ref[...] = jnp.zeros(st_ref.shape, f32)
        tail_ref[:, 0:8, :] = jnp.zeros((nb, 8, tail_ref.shape[2]), f32)
        tailm_ref[:, 0:8, :] = jnp.zeros((nb, 8, tailm_ref.shape[2]), f32)

    incl, strict = _pair_masks(c)
    same_head = (_iota2((LANES, LANES), 0) < HALF) == (_iota2((LANES, LANES), 1) < HALF)
    pairs = range(RWKV_HEADS // 2)
    sl = [slice(p * 128, (p + 1) * 128) for p in pairs]
    c2 = 2 * c
    w0, a0, k_k, k_a, r_k, lnx_g, lnx_b, v0 = (vecs_ref[i : i + 1, :] for i in range(8))

    def chain(bi, s):
        rs = slice(s * c, (s + 1) * c)
        raw = jnp.concatenate([r_ref[bi, rs, :], k_ref[bi, rs, :], v_ref[bi, rs, :]], axis=1).astype(f32)
        tail_ref[bi, 8 : 8 + c, :] = raw
        prev = tail_ref[bi, 7 : 7 + c, :]
        tail_ref[bi, 0:8, :] = tail_ref[bi, c : c + 8, :]
        rkv = raw + mu_ref[...] * (prev - raw)
        rawm = misc_ref[bi, rs, :]
        tailm_ref[bi, 8 : 8 + c, :] = rawm
        prevm = tailm_ref[bi, 7 : 7 + c, :]
        tailm_ref[bi, 0:8, :] = tailm_ref[bi, c : c + 8, :]
        misc = rawm + mum_ref[...] * (prevm - rawm)

        r = rkv[:, 0:512]
        k = rkv[:, 512:1024]
        v = rkv[:, 1024:1536]
        wa = misc[:, 0:128]
        lora = [functools.partial(_dot, jnp.tanh(wa), w2_ref[...]), functools.partial(_dot, wa, a2_ref[...]),
                functools.partial(_dot, _sigmoid(misc[:, 128:256]), g2_ref[...])]
        if has_vres:
            lora.append(functools.partial(_dot, misc[:, 256:384], v2_ref[...]))
        lora = yield lora
        log_w = -jnp.exp(-_softplus(-(w0 + lora[0])) - 0.5)
        a = _sigmoid(a0 + lora[1])
        g = lora[2]
        if has_vres:
            v = v + (vf_ref[bi, rs, :] - v) * _sigmoid(v0 + lora[3])
        else:
            vf_out_ref[bi, rs, :] = v
        kk = k * k_k
        kk = kk * lax.rsqrt(_head64_sums(kk * kk) + 1e-6)
        k = k * (1.0 + (a - 1.0) * k_a)
        alpha = -kk
        beta = kk * a
        lw = _cumsum_rows(log_w)
        lw_last = lw[c - 1 : c, :]
        inv = jnp.exp(-lw)
        al_bar = alpha * jnp.exp(lw - log_w)
        be_hat = beta * inv
        k_hat = k * inv
        r_bar = r * jnp.exp(lw)
        to_end = jnp.exp(lw_last - lw)
        be_end = beta * to_end
        k_end = k * to_end
        g_chunk = jnp.exp(lw_last)

        al_st = [_stack2(al_bar[:, s]).astype(bf16) for s in sl]
        v_st = [_stack2(v[:, s]).astype(bf16) for s in sl]
        m1 = yield [functools.partial(_dot_nt, _rows(al_st[p], _stack2(r_bar[:, s]).astype(bf16)),
                                      _rows(be_hat[:, s], be_hat[:, s], k_hat[:, s], k_hat[:, s]))
                    for p, s in zip(pairs, sl)]
        t_inv = yield from _unit_lower_inverses_staged([jnp.where(strict, m[:c2, :c2], 0.0) for m in m1], c)
        l_ak = [jnp.where(strict, m[:c2, c2:], 0.0).astype(bf16) for m in m1]
        q_rb = [jnp.where(incl, m[c2:, :c2], 0.0).astype(bf16) for m in m1]
        q_rk = [jnp.where(incl, m[c2:, c2:], 0.0).astype(bf16) for m in m1]
        m2 = yield [functools.partial(_dot, _rows(l_ak[p], q_rk[p]), v_st[p]) for p in pairs]
        m3 = yield [functools.partial(_dot, t_inv[p], _lanes(al_st[p], m2[p][:c2].astype(bf16))) for p in pairs]
        for _ in range(2 * s):
            yield []
        st_old = [st_ref[bi, p] for p in pairs]
        m4 = yield [functools.partial(_dot_nt, _rows(_unstack2(m3[p][:, :128]), r_bar[:, sl[p]]), st_old[p]) for p in pairs]
        u = [m4[p][:c] + _unstack2(m3[p][:, 128:]) for p in pairs]
        m56 = yield ([functools.partial(_dot, q_rb[p], _stack2(u[p])) for p in pairs]
                     + [functools.partial(_dot_tn, _rows(u[p], v[:, sl[p]]), _rows(be_end[:, sl[p]], k_end[:, sl[p]]))
                        for p in pairs])
        m5, m6 = m56[: len(pairs)], m56[len(pairs):]
        ys = []
        for p in pairs:
            st_ref[bi, p] = g_chunk[:, sl[p]] * st_old[p] + jnp.where(same_head, m6[p], 0.0)
            ys.append(m4[p][c:] + _unstack2(m5[p]) + _unstack2(m2[p][c2:]))
        y = jnp.concatenate(ys, axis=1)
        mean = _head64_sums(y) * (1.0 / RWKV_DH)
        d = y - mean
        var = _head64_sums(d * d) * (1.0 / RWKV_DH)
        y = d * lax.rsqrt(var + RWKV_GN_EPS) * lnx_g + lnx_b
        bonus = _head64_sums(r * k * r_k) * v
        y_ref[bi, rs, :] = ((y + bonus) * g).astype(y_ref.dtype)

    return [chain(bi, s) for s in range(step_rows // c) for bi in range(nb)]


def _mixers_body(*refs, has_vres):
    it = iter(refs)
    take = lambda n: [next(it) for _ in range(n)]
    qkv, zg, sm, conv, gpar, gng, qk, v, og, mpar, mng, r, k, v3, misc = take(15)
    vf = next(it) if has_vres else None
    mu, mum, vecs, w2, a2, g2 = take(6)
    v2 = next(it) if has_vres else None
    ya, yb, yc = take(3)
    vf_out = None if has_vres else next(it)
    s, buf, cs, m, st, tail, tailm = take(7)
    _lockstep(*_rwkv_chains(r, k, v3, misc, vf, mu, mum, vecs, w2, a2, g2, v2, yc, vf_out, st, tail, tailm),
              *_gdn_chains(qkv, zg, sm, conv, gpar, gng, ya, s, buf),
              *_mlstm_chains(qk, v, og, sm, mpar, mng, yb, cs, m))


def _mixers(z, zm, v_first, gdn_args, mlstm_args, rwkv_args, b, t):
    c, rows = CHUNK, STEP_CHUNKS * CHUNK
    has_vres = v_first is not None
    z3 = z.reshape(b, t, z.shape[1])
    zm3 = zm.reshape(b, t, zm.shape[1])
    blk = _seq_spec(b, rows, 512, 0)
    zblk = lambda off: _seq_spec(b, rows, 512, off // 512)
    conv_w, gdn_par, gdn_ng = gdn_args
    ml_par, ml_ng = mlstm_args
    mu, mu_misc, vecs, w2p, a2p, g2, v2p = rwkv_args
    in_specs = [
        _seq_spec(b, rows, 1536, Z_GDN // 1536), zblk(Z_GDN + 1536), _seq_spec(b, rows, 128, MISC_SMALLS // 128),
        _bcast_spec((4, 1536)), _bcast_spec((8, 128)), _bcast_spec((1, 128)),
        zblk(Z_MLSTM), zblk(Z_MLSTM + 512), zblk(Z_MLSTM + 1024), _bcast_spec((8, 128)), _bcast_spec((1, 512)),
        zblk(Z_RWKV), zblk(Z_RWKV + 512), zblk(Z_RWKV + 1024), blk,
    ]
    args = [z3, z3, zm3, conv_w, gdn_par, gdn_ng, z3, z3, z3, ml_par, ml_ng, z3, z3, z3, zm3]
    if has_vres:
        in_specs.append(blk)
        args.append(v_first.reshape(b, t, 512))
    in_specs += [_bcast_spec(sh) for sh in ((1, 1536), (1, 512), (8, 512), (128, 512), (128, 512), (128, 512))]
    args += [mu, mu_misc, vecs, w2p, a2p, g2]
    if has_vres:
        in_specs.append(_bcast_spec((128, 512)))
        args.append(v2p)
    y_sd = jax.ShapeDtypeStruct((b, t, 512), bf16)
    res = pl.pallas_call(
        functools.partial(_mixers_body, has_vres=has_vres),
        grid=(t // rows,),
        in_specs=in_specs,
        out_specs=[blk] * (3 if has_vres else 4),
        out_shape=[y_sd] * 3 + ([] if has_vres else [jax.ShapeDtypeStruct((b, t, 512), f32)]),
        scratch_shapes=[
            pltpu.VMEM((b, GDN_HEADS, 128, 128), f32), pltpu.VMEM((b, c + 8, 1536), f32),
            pltpu.VMEM((b, MLSTM_HEADS // 2, 128, 256), f32), pltpu.VMEM((b, 8, 128), f32),
            pltpu.VMEM((b, RWKV_HEADS // 2, 128, 128), f32), pltpu.VMEM((b, c + 8, 1536), f32),
            pltpu.VMEM((b, c + 8, 512), f32),
        ],
        compiler_params=_cparams("arbitrary"),
        name="mixers_res" if has_vres else "mixers",
    )(*args)
    ys = [o.reshape(b * t, 512) for o in res]
    return ys[0], ys[1], ys[2], (v_first if has_vres else ys[3])


def _split3(x):
    hi = x.astype(bf16)
    r1 = x - hi.astype(f32)
    mid = r1.astype(bf16)
    return hi, mid, (r1 - mid.astype(f32)).astype(bf16)


def _router_pack(w):
    return _pad_cols(jnp.concatenate(_split3(w.astype(f32)), axis=1), LANES)


def _route(h, w3, b):
    h_hi, h_mid, h_lo = _split3(h)
    a = _dot(h_hi, w3)
    m = _dot(h_mid, w3)
    lo = _dot(h_lo, w3)
    down = lambda x, groups: pltpu.roll(x, LANES - groups * N_EXPERTS, 1)
    logits = a + down(a, 1) + down(a, 2) + m + down(m, 1) + lo + b
    lane = _iota2(logits.shape, 1)
    logits = jnp.where(lane < N_EXPERTS, logits, -jnp.inf)
    m1 = jnp.max(logits, -1, keepdims=True)
    i1 = jnp.min(jnp.where(logits == m1, lane, LANES), -1, keepdims=True)
    rest = jnp.where(lane == i1, -jnp.inf, logits)
    m2 = jnp.max(rest, -1, keepdims=True)
    i2 = jnp.min(jnp.where(rest == m2, lane, LANES), -1, keepdims=True)
    e = jnp.exp(m2 - m1)
    g1 = 1.0 / (1.0 + e)
    g2 = e / (1.0 + e)
    return (jnp.where(lane == 0, i1, jnp.where(lane == 1, i2, 0)),
            jnp.where(lane == 0, g1, jnp.where(lane == 1, g2, 0.0)))


def _merge_body(*refs, with_router):
    (ya_ref, yb_ref, yc_ref, ga_ref, gb_ref, gc_ref, x_ref, wb_ref, wo_ref, gt_ref, g_ref, b_ref,
     sc_ref, sh_ref) = refs[:14]
    merged = _sigmoid(ga_ref[...].astype(f32)) * _dot(ya_ref[...], wb_ref[0])
    merged = merged + _sigmoid(gb_ref[...].astype(f32)) * _dot(yb_ref[...], wb_ref[1])
    merged = merged + _sigmoid(gc_ref[...].astype(f32)) * _dot(yc_ref[...], wb_ref[2])
    y = _dot(merged, wo_ref[...])
    xn = _ln_rows(DEEPNORM_ALPHA * x_ref[...] + (1.0 + gt_ref[...]) * y, g_ref[...], b_ref[...])
    h = xn * (1.0 + sc_ref[...]) + sh_ref[...]
    if with_router:
        rw_ref, rb_ref, xn_ref, h_ref, idx_ref, gate_ref = refs[14:]
        idx_ref[...], gate_ref[...] = _route(h, rw_ref[...], rb_ref[...])
    else:
        xn_ref, h_ref = refs[14:]
    xn_ref[...] = xn
    h_ref[...] = h.astype(h_ref.dtype)


def _merge(ya, yb, yc, z, x, wb, wo, mod, ln_g, ln_b, t, h_dtype, router=None):
    n = x.shape[0]
    tm = 512
    gate = lambda i: pl.BlockSpec((tm, 1024), lambda r: (r, Z_GATE // 1024 + i))
    in_specs = [
        _row_spec(tm, 512), _row_spec(tm, 512), _row_spec(tm, 512), gate(0), gate(1), gate(2), _row_spec(tm),
        pl.BlockSpec((3, 512, D_MODEL), lambda r: (0, 0, 0)),
        pl.BlockSpec((D_MODEL, D_MODEL), lambda r: (0, 0)),
        _mod_spec(t, tm, 2), _vec_spec(), _vec_spec(), _mod_spec(t, tm, 4), _mod_spec(t, tm, 3),
    ]
    args = [ya, yb, yc, z, z, z, x, wb, wo, mod, ln_g, ln_b, mod, mod]
    out_specs = [_row_spec(tm), _row_spec(tm)]
    out_shape = [jax.ShapeDtypeStruct((n, D_MODEL), f32), jax.ShapeDtypeStruct((n, D_MODEL), h_dtype)]
    if router is not None:
        in_specs += [pl.BlockSpec((D_MODEL, LANES), lambda r: (0, 0)), _vec_spec(LANES)]
        args += list(router)
        out_specs += [_row_spec(tm, LANES), _row_spec(tm, LANES)]
        out_shape += [jax.ShapeDtypeStruct((n, LANES), i32), jax.ShapeDtypeStruct((n, LANES), f32)]
    return pl.pallas_call(
        functools.partial(_merge_body, with_router=router is not None),
        grid=(n // tm,),
        in_specs=in_specs,
        out_specs=out_specs,
        out_shape=out_shape,
        compiler_params=_cparams("arbitrary"),
        name="merge" if router is None else "merge_route",
    )(*args)


def _ffn_body(h_ref, x_ref, wg_ref, wu_ref, wd_ref, gt_ref, g_ref, b_ref, sc_ref, sh_ref, xn_ref, hn_ref, acc_ref):
    f = pl.program_id(1)

    @pl.when(f == 0)
    def _():
        acc_ref[...] = jnp.zeros(acc_ref.shape, f32)

    h = h_ref[...]
    act = _silu(_dot(h, wg_ref[...])) * _dot(h, wu_ref[...])
    acc_ref[...] += _dot(act, wd_ref[...])

    @pl.when(f == pl.num_programs(1) - 1)
    def _():
        xn = _ln_rows(DEEPNORM_ALPHA * x_ref[...] + (1.0 + gt_ref[...]) * acc_ref[...], g_ref[...], b_ref[...])
        xn_ref[...] = xn
        hn_ref[...] = (xn * (1.0 + sc_ref[...]) + sh_ref[...]).astype(hn_ref.dtype)


def _ffn(h, x, wg, wu, wd, mod, mod_next, ln_g, ln_b, t):
    n = x.shape[0]
    dff = wg.shape[1]
    tm, tf = 512, dff // 2
    row = pl.BlockSpec((tm, D_MODEL), lambda i, f: (i, 0))
    vec = pl.BlockSpec((1, D_MODEL), lambda i, f: (0, 0))
    modspec = lambda slot: pl.BlockSpec((None, 1, D_MODEL), lambda i, f: ((i * tm // t) * 6 + slot, 0, 0))
    return pl.pallas_call(
        _ffn_body,
        grid=(n // tm, dff // tf),
        in_specs=[
            row, row,
            pl.BlockSpec((D_MODEL, tf), lambda i, f: (0, f)),
            pl.BlockSpec((D_MODEL, tf), lambda i, f: (0, f)),
            pl.BlockSpec((tf, D_MODEL), lambda i, f: (f, 0)),
            modspec(5), vec, vec, modspec(1), modspec(0),
        ],
        out_specs=[row, row],
        out_shape=[jax.ShapeDtypeStruct((n, D_MODEL), f32), jax.ShapeDtypeStruct((n, D_MODEL), bf16)],
        scratch_shapes=[pltpu.VMEM((tm, D_MODEL), f32)],
        compiler_params=_cparams("arbitrary", "arbitrary"),
        name="ffn",
    )(h, x, wg, wu, wd, mod, ln_g, ln_b, mod_next, mod_next)


def _moe_body(be_ref, live_ref, tok_ref, tokn_ref, dst_ref, gate_ref, h_hbm, wg_ref, wu_ref, wd_ref, y_hbm,
              xbuf, xb16, acc_ref, obuf, gsem, ssem):
    j = pl.program_id(0)
    f = pl.program_id(1)
    nb = pl.num_programs(0)
    nf = pl.num_programs(1)
    rows = xbuf.shape[0]
    live = live_ref[j] > 0
    next_live = live_ref[jnp.minimum(j + 1, nb - 1)] > 0
    prev_live = jnp.logical_and(j > 0, live_ref[jnp.maximum(j - 1, 0)] > 0)

    def gather(ref, static):
        def start(r, carry, priority=0):
            pltpu.make_async_copy(h_hbm.at[pl.ds(ref[0, 0, r], 1), :], xbuf.at[pl.ds(r, 1), :], gsem).start(priority)
            return carry

        if static:
            for r in range(rows):
                start(r, 0, r % 2)
        else:
            lax.fori_loop(0, rows, start, 0, unroll=8)

    def gather_wait():
        pltpu.make_async_copy(h_hbm.at[pl.ds(0, rows), :], xbuf, gsem).wait()

    def scatter_wait():
        pltpu.make_async_copy(obuf, y_hbm.at[pl.ds(0, rows), :], ssem).wait()

    @pl.when(jnp.logical_and(f == 0, j == 0))
    def _():
        obuf[...] = jnp.zeros(obuf.shape, f32)
        spare = pltpu.make_async_copy(obuf, y_hbm.at[pl.ds(y_hbm.shape[0] - rows, rows), :], ssem)
        spare.start()
        spare.wait()

    @pl.when(jnp.logical_and(f == 0, jnp.logical_and(j == 0, live)))
    def _():
        gather(tok_ref, static=False)

    @pl.when(jnp.logical_and(f == 0, live))
    def _():
        gather_wait()
        xb16[...] = xbuf[...].astype(bf16)
        acc_ref[...] = jnp.zeros(acc_ref.shape, f32)

    @pl.when(jnp.logical_and(f == 0, jnp.logical_and(live, jnp.logical_and(j + 1 < nb, next_live))))
    def _():
        gather(tokn_ref, static=True)

    half = rows // 2

    def swiglu(rs):
        x = xb16[rs, :]
        act = _silu(_dot(x, wg_ref[...])) * _dot(x, wu_ref[...])
        acc_ref[rs, :] += _dot(act, wd_ref[...])

    @pl.when(live_ref[j] > half)
    def _():
        swiglu(slice(0, rows))

    @pl.when(jnp.logical_and(live, live_ref[j] <= half))
    def _():
        swiglu(slice(0, half))

    @pl.when(jnp.logical_and(f == nf - 1, prev_live))
    def _():
        scatter_wait()

    @pl.when(jnp.logical_and(f == nf - 1, live))
    def _():
        obuf[...] = acc_ref[...] * gate_ref[...]
        for r in range(rows):
            pltpu.make_async_copy(obuf.at[pl.ds(r, 1), :], y_hbm.at[pl.ds(dst_ref[0, 0, r], 1), :], ssem).start(r % 2)

    @pl.when(jnp.logical_and(f == nf - 1, jnp.logical_and(j == nb - 1, live)))
    def _():
        scatter_wait()


def _moe(h, block_e, block_live, row_tok, row_dst, row_gate, wg, wu, wd):
    n = h.shape[0]
    n_blocks = block_e.shape[0]
    rb = MOE_ROWS
    dff = wg.shape[2]
    tf = 512
    smem_blk =pl.BlockSpec((1, 1, rb), lambda j, f, be, lv: (j, 0, 0), memory_space=pltpu.SMEM)
    smem_next = pl.BlockSpec((1, 1, rb), lambda j, f, be, lv: (jnp.minimum(j + 1, n_blocks - 1), 0, 0),
                             memory_space=pltpu.SMEM)
    grid_spec = pltpu.PrefetchScalarGridSpec(
        num_scalar_prefetch=2,
        grid=(n_blocks, dff // tf),
        in_specs=[
            smem_blk,
            smem_next,
            smem_blk,
            pl.BlockSpec((rb, 1), lambda j, f, be, lv: (j, 0)),
            pl.BlockSpec(memory_space=pl.ANY),
            pl.BlockSpec((None, D_MODEL, tf), lambda j, f, be, lv: (be[j], 0, f)),
            pl.BlockSpec((None, D_MODEL, tf), lambda j, f, be, lv: (be[j], 0, f)),
            pl.BlockSpec((None, tf, D_MODEL), lambda j, f, be, lv: (be[j], f, 0)),
        ],
        out_specs=pl.BlockSpec(memory_space=pl.ANY),
        scratch_shapes=[
            pltpu.VMEM((rb, D_MODEL), f32),
            pltpu.VMEM((rb, D_MODEL), bf16),
            pltpu.VMEM((rb, D_MODEL), f32),
            pltpu.VMEM((rb, D_MODEL), f32),
            pltpu.SemaphoreType.DMA,
            pltpu.SemaphoreType.DMA,
        ],
    )
    return pl.pallas_call(
        _moe_body,
        grid_spec=grid_spec,
        out_shape=jax.ShapeDtypeStruct((TOP_K * n + rb, D_MODEL), f32),
        compiler_params=_cparams("arbitrary", "arbitrary"),
        name="moe",
    )(block_e, block_live, row_tok, row_tok, row_dst, row_gate, h, wg, wu, wd)


def _routing_tables(top_e, top_g, n):
    rb = MOE_ROWS
    n_assign = n * TOP_K
    n_blocks = -(-n_assign // rb) + N_EXPERTS
    rows = n_blocks * rb
    e_flat = top_e.reshape(-1)
    onehot = (e_flat[:, None] == jnp.arange(N_EXPERTS, dtype=i32)[None, :]).astype(i32)
    rank = jnp.sum((jnp.cumsum(onehot, axis=0) - onehot) * onehot, axis=1)
    counts = jnp.sum(onehot, axis=0)
    padded = (counts + rb - 1) // rb * rb
    pad_end = jnp.cumsum(padded)
    pad_start = pad_end - padded
    dest = jnp.sum(onehot * pad_start[None, :], axis=1) + rank
    row_a = jnp.full((rows,), n_assign, i32).at[dest].set(jnp.arange(n_assign, dtype=i32), unique_indices=True)
    valid = row_a < n_assign
    tok, slot_k = row_a // TOP_K, row_a % TOP_K
    row_tok = jnp.where(valid, tok, 0)
    row_dst = jnp.where(valid, slot_k * n + tok, n_assign + jnp.arange(rows, dtype=i32) % rb)
    row_gate = jnp.where(valid, top_g.reshape(-1)[jnp.minimum(row_a, n_assign - 1)], 0.0)
    starts = jnp.arange(n_blocks, dtype=i32) * rb
    block_e = jnp.minimum(jnp.sum((pad_end[None, :] <= starts[:, None]).astype(i32), axis=1), N_EXPERTS - 1)
    block_live = jnp.clip(jnp.sum((jnp.arange(N_EXPERTS)[None, :] == block_e[:, None]) * (pad_start + counts)[None, :],
                                  axis=1) - starts, 0, rb).astype(i32)
    block_live = jnp.where(starts < pad_end[-1], block_live, 0)
    return (block_e, block_live, row_tok.reshape(n_blocks, 1, rb), row_dst.reshape(n_blocks, 1, rb),
            row_gate.reshape(rows, 1))


def _final_body(x_ref, y0_ref, y1_ref, gt_ref, g_ref, b_ref, o_ref):
    y = y0_ref[...] + y1_ref[...]
    o_ref[...] = _ln_rows(DEEPNORM_ALPHA * x_ref[...] + (1.0 + gt_ref[...]) * y, g_ref[...], b_ref[...])


def _final_ln(x, y2, mod, ln_g, ln_b, t):
    n = x.shape[0]
    tm = 512
    nb = n // tm
    return pl.pallas_call(
        _final_body,
        grid=(nb,),
        in_specs=[_row_spec(tm), _row_spec(tm), pl.BlockSpec((tm, D_MODEL), lambda i: (i + nb, 0)),
                  _mod_spec(t, tm, 5), _vec_spec(), _vec_spec()],
        out_specs=_row_spec(tm),
        out_shape=jax.ShapeDtypeStruct((n, D_MODEL), f32),
        compiler_params=_cparams("arbitrary"),
        name="final_ln",
    )(x, y2, y2, mod, ln_g, ln_b)


def _pad_cols(w, width):
    return jnp.pad(w, ((0, 0), (0, width - w.shape[1])))


def _pad_rows(w, height):
    return jnp.pad(w, ((0, height - w.shape[0]), (0, 0)))


def _regroup_w_in(w_in, v1):
    d = w_in.shape[0]
    gdn, ml, rw = 0, 2056, 3600
    smalls = jnp.concatenate([w_in[:, gdn + 2048 : gdn + 2056], w_in[:, ml + 1536 : ml + 1544]], axis=1)
    vdown = jnp.zeros((d, LORA_V), w_in.dtype) if v1 is None else v1
    misc = jnp.concatenate([w_in[:, rw + 1536 : rw + 1792], _pad_cols(vdown, 128), _pad_cols(smalls, 128)], axis=1)
    cat = jnp.concatenate(
        [w_in[:, gdn : gdn + 2048], w_in[:, ml : ml + 1536], w_in[:, rw : rw + 1536], w_in[:, 5392:8464]], axis=1)
    assert cat.shape[1] == Z_WIDTH and misc.shape[1] == MISC_WIDTH
    return cat.astype(bf16), misc.astype(bf16)


def _row(v, width=None):
    v = v.reshape(1, -1).astype(f32)
    return v if width is None else _pad_cols(v, width)


def kernel(x, c, ln_in_g, ln_in_b, ada_w, ada_b, w_in, gdn_conv, gdn_a_log, gdn_dt_bias, gdn_norm_g, mlstm_b_i, mlstm_b_f, mlstm_norm_g, rwkv_mu, rwkv_w0, rwkv_w2, rwkv_a0, rwkv_a2, rwkv_g2, rwkv_k_k, rwkv_k_a, rwkv_r_k, rwkv_lnx_g, rwkv_lnx_b, rwkv_v1, rwkv_mu_v1, rwkv_v0, rwkv_v2, w_branch, w_o, ln1_g, ln1_b, ln2_g, ln2_b, ffn_w_gate, ffn_w_up, ffn_w_down, moe_router, moe_router_b, moe_w_gate, moe_w_up, moe_w_down):
    b, t, d = x.shape
    n = b * t
    depth = w_in.shape[0]
    xf = x.reshape(n, d)

    c_pad = _pad_rows(c.astype(f32), 8)
    mods = []
    for layer in range(depth):
        mod = _ada_mod(c_pad, ada_w[layer].astype(bf16), _row(ada_b[layer]))[:b]
        mods.append(mod.reshape(b * 6, 1, d))

    xcur, h = _entry_ln(xf, _row(ln_in_g), _row(ln_in_b), mods[0], t)
    v_first = None
    out = None
    for layer in range(depth):
        mod = mods[layer]
        has_vres = layer > 0
        w_main, w_misc = _regroup_w_in(w_in[layer], rwkv_v1[layer - 1] if has_vres else None)
        z, zm = _in_proj(h, w_main, w_misc, 2048, 1024)

        gdn_par = jnp.zeros((8, LANES), f32).at[0, 0:4].set(gdn_a_log[layer]).at[1, 0:4].set(gdn_dt_bias[layer])
        ml_par = jnp.zeros((8, LANES), f32).at[0, SM_ML_I : SM_ML_I + 4].set(mlstm_b_i[layer])
        ml_par = ml_par.at[0, SM_ML_F : SM_ML_F + 4].set(mlstm_b_f[layer])

        mu = rwkv_mu[layer].astype(f32)
        mu_v = rwkv_mu_v1[layer - 1].astype(f32) if has_vres else jnp.zeros((LORA_V,), f32)
        mu_misc = _row(jnp.concatenate([mu[1536:1792], mu_v]), 512)
        vecs = jnp.stack([
            rwkv_w0[layer], rwkv_a0[layer], rwkv_k_k[layer], rwkv_k_a[layer], rwkv_r_k[layer].reshape(-1),
            rwkv_lnx_g[layer], rwkv_lnx_b[layer],
            rwkv_v0[layer - 1] if has_vres else jnp.zeros((RWKV_WIDTH,), f32),
        ]).astype(f32)
        w2p = _pad_rows(rwkv_w2[layer], 128).astype(bf16)
        a2p = jnp.concatenate([jnp.zeros((LORA_W, RWKV_WIDTH), f32), rwkv_a2[layer]], axis=0).astype(bf16)
        v2p = _pad_rows(rwkv_v2[layer - 1], 128).astype(bf16) if has_vres else None
        y_a, y_b, y_c, v_first = _mixers(
            z, zm, v_first,
            (gdn_conv[layer].astype(f32), gdn_par, _row(gdn_norm_g[layer])),
            (ml_par, _row(mlstm_norm_g[layer])),
            (_row(mu[:1536]), mu_misc, vecs, w2p, a2p, rwkv_g2[layer].astype(bf16), v2p), b, t)

        moe_layer = layer % 2 == 1
        i = layer // 2
        router = (_router_pack(moe_router[i]), _row(moe_router_b[i], LANES)) if moe_layer else None
        merged = _merge(y_a, y_b, y_c, z, xcur, w_branch[layer].astype(bf16), w_o[layer].astype(bf16), mod,
                        _row(ln1_g[layer]), _row(ln1_b[layer]), t, f32 if moe_layer else bf16, router)
        xcur, h2 = merged[:2]
        if not moe_layer:
            mod_next = mods[layer + 1] if layer + 1 < depth else mod
            xcur, h = _ffn(h2, xcur, ffn_w_gate[i].astype(bf16), ffn_w_up[i].astype(bf16), ffn_w_down[i].astype(bf16),
                           mod, mod_next, _row(ln2_g[layer]), _row(ln2_b[layer]), t)
            out = xcur
        else:
            idx, gate = merged[2:]
            tables = _routing_tables(idx[:, :TOP_K], gate[:, :TOP_K], n)
            y2 = _moe(h2, *tables, moe_w_gate[i], moe_w_up[i], moe_w_down[i])
            out = _final_ln(xcur, y2, mod, _row(ln2_g[layer]), _row(ln2_b[layer]), t)
            xcur = out
    return out.reshape(b, t, d)
```
